```python
import math, functools
import jax, jax.numpy as jnp
from jax import lax
import numpy as np

D_MODEL = 1024
BATCH = 16
SEQ = 2048
DEPTH = 2
DEC_BATCH = 32
DEC_SEQ = 8
PAST_LEN = 16384
PAGE_SIZE = 128

GDN_HEADS = 4
GDN_DK = 128
GDN_DV = 128
GDN_WIDTH = GDN_HEADS * GDN_DV
CONV_W = 4
CONV_CH = 3 * GDN_WIDTH
GDN_CHUNK = 64
FOX_HEADS = 8
FOX_HEAD_DIM = (D_MODEL - GDN_WIDTH) // FOX_HEADS
FOX_WIDTH = FOX_HEADS * FOX_HEAD_DIM
MIX_WIDTH = GDN_WIDTH + FOX_WIDTH
Q_BLOCK = 128
N_MEM = 256
XA_HEADS = 4
XA_HEAD_DIM = 128
XA_WIDTH = XA_HEADS * XA_HEAD_DIM
N_EXPERTS = 32
TOP_K = 4
D_FF = D_MODEL
SWIGLU_LIMIT = 7.0
SWIGLU_ALPHA = 1.702
MOE_BLOCK = 128
NORM_EPS = 1e-6
FORGET_BIAS = 4.0
IN_WIDTHS = (GDN_HEADS * GDN_DK, GDN_HEADS * GDN_DK, GDN_WIDTH, GDN_WIDTH, GDN_HEADS, GDN_HEADS, FOX_WIDTH, FOX_WIDTH, FOX_WIDTH, FOX_HEADS)
IN_COLS = 2 * GDN_HEADS * GDN_DK + 2 * GDN_WIDTH + 2 * GDN_HEADS + 3 * FOX_WIDTH + FOX_HEADS

kernel_name = 'hymba_gdn_fox_memxattn_moe_step'


def rmsnorm(x, g):
    xf = x.astype(jnp.float32)
    y = xf * lax.rsqrt(jnp.mean(xf * xf, axis=-1, keepdims=True) + NORM_EPS)
    return (y * g.astype(jnp.float32)).astype(x.dtype)


def l2norm(x):
    xf = x.astype(jnp.float32)
    return xf * lax.rsqrt(jnp.sum(xf * xf, axis=-1, keepdims=True) + NORM_EPS)


def split_cols(proj):
    idx, acc = [], 0
    for w in IN_WIDTHS[:-1]:
        acc += w
        idx.append(acc)
    return jnp.split(proj, idx, axis=-1)


def to_chunks(t, c, n):
    t = jnp.swapaxes(t, 1, 2)
    pad = n * c - t.shape[2]
    t = jnp.pad(t, [(0, 0), (0, 0), (0, pad)] + [(0, 0)] * (t.ndim - 3))
    return t.reshape(t.shape[:2] + (n, c) + t.shape[3:])


def gated_delta_chunked(q, k, v, g, beta, s0):
    b, l = q.shape[:2]
    c = min(GDN_CHUNK, l)
    n = -(-l // c)
    qc, kc, vc = to_chunks(q, c, n), to_chunks(k, c, n), to_chunks(v, c, n)
    gc, bc = to_chunks(g, c, n), to_chunks(beta, c, n)
    cum = jnp.cumsum(gc, axis=-1)
    diff = cum[..., :, None] - cum[..., None, :]
    pos = jnp.arange(c)
    strict = pos[:, None] > pos[None, :]
    incl = pos[:, None] >= pos[None, :]
    dec_strict = jnp.exp(jnp.where(strict, diff, -jnp.inf))
    dec_incl = jnp.exp(jnp.where(incl, diff, -jnp.inf))
    a_mat = bc[..., :, None] * jnp.einsum('bhnid,bhnjd->bhnij', kc, kc) * dec_strict
    ia = a_mat + jnp.eye(c, dtype=a_mat.dtype)
    e_cum = jnp.exp(cum)
    rhs = jnp.concatenate([bc[..., None] * vc, (bc * e_cum)[..., None] * kc], axis=-1)
    sol = lax.linalg.triangular_solve(ia, rhs, left_side=True, lower=True)
    w_v, w_k = sol[..., :GDN_DV], sol[..., GDN_DV:]
    p_qk = jnp.einsum('bhnid,bhnjd->bhnij', qc, kc) * dec_incl
    q_g = qc * e_cum[..., None]
    k_d = kc * jnp.exp(cum[..., -1:] - cum)[..., None]
    g_end = jnp.exp(cum[..., -1])

    def step(s, inp):
        wv, wk, p, qg, kd, ge = inp
        u = wv - jnp.einsum('bhck,bhkv->bhcv', wk, s)
        o = jnp.einsum('bhck,bhkv->bhcv', qg, s) + jnp.einsum('bhij,bhjv->bhiv', p, u)
        s = ge[..., None, None] * s + jnp.einsum('bhck,bhcv->bhkv', kd, u)
        return s, o

    xs = tuple(jnp.moveaxis(t, 2, 0) for t in (w_v, w_k, p_qk, q_g, k_d, g_end))
    s_fin, o = lax.scan(step, s0, xs)
    o = jnp.moveaxis(o, 0, 2).reshape(b, GDN_HEADS, n * c, GDN_DV)[:, :, :l]
    return jnp.swapaxes(o, 1, 2), s_fin


def fox_prompt(fq, fk, fv, logf):
    b, l = fq.shape[:2]
    nb = l // Q_BLOCK
    scale = FOX_HEAD_DIM ** -0.5
    c_t = jnp.swapaxes(jnp.cumsum(logf, axis=1), 1, 2)
    qb = jnp.moveaxis(fq.reshape(b, nb, Q_BLOCK, FOX_HEADS, FOX_HEAD_DIM), 1, 0)
    cb = jnp.moveaxis(c_t.reshape(b, FOX_HEADS, nb, Q_BLOCK), 2, 0)
    qpos = jnp.arange(l).reshape(nb, Q_BLOCK)
    kpos = jnp.arange(l)

    def block(args):
        q, cq, qp = args
        s = jnp.einsum('bqhd,bkhd->bhqk', q, fk).astype(jnp.float32) * scale
        s = s + cq[..., :, None] - c_t[:, :, None, :]
        s = jnp.where(kpos[None, :] <= qp[:, None], s, -jnp.inf)
        p = jax.nn.softmax(s, axis=-1).astype(fv.dtype)
        return jnp.einsum('bhqk,bkhd->bqhd', p, fv)

    o = lax.map(block, (qb, cb, qpos))
    return jnp.moveaxis(o, 0, 1).reshape(b, l, FOX_HEADS, FOX_HEAD_DIM)


def fox_sample(fq, fk, fv, logf, k_past, v_past, logf_past):
    scale = FOX_HEAD_DIM ** -0.5
    p_len = k_past.shape[1]
    lfp = logf_past.astype(jnp.float32)
    r_past = jnp.swapaxes(lax.cumsum(lfp, axis=1, reverse=True) - lfp, 1, 2)
    c_new = jnp.swapaxes(jnp.cumsum(logf, axis=1), 1, 2)
    s_past = jnp.einsum('bqhd,bphd->bhqp', fq, k_past).astype(jnp.float32) * scale
    s_past = s_past + r_past[:, :, None, :] + c_new[..., :, None]
    s_new = jnp.einsum('bqhd,bkhd->bhqk', fq, fk).astype(jnp.float32) * scale
    s_new = s_new + c_new[..., :, None] - c_new[..., None, :]
    pos = jnp.arange(fq.shape[1])
    s_new = jnp.where(pos[None, :] <= pos[:, None], s_new, -jnp.inf)
    p = jax.nn.softmax(jnp.concatenate([s_past, s_new], axis=-1), axis=-1).astype(fv.dtype)
    return (jnp.einsum('bhqp,bphd->bqhd', p[..., :p_len], v_past)
            + jnp.einsum('bhqk,bkhd->bqhd', p[..., p_len:], fv))


def mixer_sublayer(h, conv_buf, s0, w_in, conv_w, a_log, dt_bias, gdn_onorm, fox_fbias, fox_onorm, w_out, fox_attend):
    b, l, _ = h.shape
    gq, gk, gv, gz, gb, ga, fq, fk, fv, ff = split_cols(h @ w_in)
    qkv = jnp.concatenate([gq, gk, gv], axis=-1)
    xpad = jnp.concatenate([conv_buf.astype(qkv.dtype), qkv], axis=1)
    conv = xpad[:, 0:l] * conv_w[0]
    for i in range(1, CONV_W):
        conv = conv + xpad[:, i:i + l] * conv_w[i]
    conv = jax.nn.silu(conv)
    new_buf = xpad[:, l:]
    cq, ck, cv = jnp.split(conv, 3, axis=-1)
    q = l2norm(cq.reshape(b, l, GDN_HEADS, GDN_DK)) * (GDN_DK ** -0.5)
    k = l2norm(ck.reshape(b, l, GDN_HEADS, GDN_DK))
    v = cv.reshape(b, l, GDN_HEADS, GDN_DV).astype(jnp.float32)
    beta = jax.nn.sigmoid(gb.astype(jnp.float32))
    g = -jnp.exp(a_log.astype(jnp.float32)) * jax.nn.softplus(ga.astype(jnp.float32) + dt_bias.astype(jnp.float32))
    o, s_new = gated_delta_chunked(q, k, v, g, beta, s0.astype(jnp.float32))
    o = rmsnorm(o, gdn_onorm) * jax.nn.silu(gz.reshape(b, l, GDN_HEADS, GDN_DV).astype(jnp.float32))
    gdn_out = o.reshape(b, l, GDN_WIDTH).astype(h.dtype)
    fq = fq.reshape(b, l, FOX_HEADS, FOX_HEAD_DIM)
    fk = fk.reshape(b, l, FOX_HEADS, FOX_HEAD_DIM)
    fv = fv.reshape(b, l, FOX_HEADS, FOX_HEAD_DIM)
    logf = jax.nn.log_sigmoid(ff.astype(jnp.float32) + fox_fbias.astype(jnp.float32))
    fo = rmsnorm(fox_attend(fq, fk, fv, logf), fox_onorm)
    fox_out = fo.reshape(b, l, FOX_WIDTH).astype(h.dtype)
    out = jnp.concatenate([gdn_out, fox_out], axis=-1) @ w_out
    return out, s_new, new_buf, fk, fv, logf


def mem_kv(mem, g, w_mkv):
    b, m, _ = mem.shape
    mk, mv = jnp.split(rmsnorm(mem, g) @ w_mkv, 2, axis=-1)
    return mk.reshape(b, m, XA_HEADS, XA_HEAD_DIM), mv.reshape(b, m, XA_HEADS, XA_HEAD_DIM)


def cross_attn(h, w_xq, w_xo, mk, mv):
    b, l, _ = h.shape
    q = (h @ w_xq).reshape(b, l, XA_HEADS, XA_HEAD_DIM)
    s = jnp.einsum('bqhd,bmhd->bhqm', q, mk).astype(jnp.float32) * (XA_HEAD_DIM ** -0.5)
    p = jax.nn.softmax(s, axis=-1).astype(mv.dtype)
    o = jnp.einsum('bhqm,bmhd->bqhd', p, mv).reshape(b, l, XA_WIDTH)
    return o @ w_xo


def moe(h, w_router, b_router, w1, b1, w2, b2):
    b, l, d = h.shape
    t = b * l
    x2 = h.reshape(t, d)
    logits = (x2 @ w_router).astype(jnp.float32) + b_router.astype(jnp.float32)
    top_val, top_idx = lax.top_k(logits, TOP_K)
    gates = jax.nn.softmax(top_val, axis=-1)
    flat_e = top_idx.reshape(-1)
    flat_tok = jnp.arange(t * TOP_K, dtype=jnp.int32) // TOP_K
    flat_g = gates.reshape(-1)
    order = jnp.argsort(flat_e)
    se = flat_e[order]
    counts = jnp.zeros((N_EXPERTS,), jnp.int32).at[flat_e].add(1)
    starts = jnp.cumsum(counts) - counts
    padded = (counts + MOE_BLOCK - 1) // MOE_BLOCK * MOE_BLOCK
    pends = jnp.cumsum(padded)
    dest = (pends - padded)[se] + jnp.arange(t * TOP_K, dtype=jnp.int32) - starts[se]
    n_blocks = -(-(t * TOP_K) // MOE_BLOCK) + N_EXPERTS
    rows = n_blocks * MOE_BLOCK
    tok_buf = jnp.zeros((rows,), jnp.int32).at[dest].set(flat_tok[order])
    g_buf = jnp.zeros((rows,), jnp.float32).at[dest].set(flat_g[order])
    block_e = jnp.minimum(jnp.searchsorted(pends, jnp.arange(n_blocks, dtype=jnp.int32) * MOE_BLOCK, side='right'), N_EXPERTS - 1)

    def expert_block(args):
        tok, e = args
        hb = (x2[tok] @ w1[e] + b1[e]).astype(jnp.float32)
        glu, lin = jnp.split(hb, 2, axis=-1)
        glu = jnp.minimum(glu, SWIGLU_LIMIT)
        lin = jnp.clip(lin, -SWIGLU_LIMIT, SWIGLU_LIMIT)
        act = glu * jax.nn.sigmoid(SWIGLU_ALPHA * glu) * (lin + 1.0)
        return act.astype(h.dtype) @ w2[e] + b2[e]

    out = lax.map(expert_block, (tok_buf.reshape(n_blocks, MOE_BLOCK), block_e))
    y = jax.ops.segment_sum(out.reshape(rows, d).astype(jnp.float32) * g_buf[:, None], tok_buf, num_segments=t)
    return y.astype(h.dtype).reshape(b, l, d)


def setup_inputs(seed: int = 0) -> dict:
    key = jax.random.key(seed)
    ks = jax.random.split(key, 36)

    def nrm(i, shape, scale):
        return scale * jax.random.normal(ks[i], shape, jnp.float32)

    def gain(i, shape):
        return 1.0 + nrm(i, shape, 0.02)

    n_pages = PAST_LEN // PAGE_SIZE
    n_used = DEC_BATCH * n_pages
    n_pool = n_used + max(1, n_used // 4)
    page_table = jax.random.permutation(ks[10], n_pool)[:n_used].reshape(DEC_BATCH, n_pages).astype(jnp.int32)
    dt = jnp.exp(jax.random.uniform(ks[15], (DEPTH, GDN_HEADS), jnp.float32, math.log(1e-3), math.log(1e-1)))
    return {
        'x_prompt': nrm(0, (BATCH, SEQ, D_MODEL), 1.0),
        'x_sample': nrm(1, (DEC_BATCH, DEC_SEQ, D_MODEL), 1.0),
        'mem_prompt': nrm(2, (BATCH, N_MEM, D_MODEL), 1.0),
        'cache_fox_k': nrm(3, (DEPTH, n_pool, PAGE_SIZE, FOX_HEADS, FOX_HEAD_DIM), 1.0),
        'cache_fox_v': nrm(4, (DEPTH, n_pool, PAGE_SIZE, FOX_HEADS, FOX_HEAD_DIM), 1.0),
        'cache_fox_logf': jax.nn.log_sigmoid(FORGET_BIAS + nrm(5, (DEPTH, n_pool, PAGE_SIZE, FOX_HEADS), 1.0)),
        'cache_mem_k': nrm(6, (DEPTH, DEC_BATCH, N_MEM, XA_HEADS, XA_HEAD_DIM), 1.0),
        'cache_mem_v': nrm(7, (DEPTH, DEC_BATCH, N_MEM, XA_HEADS, XA_HEAD_DIM), 1.0),
        'state_gdn': nrm(8, (DEPTH, DEC_BATCH, GDN_HEADS, GDN_DK, GDN_DV), 0.1),
        'state_conv': nrm(9, (DEPTH, DEC_BATCH, CONV_W - 1, CONV_CH), 1.0),
        'page_table': page_table,
        'norm1_g': gain(11, (DEPTH, D_MODEL)),
        'w_in': nrm(12, (DEPTH, D_MODEL, IN_COLS), D_MODEL ** -0.5),
        'conv_w': nrm(13, (DEPTH, CONV_W, CONV_CH), CONV_W ** -0.5),
        'gdn_a_log': jnp.log(jax.random.uniform(ks[14], (DEPTH, GDN_HEADS), jnp.float32, 1.0, 16.0)),
        'gdn_dt_bias': dt + jnp.log(-jnp.expm1(-dt)),
        'gdn_onorm': gain(16, (DEPTH, GDN_DV)),
        'fox_fbias': FORGET_BIAS + nrm(17, (DEPTH, FOX_HEADS), 0.5),
        'fox_onorm': gain(18, (DEPTH, FOX_HEAD_DIM)),
        'w_out': nrm(19, (DEPTH, MIX_WIDTH, D_MODEL), 0.5 * MIX_WIDTH ** -0.5),
        'norm2_g': gain(20, (DEPTH, D_MODEL)),
        'mem_norm_g': gain(21, (DEPTH, D_MODEL)),
        'w_xq': nrm(22, (DEPTH, D_MODEL, XA_WIDTH), D_MODEL ** -0.5),
        'w_mkv': nrm(23, (DEPTH, D_MODEL, 2 * XA_WIDTH), D_MODEL ** -0.5),
        'w_xo': nrm(24, (DEPTH, XA_WIDTH, D_MODEL), 0.5 * XA_WIDTH ** -0.5),
        'norm3_g': gain(25, (DEPTH, D_MODEL)),
        'w_router': nrm(26, (DEPTH, D_MODEL, N_EXPERTS), D_MODEL ** -0.5),
        'b_router': nrm(27, (DEPTH, N_EXPERTS), 0.01),
        'w1': nrm(28, (DEPTH, N_EXPERTS, D_MODEL, 2 * D_FF), D_MODEL ** -0.5),
        'b1': nrm(29, (DEPTH, N_EXPERTS, 2 * D_FF), 0.01),
        'w2': nrm(30, (DEPTH, N_EXPERTS, D_FF, D_MODEL), 0.5 * D_FF ** -0.5),
        'b2': nrm(31, (DEPTH, N_EXPERTS, D_MODEL), 0.01),
        'final_norm_g': gain(32, (D_MODEL,)),
    }


def reference(x_prompt, x_sample, mem_prompt, cache_fox_k, cache_fox_v, cache_fox_logf, cache_mem_k, cache_mem_v, state_gdn, state_conv, page_table, norm1_g, w_in, conv_w, gdn_a_log, gdn_dt_bias, gdn_onorm, fox_fbias, fox_onorm, w_out, norm2_g, mem_norm_g, w_xq, w_mkv, w_xo, norm3_g, w_router, b_router, w1, b1, w2, b2, final_norm_g):
    bp = x_prompt.shape[0]
    bs = x_sample.shape[0]
    past = page_table.shape[1] * cache_fox_k.shape[2]
    xp, xs = x_prompt, x_sample
    fkp, fvp, flp, mkp, mvp, sgp, scp = [], [], [], [], [], [], []
    fks, fvs, fls, sgs, scs = [], [], [], [], []
    for l in range(DEPTH):
        mix_w = (w_in[l], conv_w[l], gdn_a_log[l], gdn_dt_bias[l], gdn_onorm[l], fox_fbias[l], fox_onorm[l], w_out[l])
        buf0 = jnp.zeros((bp, CONV_W - 1, CONV_CH), x_prompt.dtype)
        s0 = jnp.zeros((bp, GDN_HEADS, GDN_DK, GDN_DV), jnp.float32)
        o, s_new, cbuf, k_new, v_new, lf_new = mixer_sublayer(rmsnorm(xp, norm1_g[l]), buf0, s0, *mix_w, fox_prompt)
        xp = xp + o
        fkp.append(k_new); fvp.append(v_new); flp.append(lf_new); sgp.append(s_new); scp.append(cbuf)
        k_past = cache_fox_k[l, page_table].reshape(bs, past, FOX_HEADS, FOX_HEAD_DIM)
        v_past = cache_fox_v[l, page_table].reshape(bs, past, FOX_HEADS, FOX_HEAD_DIM)
        lf_past = cache_fox_logf[l, page_table].reshape(bs, past, FOX_HEADS)
        attend = functools.partial(fox_sample, k_past=k_past, v_past=v_past, logf_past=lf_past)
        o, s_new, cbuf, k_new, v_new, lf_new = mixer_sublayer(rmsnorm(xs, norm1_g[l]), state_conv[l], state_gdn[l], *mix_w, attend)
        xs = xs + o
        fks.append(k_new); fvs.append(v_new); fls.append(lf_new); sgs.append(s_new); scs.append(cbuf)
        mk, mv = mem_kv(mem_prompt, mem_norm_g[l], w_mkv[l])
        mkp.append(mk); mvp.append(mv)
        xp = xp + cross_attn(rmsnorm(xp, norm2_g[l]), w_xq[l], w_xo[l], mk, mv)
        xs = xs + cross_attn(rmsnorm(xs, norm2_g[l]), w_xq[l], w_xo[l], cache_mem_k[l], cache_mem_v[l])
        moe_w = (w_router[l], b_router[l], w1[l], b1[l], w2[l], b2[l])
        xp = xp + moe(rmsnorm(xp, norm3_g[l]), *moe_w)
        xs = xs + moe(rmsnorm(xs, norm3_g[l]), *moe_w)
    y_prompt = rmsnorm(xp, final_norm_g)
    y_sample = rmsnorm(xs, final_norm_g)
    return (y_prompt, y_sample, jnp.stack(fkp), jnp.stack(fvp), jnp.stack(flp), jnp.stack(mkp), jnp.stack(mvp), jnp.stack(sgp), jnp.stack(scp), jnp.stack(fks), jnp.stack(fvs), jnp.stack(fls), jnp.stack(sgs), jnp.stack(scs))
```

```python
import functools
import math

import jax
import jax.numpy as jnp
from jax import lax
from jax.experimental import pallas as pl
from jax.experimental.pallas import tpu as pltpu

F32 = jnp.float32
BF16 = jnp.bfloat16
I32 = jnp.int32
HIGHEST = lax.Precision.HIGHEST

D_MODEL = 1024
GDN_HEADS = 4
GDN_DK = 128
GDN_DV = 128
GDN_WIDTH = GDN_HEADS * GDN_DV
CONV_W = 4
CONV_CH = 3 * GDN_WIDTH
GDN_CHUNK = 64
FOX_HEADS = 8
FOX_HEAD_DIM = 64
FOX_WIDTH = FOX_HEADS * FOX_HEAD_DIM
XA_HEADS = 4
XA_HEAD_DIM = 128
XA_WIDTH = XA_HEADS * XA_HEAD_DIM
N_EXPERTS = 32
TOP_K = 4
D_FF = D_MODEL
SWIGLU_LIMIT = 7.0
SWIGLU_ALPHA = 1.702
NORM_EPS = 1e-6
NEG_BIG = -1e30

LANES = 128
SUBLANES = 8
VMEM_LIMIT = 52 * 1024 * 1024
MAIN_COLS = 2 * GDN_WIDTH + 2 * GDN_WIDTH + 3 * FOX_WIDTH
EXPERT_BLOCK = 256


def _cparams(sem):
    return pltpu.CompilerParams(dimension_semantics=sem, vmem_limit_bytes=VMEM_LIMIT)


def _dot(a, b):
    return jnp.dot(a.astype(BF16), b.astype(BF16), preferred_element_type=F32)


def _dot_nt(a, b):
    return lax.dot_general(a.astype(BF16), b.astype(BF16), (((1,), (1,)), ((), ())),
                           preferred_element_type=F32)


def _dot_tn(a, b):
    return lax.dot_general(a.astype(BF16), b.astype(BF16), (((0,), (0,)), ((), ())),
                           preferred_element_type=F32)


def _dot_hi(a, b):
    return jnp.dot(a, b, precision=HIGHEST, preferred_element_type=F32)


def _dot_nt_hi(a, b):
    return lax.dot_general(a, b, (((1,), (1,)), ((), ())), precision=HIGHEST,
                           preferred_element_type=F32)


def _rms(x, g):
    return x * lax.rsqrt(jnp.mean(x * x, axis=-1, keepdims=True) + NORM_EPS) * g


def _sigmoid(x):
    return 1.0 / (1.0 + jnp.exp(-x))


def _softplus(x):
    return jnp.maximum(x, 0.0) + jnp.log1p(jnp.exp(-jnp.abs(x)))


def _log_sigmoid(x):
    return jnp.minimum(x, 0.0) - jnp.log1p(jnp.exp(-jnp.abs(x)))


def _iota2(shape, dim):
    return lax.broadcasted_iota(I32, shape, dim)


def _in_proj_kernel(x_ref, g_ref, wm_ref, ws_ref, wst_ref,
                    qkv_ref, z_ref, fq_ref, fk_ref, fv_ref, sm_ref, smt_ref):
    hb = _rms(x_ref[...], g_ref[...]).astype(BF16)
    c0 = 0
    for ref, width in ((qkv_ref, CONV_CH), (z_ref, GDN_WIDTH), (fq_ref, FOX_WIDTH),
                       (fk_ref, FOX_WIDTH), (fv_ref, FOX_WIDTH)):
        ref[...] = jnp.dot(hb, wm_ref[:, c0:c0 + width], preferred_element_type=F32)
        c0 += width
    sm_ref[...] = jnp.dot(hb, ws_ref[...], preferred_element_type=F32)
    smt_ref[...] = lax.dot_general(wst_ref[...], hb, (((1,), (1,)), ((), ())),
                                   preferred_element_type=F32)


def _in_proj(x2, g, wm, ws, wst, tm):
    t = x2.shape[0]
    row = lambda w: pl.BlockSpec((tm, w), lambda i: (i, 0))
    full = lambda a: pl.BlockSpec(a.shape, lambda i: (0,) * a.ndim)
    return pl.pallas_call(
        _in_proj_kernel,
        grid=(t // tm,),
        in_specs=[row(D_MODEL), full(g), full(wm), full(ws), full(wst)],
        out_specs=[row(CONV_CH), row(GDN_WIDTH), row(FOX_WIDTH), row(FOX_WIDTH), row(FOX_WIDTH),
                   row(LANES), pl.BlockSpec((16, tm), lambda i: (0, i))],
        out_shape=[jax.ShapeDtypeStruct((t, CONV_CH), F32), jax.ShapeDtypeStruct((t, GDN_WIDTH), F32),
                   jax.ShapeDtypeStruct((t, FOX_WIDTH), F32), jax.ShapeDtypeStruct((t, FOX_WIDTH), F32),
                   jax.ShapeDtypeStruct((t, FOX_WIDTH), F32), jax.ShapeDtypeStruct((t, LANES), F32),
                   jax.ShapeDtypeStruct((16, t), F32)],
        compiler_params=_cparams(("parallel",)),
    )(x2, g, wm, ws, wst)


def _norm_proj_kernel(x_ref, g_ref, w_ref, *out_refs):
    hb = _rms(x_ref[...], g_ref[...]).astype(BF16)
    c0 = 0
    for ref in out_refs:
        width = ref.shape[-1]
        ref[...] = jnp.dot(hb, w_ref[:, c0:c0 + width], preferred_element_type=F32)
        c0 += width


def _norm_proj(x2, g, w, widths, tm):
    t = x2.shape[0]
    return pl.pallas_call(
        _norm_proj_kernel,
        grid=(t // tm,),
        in_specs=[pl.BlockSpec((tm, D_MODEL), lambda i: (i, 0)),
                  pl.BlockSpec(g.shape, lambda i: (0, 0)),
                  pl.BlockSpec(w.shape, lambda i: (0, 0))],
        out_specs=[pl.BlockSpec((tm, wd), lambda i: (i, 0)) for wd in widths],
        out_shape=[jax.ShapeDtypeStruct((t, wd), F32) for wd in widths],
        compiler_params=_cparams(("parallel",)),
    )(x2, g, w)


def _small_mm(a, b):
    acc = a[:, 0:1] * b[0:1, :]
    for i in range(1, a.shape[1]):
        acc = acc + a[:, i:i + 1] * b[i:i + 1, :]
    return acc


def _logf_kernel(sm_ref, smt_ref, fbr_ref, fbc_ref, logf_ref, ct_ref, *, seq, chunk):
    ff = sm_ref[0][:, 8:16] + fbr_ref[:, 8:16]
    logf_ref[0] = _log_sigmoid(ff)
    lft = _log_sigmoid(smt_ref[0][8:16, :] + fbc_ref[8:16, 0:1])
    tri = (_iota2((chunk, chunk), 0) <= _iota2((chunk, chunk), 1)).astype(F32)
    carry = jnp.zeros((FOX_HEADS, 1), F32)
    for c in range(seq // chunk):
        blk = lft[:, c * chunk:(c + 1) * chunk]
        cs = (_small_mm(blk, tri) if chunk <= SUBLANES else _dot_hi(blk, tri)) + carry
        ct_ref[0, :, c * chunk:(c + 1) * chunk] = cs
        carry = cs[:, chunk - 1:chunk]


def _logf(sm3, smt3, fbr, fbc):
    b, seq, _ = sm3.shape
    chunk = min(256, seq)
    return pl.pallas_call(
        functools.partial(_logf_kernel, seq=seq, chunk=chunk),
        grid=(b,),
        in_specs=[pl.BlockSpec((1, seq, LANES), lambda i: (i, 0, 0)),
                  pl.BlockSpec((1, 16, seq), lambda i: (i, 0, 0)),
                  pl.BlockSpec(fbr.shape, lambda i: (0, 0)),
                  pl.BlockSpec(fbc.shape, lambda i: (0, 0))],
        out_specs=[pl.BlockSpec((1, seq, FOX_HEADS), lambda i: (i, 0, 0)),
                   pl.BlockSpec((1, FOX_HEADS, seq), lambda i: (i, 0, 0))],
        out_shape=[jax.ShapeDtypeStruct((b, seq, FOX_HEADS), F32),
                   jax.ShapeDtypeStruct((b, FOX_HEADS, seq), F32)],
        compiler_params=_cparams(("parallel",)),
    )(sm3, smt3, fbr, fbc)


def _unit_lower_inverse(a, c, mm):
    eye = (_iota2((c, c), 0) == _iota2((c, c), 1)).astype(F32)
    n = -a
    t = eye + n
    p = n
    levels = int(math.log2(c))
    for _ in range(levels - 1):
        p = mm(p, p)
        t = t + mm(t, p)
    return t


def _gdn_kernel(qkv_ref, z_ref, sm_ref, smt_ref, cst_ref, s0_ref, cw_ref, gpr_ref, gpc_ref,
                o_ref, sn_ref, cb_ref, s_scr, tail_scr, *, tl, c, exact_small):
    t = pl.program_id(1)
    nt = pl.num_programs(1)

    @pl.when(t == 0)
    def _():
        s_scr[...] = s0_ref[0]
        tail_scr[...] = cst_ref[0]

    if exact_small:
        mm = lambda a, b: jnp.dot(a, b, preferred_element_type=F32)
        mm_nt = lambda a, b: lax.dot_general(a, b, (((1,), (1,)), ((), ())), preferred_element_type=F32)
        mm_tn = lambda a, b: lax.dot_general(a, b, (((0,), (0,)), ((), ())), preferred_element_type=F32)
        rb = lambda v: v.astype(BF16).astype(F32)
    else:
        mm, mm_nt, mm_tn = _dot, _dot_nt, _dot_tn
        rb = lambda v: v

    x = qkv_ref[0]
    tail = tail_scr[...]
    cw = cw_ref[...]
    row8 = _iota2((SUBLANES, CONV_CH), 0)
    acc = x * cw[CONV_W - 1:CONV_W, :]
    for s in range(1, CONV_W):
        xs = pltpu.roll(x, s, 0)
        head = jnp.where(row8 < s, pltpu.roll(tail, s, 0), xs[0:SUBLANES])
        xs = head if tl == SUBLANES else jnp.concatenate([head, xs[SUBLANES:]], axis=0)
        acc = acc + xs * cw[CONV_W - 1 - s:CONV_W - s, :]
    conv = acc * _sigmoid(acc)
    tail_scr[...] = x[tl - SUBLANES:tl, :]

    @pl.when(t == nt - 1)
    def _():
        cb_ref[0] = x[tl - (CONV_W - 1):tl, :]

    sm = sm_ref[0]
    beta_c = _sigmoid(sm)
    g_c = -jnp.exp(gpr_ref[0:1, :]) * _softplus(sm + gpr_ref[1:2, :])
    smt = smt_ref[0]
    g_r = -jnp.exp(gpc_ref[:, 0:1]) * _softplus(smt + gpc_ref[:, 1:2])
    onorm = gpr_ref[2:3, :]

    ii = _iota2((c, c), 0)
    jj = _iota2((c, c), 1)
    tri_c = (jj <= ii).astype(F32)
    tri_r = (ii <= jj).astype(F32)
    small = c <= SUBLANES

    for ci in range(tl // c):
        r0 = ci * c
        gc_blk = g_c[r0:r0 + c, :]
        gr_blk = g_r[:, r0:r0 + c]
        cum_c_all = _small_mm(tri_c, gc_blk) if small else _dot_hi(tri_c, gc_blk)
        cum_r_all = _small_mm(gr_blk, tri_r) if small else _dot_hi(gr_blk, tri_r)
        for h in range(GDN_HEADS):
            lo = h * GDN_DK
            qh = conv[r0:r0 + c, lo:lo + GDN_DK]
            kh = conv[r0:r0 + c, GDN_WIDTH + lo:GDN_WIDTH + lo + GDN_DK]
            vh = conv[r0:r0 + c, 2 * GDN_WIDTH + lo:2 * GDN_WIDTH + lo + GDN_DV]
            qh = qh * lax.rsqrt(jnp.sum(qh * qh, axis=-1, keepdims=True) + NORM_EPS) * (GDN_DK ** -0.5)
            kh = kh * lax.rsqrt(jnp.sum(kh * kh, axis=-1, keepdims=True) + NORM_EPS)
            beta = beta_c[r0:r0 + c, h:h + 1]
            cum_c = cum_c_all[:, GDN_HEADS + h:GDN_HEADS + h + 1]
            cum_r = cum_r_all[GDN_HEADS + h:GDN_HEADS + h + 1, :]
            cum_last = cum_c[c - 1:c, :]
            dec = jnp.exp(jnp.where(ii >= jj, cum_c - cum_r, NEG_BIG))
            dec_strict = jnp.where(ii > jj, dec, 0.0)
            kq, kk, vv = rb(qh), rb(kh), vh
            a_mat = beta * mm_nt(kk, kk) * dec_strict
            tinv = _unit_lower_inverse(a_mat, c, lambda u, w: mm(rb(u), rb(w)))
            e_cum = jnp.exp(cum_c)
            rhs = jnp.concatenate([beta * vv, (beta * e_cum) * kh], axis=-1)
            sol = mm(rb(tinv), rb(rhs))
            w_v, w_k = sol[:, :GDN_DV], sol[:, GDN_DV:]
            p_qk = mm_nt(kq, kk) * dec
            q_g = qh * e_cum
            k_d = kh * jnp.exp(cum_last - cum_c)
            s_h = s_scr[h]
            s_m = rb(s_h)
            u = w_v - mm(rb(w_k), s_m)
            o = mm(rb(q_g), s_m) + mm(rb(p_qk), rb(u))
            s_scr[h] = jnp.exp(cum_last) * s_h + mm_tn(rb(k_d), rb(u))
            zh = z_ref[0, r0:r0 + c, lo:lo + GDN_DV]
            o = _rms(o, onorm) * (zh * _sigmoid(zh))
            o_ref[0, r0:r0 + c, lo:lo + GDN_DV] = o

    @pl.when(t == nt - 1)
    def _():
        sn_ref[0] = s_scr[...]


def _gdn(qkv3, z3, sm3, smt3, cst8, s0, cw, gpr, gpc, tl, c, exact_small):
    b, seq, _ = qkv3.shape
    nt = seq // tl
    full2 = lambda a: pl.BlockSpec(a.shape, lambda i, j: (0, 0))
    return pl.pallas_call(
        functools.partial(_gdn_kernel, tl=tl, c=c, exact_small=exact_small),
        grid=(b, nt),
        in_specs=[pl.BlockSpec((1, tl, CONV_CH), lambda i, j: (i, j, 0)),
                  pl.BlockSpec((1, tl, GDN_WIDTH), lambda i, j: (i, j, 0)),
                  pl.BlockSpec((1, tl, LANES), lambda i, j: (i, j, 0)),
                  pl.BlockSpec((1, 16, tl), lambda i, j: (i, 0, j)),
                  pl.BlockSpec((1, SUBLANES, CONV_CH), lambda i, j: (i, 0, 0)),
                  pl.BlockSpec((1, GDN_HEADS, GDN_DK, GDN_DV), lambda i, j: (i, 0, 0, 0)),
                  full2(cw), full2(gpr), full2(gpc)],
        out_specs=[pl.BlockSpec((1, tl, GDN_WIDTH), lambda i, j: (i, j, 0)),
                   pl.BlockSpec((1, GDN_HEADS, GDN_DK, GDN_DV), lambda i, j: (i, 0, 0, 0)),
                   pl.BlockSpec((1, CONV_W - 1, CONV_CH), lambda i, j: (i, 0, 0))],
        out_shape=[jax.ShapeDtypeStruct((b, seq, GDN_WIDTH), F32),
                   jax.ShapeDtypeStruct((b, GDN_HEADS, GDN_DK, GDN_DV), F32),
                   jax.ShapeDtypeStruct((b, CONV_W - 1, CONV_CH), F32)],
        scratch_shapes=[pltpu.VMEM((GDN_HEADS, GDN_DK, GDN_DV), F32),
                        pltpu.VMEM((SUBLANES, CONV_CH), F32)],
        compiler_params=_cparams(("parallel", "arbitrary")),
    )(qkv3, z3, sm3, smt3, cst8, s0, cw, gpr, gpc)


def _fox_prompt_kernel(fq_ref, fk_ref, fv_ref, ct_ref, on_ref, o_ref, kb_scr, vb_scr, *, tq):
    qi = pl.program_id(1)

    @pl.when(qi == 0)
    def _():
        kb_scr[...] = fk_ref[0].astype(BF16)
        vb_scr[...] = fv_ref[0].astype(BF16)

    q_all = fq_ref[0] * (FOX_HEAD_DIM ** -0.5)
    onorm = on_ref[...]
    causal = _iota2((tq, tq), 1) <= _iota2((tq, tq), 0)
    outs = []
    for h in range(FOX_HEADS):
        lo = h * FOX_HEAD_DIM
        qh = q_all[:, lo:lo + FOX_HEAD_DIM].astype(BF16)

        def step(j, carry, masked, lo=lo, qh=qh, h=h):
            m, l, acc = carry
            r0 = pl.multiple_of(j * tq, tq)
            ks = kb_scr[pl.ds(r0, tq), lo:lo + FOX_HEAD_DIM]
            vs = vb_scr[pl.ds(r0, tq), lo:lo + FOX_HEAD_DIM]
            cj = ct_ref[0, h, pl.ds(j, 1), :]
            s = lax.dot_general(qh, ks, (((1,), (1,)), ((), ())), preferred_element_type=F32) - cj
            if masked:
                s = jnp.where(causal, s, NEG_BIG)
            m_new = jnp.maximum(m, jnp.max(s, axis=-1, keepdims=True))
            p = jnp.exp(s - m_new)
            alpha = jnp.exp(m - m_new)
            l = alpha * l + jnp.sum(p, axis=-1, keepdims=True)
            acc = alpha * acc + jnp.dot(p.astype(BF16), vs, preferred_element_type=F32)
            return m_new, l, acc

        init = (jnp.full((tq, 1), NEG_BIG, F32), jnp.zeros((tq, 1), F32),
                jnp.zeros((tq, FOX_HEAD_DIM), F32))
        carry = lax.fori_loop(0, qi, functools.partial(step, masked=False), init)
        m, l, acc = step(qi, carry, True)
        outs.append(_rms(acc / l, onorm))
    o_ref[0] = jnp.concatenate(outs, axis=-1)


def _fox_prompt(fq3, fk3, fv3, ct4, onorm, tq):
    b, seq, _ = fq3.shape
    nq = seq // tq
    return pl.pallas_call(
        functools.partial(_fox_prompt_kernel, tq=tq),
        grid=(b, nq),
        in_specs=[pl.BlockSpec((1, tq, FOX_WIDTH), lambda i, j: (i, j, 0)),
                  pl.BlockSpec((1, seq, FOX_WIDTH), lambda i, j: (i, 0, 0)),
                  pl.BlockSpec((1, seq, FOX_WIDTH), lambda i, j: (i, 0, 0)),
                  pl.BlockSpec((1, FOX_HEADS, nq, tq), lambda i, j: (i, 0, 0, 0)),
                  pl.BlockSpec(onorm.shape, lambda i, j: (0, 0))],
        out_specs=pl.BlockSpec((1, tq, FOX_WIDTH), lambda i, j: (i, j, 0)),
        out_shape=jax.ShapeDtypeStruct((b, seq, FOX_WIDTH), F32),
        scratch_shapes=[pltpu.VMEM((seq, FOX_WIDTH), BF16), pltpu.VMEM((seq, FOX_WIDTH), BF16)],
        compiler_params=_cparams(("parallel", "arbitrary")),
    )(fq3, fk3, fv3, ct4, onorm)


def _fox_sample_kernel(pt_ref, qbd_ref, kn_ref, vn_ref, cn_ref, on_ref, *rest, pps, lq):
    k_refs = rest[0:pps]
    v_refs = rest[pps:2 * pps]
    lf_refs = rest[2 * pps:3 * pps]
    o_ref = rest[3 * pps]
    m_scr, l_scr, acc_scr, suf_scr = rest[3 * pps + 1:]
    step = pl.program_id(1)
    nstep = pl.num_programs(1)
    rows = FOX_HEADS * lq
    qbd = qbd_ref[0]
    expand = (_iota2((rows, FOX_HEADS), 0) // lq == _iota2((rows, FOX_HEADS), 1)).astype(F32)

    @pl.when(step == 0)
    def _():
        s = lax.dot_general(qbd, kn_ref[0].astype(BF16), (((1,), (1,)), ((), ())),
                            preferred_element_type=F32)
        s = s - _small_mm(expand, cn_ref[0])
        qpos = _iota2((rows, lq), 0) % lq
        s = jnp.where(_iota2((rows, lq), 1) <= qpos, s, NEG_BIG)
        m = jnp.max(s, axis=-1, keepdims=True)
        p = jnp.exp(s - m)
        m_scr[...] = m
        l_scr[...] = jnp.sum(p, axis=-1, keepdims=True)
        acc_scr[...] = jnp.dot(p.astype(BF16), vn_ref[0].astype(BF16), preferred_element_type=F32)
        suf_scr[...] = jnp.zeros_like(suf_scr)

    page = k_refs[0].shape[2]
    upper = (_iota2((page, page), 0) > _iota2((page, page), 1)).astype(F32)
    m = m_scr[...]
    l = l_scr[...]
    acc = acc_scr[...]
    suf = suf_scr[...]
    for i in range(pps):
        lf_rows = _dot_nt_hi(expand, lf_refs[i][0, 0])
        r_past = _dot_hi(lf_rows, upper) + suf
        suf = suf + jnp.sum(lf_rows, axis=-1, keepdims=True)
        kp = k_refs[i][0, 0].astype(BF16)
        s = lax.dot_general(qbd, kp, (((1,), (1,)), ((), ())), preferred_element_type=F32) + r_past
        m_new = jnp.maximum(m, jnp.max(s, axis=-1, keepdims=True))
        p = jnp.exp(s - m_new)
        alpha = jnp.exp(m - m_new)
        l = alpha * l + jnp.sum(p, axis=-1, keepdims=True)
        acc = alpha * acc + jnp.dot(p.astype(BF16), v_refs[i][0, 0].astype(BF16),
                                    preferred_element_type=F32)
        m = m_new
    m_scr[...] = m
    l_scr[...] = l
    acc_scr[...] = acc
    suf_scr[...] = suf

    @pl.when(step == nstep - 1)
    def _():
        o = acc / l
        outs = []
        for h in range(FOX_HEADS):
            oh = o[h * lq:(h + 1) * lq, h * FOX_HEAD_DIM:(h + 1) * FOX_HEAD_DIM]
            outs.append(_rms(oh, on_ref[...]))
        o_ref[0] = jnp.concatenate(outs, axis=-1)


def _fox_sample(page_table, layer, qbd, kn3, vn3, cn3, onorm, ck4, cv4, clf4, pps):
    b, lq, _ = kn3.shape
    n_pages = page_table.shape[1]
    page = ck4.shape[2]
    nstep = n_pages // pps
    rows = FOX_HEADS * lq

    def page_map(i):
        return lambda bi, s, pt: (layer, pt[bi, n_pages - 1 - (s * pps + i)], 0, 0)

    in_specs = [pl.BlockSpec((1, rows, FOX_WIDTH), lambda bi, s, pt: (bi, 0, 0)),
                pl.BlockSpec((1, lq, FOX_WIDTH), lambda bi, s, pt: (bi, 0, 0)),
                pl.BlockSpec((1, lq, FOX_WIDTH), lambda bi, s, pt: (bi, 0, 0)),
                pl.BlockSpec((1, FOX_HEADS, lq), lambda bi, s, pt: (bi, 0, 0)),
                pl.BlockSpec(onorm.shape, lambda bi, s, pt: (0, 0))]
    in_specs += [pl.BlockSpec((1, 1, page, FOX_WIDTH), page_map(i)) for i in range(pps)]
    in_specs += [pl.BlockSpec((1, 1, page, FOX_WIDTH), page_map(i)) for i in range(pps)]
    in_specs += [pl.BlockSpec((1, 1, page, FOX_HEADS), page_map(i)) for i in range(pps)]
    grid_spec = pltpu.PrefetchScalarGridSpec(
        num_scalar_prefetch=1,
        grid=(b, nstep),
        in_specs=in_specs,
        out_specs=pl.BlockSpec((1, lq, FOX_WIDTH), lambda bi, s, pt: (bi, 0, 0)),
        scratch_shapes=[pltpu.VMEM((rows, 1), F32), pltpu.VMEM((rows, 1), F32),
                        pltpu.VMEM((rows, FOX_WIDTH), F32), pltpu.VMEM((rows, 1), F32)],
    )
    return pl.pallas_call(
        functools.partial(_fox_sample_kernel, pps=pps, lq=lq),
        grid_spec=grid_spec,
        out_shape=jax.ShapeDtypeStruct((b, lq, FOX_WIDTH), F32),
        compiler_params=_cparams(("parallel", "arbitrary")),
    )(page_table, qbd, kn3, vn3, cn3, onorm, *([ck4] * pps), *([cv4] * pps), *([clf4] * pps))


def _out_proj_kernel(x_ref, a_ref, b_ref, w_ref, o_ref):
    o_ref[...] = (x_ref[...]
                  + jnp.dot(a_ref[...].astype(BF16), w_ref[0:GDN_WIDTH, :], preferred_element_type=F32)
                  + jnp.dot(b_ref[...].astype(BF16), w_ref[GDN_WIDTH:, :], preferred_element_type=F32))


def _out_proj(x2, a2, b2, w, tm):
    t = x2.shape[0]
    return pl.pallas_call(
        _out_proj_kernel,
        grid=(t // tm,),
        in_specs=[pl.BlockSpec((tm, D_MODEL), lambda i: (i, 0)),
                  pl.BlockSpec((tm, GDN_WIDTH), lambda i: (i, 0)),
                  pl.BlockSpec((tm, FOX_WIDTH), lambda i: (i, 0)),
                  pl.BlockSpec(w.shape, lambda i: (0, 0))],
        out_specs=pl.BlockSpec((tm, D_MODEL), lambda i: (i, 0)),
        out_shape=jax.ShapeDtypeStruct((t, D_MODEL), F32),
        compiler_params=_cparams(("parallel",)),
    )(x2, a2, b2, w)


def _xattn_kernel(x_ref, g_ref, wq_ref, wo_ref, mk_ref, mv_ref, o_ref):
    x = x_ref[0]
    hb = _rms(x, g_ref[...]).astype(BF16)
    q = jnp.dot(hb, wq_ref[...], preferred_element_type=F32) * (XA_HEAD_DIM ** -0.5)
    mk = mk_ref[0].astype(BF16)
    mv = mv_ref[0].astype(BF16)
    outs = []
    for h in range(XA_HEADS):
        lo = h * XA_HEAD_DIM
        s = lax.dot_general(q[:, lo:lo + XA_HEAD_DIM].astype(BF16), mk[:, lo:lo + XA_HEAD_DIM],
                            (((1,), (1,)), ((), ())), preferred_element_type=F32)
        p = jnp.exp(s - jnp.max(s, axis=-1, keepdims=True))
        p = p / jnp.sum(p, axis=-1, keepdims=True)
        outs.append(jnp.dot(p.astype(BF16), mv[:, lo:lo + XA_HEAD_DIM], preferred_element_type=F32))
    o = jnp.concatenate(outs, axis=-1).astype(BF16)
    o_ref[0] = x + jnp.dot(o, wo_ref[...], preferred_element_type=F32)


def _xattn(x3, g, wq, wo, mk3, mv3, tq):
    b, seq, _ = x3.shape
    n_mem = mk3.shape[1]
    return pl.pallas_call(
        _xattn_kernel,
        grid=(b, seq // tq),
        in_specs=[pl.BlockSpec((1, tq, D_MODEL), lambda i, j: (i, j, 0)),
                  pl.BlockSpec(g.shape, lambda i, j: (0, 0)),
                  pl.BlockSpec(wq.shape, lambda i, j: (0, 0)),
                  pl.BlockSpec(wo.shape, lambda i, j: (0, 0)),
                  pl.BlockSpec((1, n_mem, XA_WIDTH), lambda i, j: (i, 0, 0)),
                  pl.BlockSpec((1, n_mem, XA_WIDTH), lambda i, j: (i, 0, 0))],
        out_specs=pl.BlockSpec((1, tq, D_MODEL), lambda i, j: (i, j, 0)),
        out_shape=jax.ShapeDtypeStruct((b, seq, D_MODEL), F32),
        compiler_params=_cparams(("parallel", "parallel")),
    )(x3, g, wq, wo, mk3, mv3)


def _router_kernel(x_ref, g_ref, wr_ref, br_ref, h_ref, route_ref, gate_ref, cnt_ref, base_scr, *, tm):
    i = pl.program_id(0)

    @pl.when(i == 0)
    def _():
        base_scr[...] = jnp.zeros_like(base_scr)

    h = _rms(x_ref[...], g_ref[...])
    h_ref[...] = h
    logits = _dot_hi(h, wr_ref[...]) + br_ref[...]
    lane = _iota2((tm, LANES), 1)
    lane_f = lane.astype(F32)
    vals, hots, idxs = [], [], []
    cur = logits
    for _ in range(TOP_K):
        mx = jnp.max(cur, axis=-1, keepdims=True)
        idx_f = jnp.min(jnp.where(cur == mx, lane_f, float(LANES)), axis=-1, keepdims=True)
        hot = lane_f == idx_f
        vals.append(mx)
        idxs.append(idx_f.astype(I32))
        hots.append(hot)
        cur = jnp.where(hot, -jnp.inf, cur)
    exps = [jnp.exp(v - vals[0]) for v in vals]
    denom = exps[0] + exps[1] + exps[2] + exps[3]
    member = (hots[0] | hots[1] | hots[2] | hots[3]).astype(F32)
    strict = (_iota2((tm, tm), 1) < _iota2((tm, tm), 0)).astype(BF16)
    before = jnp.dot(strict, member.astype(BF16), preferred_element_type=F32) + base_scr[...]
    route = jnp.zeros((tm, LANES), I32)
    gate = jnp.zeros((tm, LANES), F32)
    for k in range(TOP_K):
        rank = jnp.sum(jnp.where(hots[k], before, 0.0), axis=-1, keepdims=True).astype(I32)
        route = jnp.where(lane == k, idxs[k], route)
        route = jnp.where(lane == TOP_K + k, rank, route)
        gate = jnp.where(lane == k, exps[k] / denom, gate)
    route_ref[...] = route
    gate_ref[...] = gate
    base_scr[...] = base_scr[...] + jnp.sum(member, axis=0, keepdims=True)
    cnt_ref[...] = base_scr[...]


def _router(x2, g, wr, br, tm):
    t = x2.shape[0]
    return pl.pallas_call(
        functools.partial(_router_kernel, tm=tm),
        grid=(t // tm,),
        in_specs=[pl.BlockSpec((tm, D_MODEL), lambda i: (i, 0)),
                  pl.BlockSpec(g.shape, lambda i: (0, 0)),
                  pl.BlockSpec(wr.shape, lambda i: (0, 0)),
                  pl.BlockSpec(br.shape, lambda i: (0, 0))],
        out_specs=[pl.BlockSpec((tm, D_MODEL), lambda i: (i, 0)),
                   pl.BlockSpec((tm, LANES), lambda i: (i, 0)),
                   pl.BlockSpec((tm, LANES), lambda i: (i, 0)),
                   pl.BlockSpec((1, LANES), lambda i: (0, 0))],
        out_shape=[jax.ShapeDtypeStruct((t, D_MODEL), F32), jax.ShapeDtypeStruct((t, LANES), I32),
                   jax.ShapeDtypeStruct((t, LANES), F32), jax.ShapeDtypeStruct((1, LANES), F32)],
        scratch_shapes=[pltpu.VMEM((1, LANES), F32)],
        compiler_params=_cparams(("arbitrary",)),
    )(x2, g, wr, br)


def _dispatch_kernel(dest_ref, h_hbm, init_hbm, xs_hbm, sem, *, tm):
    del init_hbm
    i = pl.program_id(0)
    npair = tm * TOP_K

    def row_copy(p):
        tok = i * tm + p // TOP_K
        return pltpu.make_async_copy(h_hbm.at[pl.ds(tok, 1)], xs_hbm.at[pl.ds(dest_ref[0, 0, p], 1)], sem)

    def issue(p, c):
        row_copy(p).start()
        return c

    lax.fori_loop(0, npair, issue, 0)

    def drain(p, c):
        row_copy(p).wait()
        return c

    lax.fori_loop(0, npair, drain, 0)


def _dispatch(dest3, h2, rows, tm):
    t = h2.shape[0]
    init = jnp.zeros((rows, D_MODEL), F32)
    return pl.pallas_call(
        functools.partial(_dispatch_kernel, tm=tm),
        grid=(t // tm,),
        in_specs=[pl.BlockSpec((1, 1, tm * TOP_K), lambda i: (i, 0, 0), memory_space=pltpu.SMEM),
                  pl.BlockSpec(memory_space=pl.ANY),
                  pl.BlockSpec(memory_space=pl.ANY)],
        out_specs=pl.BlockSpec(memory_space=pl.ANY),
        out_shape=jax.ShapeDtypeStruct((rows, D_MODEL), F32),
        scratch_shapes=[pltpu.SemaphoreType.DMA(())],
        input_output_aliases={2: 0},
        compiler_params=pltpu.CompilerParams(dimension_semantics=("arbitrary",),
                                             vmem_limit_bytes=VMEM_LIMIT, has_side_effects=True),
    )(dest3, h2, init)


def _ffn_kernel(be_ref, nv_ref, x_ref, w1_ref, b1_ref, w2_ref, b2_ref, o_ref):
    i = pl.program_id(0)
    live = i * EXPERT_BLOCK < nv_ref[0]

    @pl.when(live)
    def _():
        hb = jnp.dot(x_ref[...].astype(BF16), w1_ref[0], preferred_element_type=F32) + b1_ref[0]
        glu = jnp.minimum(hb[:, :D_FF], SWIGLU_LIMIT)
        lin = jnp.clip(hb[:, D_FF:], -SWIGLU_LIMIT, SWIGLU_LIMIT)
        act = glu * _sigmoid(SWIGLU_ALPHA * glu) * (lin + 1.0)
        o_ref[...] = jnp.dot(act.astype(BF16), w2_ref[0], preferred_element_type=F32) + b2_ref[0]

    @pl.when(jnp.logical_not(live))
    def _():
        o_ref[...] = jnp.zeros_like(o_ref)


def _ffn(block_e, nvalid, xs, w1, b1, w2, b2):
    rows = xs.shape[0]
    nb = rows // EXPERT_BLOCK
    grid_spec = pltpu.PrefetchScalarGridSpec(
        num_scalar_prefetch=2,
        grid=(nb,),
        in_specs=[pl.BlockSpec((EXPERT_BLOCK, D_MODEL), lambda i, be, nv: (i, 0)),
                  pl.BlockSpec((1, D_MODEL, 2 * D_FF), lambda i, be, nv: (be[i], 0, 0)),
                  pl.BlockSpec((1, 1, 2 * D_FF), lambda i, be, nv: (be[i], 0, 0)),
                  pl.BlockSpec((1, D_FF, D_MODEL), lambda i, be, nv: (be[i], 0, 0)),
                  pl.BlockSpec((1, 1, D_MODEL), lambda i, be, nv: (be[i], 0, 0))],
        out_specs=pl.BlockSpec((EXPERT_BLOCK, D_MODEL), lambda i, be, nv: (i, 0)),
    )
    return pl.pallas_call(
        _ffn_kernel,
        grid_spec=grid_spec,
        out_shape=jax.ShapeDtypeStruct((rows, D_MODEL), F32),
        compiler_params=_cparams(("arbitrary",)),
    )(block_e, nvalid, xs, w1, b1, w2, b2)


def _combine_kernel(dest_ref, x_ref, gate_ref, fg_ref, os_hbm, y_ref, yn_ref, buf, sem, *, tm):
    npair = tm * TOP_K

    def row_copy(p):
        return pltpu.make_async_copy(os_hbm.at[pl.ds(dest_ref[0, 0, p], 1)],
                                     buf.at[p % TOP_K, pl.ds(p // TOP_K, 1)], sem)

    def issue(p, c):
        row_copy(p).start()
        return c

    lax.fori_loop(0, npair, issue, 0)

    def drain(p, c):
        row_copy(p).wait()
        return c

    lax.fori_loop(0, npair, drain, 0)
    gate = gate_ref[...]
    y = x_ref[...]
    for k in range(TOP_K):
        y = y + gate[:, k:k + 1] * buf[k]
    y_ref[...] = y
    yn_ref[...] = _rms(y, fg_ref[...])


def _combine(dest3, x2, gate, fg, os2, tm):
    t = x2.shape[0]
    return pl.pallas_call(
        functools.partial(_combine_kernel, tm=tm),
        grid=(t // tm,),
        in_specs=[pl.BlockSpec((1, 1, tm * TOP_K), lambda i: (i, 0, 0), memory_space=pltpu.SMEM),
                  pl.BlockSpec((tm, D_MODEL), lambda i: (i, 0)),
                  pl.BlockSpec((tm, LANES), lambda i: (i, 0)),
                  pl.BlockSpec(fg.shape, lambda i: (0, 0)),
                  pl.BlockSpec(memory_space=pl.ANY)],
        out_specs=[pl.BlockSpec((tm, D_MODEL), lambda i: (i, 0)),
                   pl.BlockSpec((tm, D_MODEL), lambda i: (i, 0))],
        out_shape=[jax.ShapeDtypeStruct((t, D_MODEL), F32), jax.ShapeDtypeStruct((t, D_MODEL), F32)],
        scratch_shapes=[pltpu.VMEM((TOP_K, tm, D_MODEL), F32), pltpu.SemaphoreType.DMA(())],
        compiler_params=_cparams(("arbitrary",)),
    )(dest3, x2, gate, fg, os2)


def _row_tile(t, want):
    tm = min(want, t)
    assert t % tm == 0
    return tm


def _mixer(x3, lw, conv_state8, s0, fox_fn):
    b, seq, _ = x3.shape
    t = b * seq
    x2 = x3.reshape(t, D_MODEL)
    qkv, z, fq, fk, fv, sm, smt = _in_proj(x2, lw["norm1_g"], lw["w_main"], lw["w_small"], lw["w_small_t"],
                                           _row_tile(t, 256))
    sm3 = sm.reshape(b, seq, LANES)
    smt3 = jnp.transpose(smt.reshape(16, b, seq), (1, 0, 2))
    logf, ct = _logf(sm3, smt3, lw["fb_row"], lw["fb_col"])
    c = min(GDN_CHUNK, seq)
    tl = min(256, seq)
    gdn_out, s_new, cbuf = _gdn(qkv.reshape(b, seq, CONV_CH), z.reshape(b, seq, GDN_WIDTH), sm3, smt3,
                                conv_state8, s0, lw["conv_w"], lw["gp_row"], lw["gp_col"], tl, c,
                                exact_small=(c <= SUBLANES))
    fq3 = fq.reshape(b, seq, FOX_WIDTH)
    fk3 = fk.reshape(b, seq, FOX_WIDTH)
    fv3 = fv.reshape(b, seq, FOX_WIDTH)
    fox_out = fox_fn(fq3, fk3, fv3, ct)
    y2 = _out_proj(x2, gdn_out.reshape(t, GDN_WIDTH), fox_out.reshape(t, FOX_WIDTH), lw["w_out"],
                   _row_tile(t, 512))
    return (y2.reshape(b, seq, D_MODEL), s_new, cbuf,
            fk3.reshape(b, seq, FOX_HEADS, FOX_HEAD_DIM), fv3.reshape(b, seq, FOX_HEADS, FOX_HEAD_DIM), logf)


def _moe(x3, lw, final_g):
    b, seq, _ = x3.shape
    t = b * seq
    x2 = x3.reshape(t, D_MODEL)
    h2, route, gate, counts = _router(x2, lw["norm3_g"], lw["w_router"], lw["b_router"], _row_tile(t, 256))
    cnt = counts[0, :N_EXPERTS].astype(I32)
    padded = (cnt + EXPERT_BLOCK - 1) // EXPERT_BLOCK * EXPERT_BLOCK
    pends = jnp.cumsum(padded)
    pstart = pends - padded
    nb = t * TOP_K // EXPERT_BLOCK + N_EXPERTS
    rows = nb * EXPERT_BLOCK
    block_e = jnp.minimum(jnp.searchsorted(pends, jnp.arange(nb, dtype=I32) * EXPERT_BLOCK, side="right"),
                          N_EXPERTS - 1).astype(I32)
    nvalid = pends[-1:].astype(I32)
    dest = pstart[route[:, :TOP_K]] + route[:, TOP_K:2 * TOP_K]
    tmd = _row_tile(t, 256)
    xs = _dispatch(dest.reshape(t // tmd, 1, tmd * TOP_K), h2, rows, tmd)
    os2 = _ffn(block_e, nvalid, xs, lw["w1"], lw["b1"], lw["w2"], lw["b2"])
    tmc = _row_tile(t, 128)
    y2, yn2 = _combine(dest.reshape(t // tmc, 1, tmc * TOP_K), x2, gate, final_g, os2, tmc)
    return y2.reshape(b, seq, D_MODEL), yn2.reshape(b, seq, D_MODEL)


def _prep_layer(l, norm1_g, w_in, conv_w, gdn_a_log, gdn_dt_bias, gdn_onorm, fox_fbias, fox_onorm, w_out,
                norm2_g, mem_norm_g, w_xq, w_mkv, w_xo, norm3_g, w_router, b_router, w1, b1, w2, b2):
    wi = w_in[l]
    gq, gk, gv, gz, gb, ga, fq, fk, fv, ff = _split_in(wi)
    w_main = jnp.concatenate([gq, gk, gv, gz, fq, fk, fv], axis=1).astype(BF16)
    w_small = jnp.concatenate([gb, ga, ff, jnp.zeros((D_MODEL, LANES - 16), F32)], axis=1).astype(BF16)
    w_small_t = jnp.transpose(w_small[:, :16])
    lanes = lambda v, off: jnp.zeros((LANES,), F32).at[off:off + v.shape[0]].set(v)
    gp_row = jnp.zeros((SUBLANES, LANES), F32)
    gp_row = gp_row.at[0].set(lanes(gdn_a_log[l], GDN_HEADS)).at[1].set(lanes(gdn_dt_bias[l], GDN_HEADS))
    gp_row = gp_row.at[2].set(gdn_onorm[l])
    gp_col = jnp.zeros((16, LANES), F32)
    gp_col = gp_col.at[GDN_HEADS:2 * GDN_HEADS, 0].set(gdn_a_log[l]).at[GDN_HEADS:2 * GDN_HEADS, 1].set(gdn_dt_bias[l])
    fb_row = jnp.zeros((1, LANES), F32).at[0, 8:16].set(fox_fbias[l])
    fb_col = jnp.zeros((16, LANES), F32).at[8:16, 0].set(fox_fbias[l])
    wr = jnp.concatenate([w_router[l], jnp.zeros((D_MODEL, LANES - N_EXPERTS), F32)], axis=1)
    br = jnp.full((1, LANES), NEG_BIG, F32).at[0, :N_EXPERTS].set(b_router[l])
    return {
        "norm1_g": norm1_g[l].reshape(1, D_MODEL), "w_main": w_main, "w_small": w_small, "w_small_t": w_small_t,
        "conv_w": conv_w[l], "gp_row": gp_row, "gp_col": gp_col, "fb_row": fb_row, "fb_col": fb_col,
        "fox_onorm": fox_onorm[l].reshape(1, FOX_HEAD_DIM), "w_out": w_out[l].astype(BF16),
        "norm2_g": norm2_g[l].reshape(1, D_MODEL), "mem_norm_g": mem_norm_g[l].reshape(1, D_MODEL),
        "w_xq": w_xq[l].astype(BF16), "w_mkv": w_mkv[l].astype(BF16), "w_xo": w_xo[l].astype(BF16),
        "norm3_g": norm3_g[l].reshape(1, D_MODEL), "w_router": wr, "b_router": br,
        "w1": w1[l].astype(BF16), "b1": b1[l].reshape(N_EXPERTS, 1, 2 * D_FF),
        "w2": w2[l].astype(BF16), "b2": b2[l].reshape(N_EXPERTS, 1, D_MODEL),
    }


def _split_in(wi):
    widths = (GDN_WIDTH, GDN_WIDTH, GDN_WIDTH, GDN_WIDTH, GDN_HEADS, GDN_HEADS,
              FOX_WIDTH, FOX_WIDTH, FOX_WIDTH, FOX_HEADS)
    outs, c0 = [], 0
    for w in widths:
        outs.append(wi[:, c0:c0 + w])
        c0 += w
    return outs


def kernel(x_prompt, x_sample, mem_prompt, cache_fox_k, cache_fox_v, cache_fox_logf, cache_mem_k, cache_mem_v, state_gdn, state_conv, page_table, norm1_g, w_in, conv_w, gdn_a_log, gdn_dt_bias, gdn_onorm, fox_fbias, fox_onorm, w_out, norm2_g, mem_norm_g, w_xq, w_mkv, w_xo, norm3_g, w_router, b_router, w1, b1, w2, b2, final_norm_g):
    depth = w_in.shape[0]
    bp, lp, _ = x_prompt.shape
    bs, ls, _ = x_sample.shape
    n_mem = mem_prompt.shape[1]
    n_pool, page = cache_fox_k.shape[1], cache_fox_k.shape[2]
    ck4 = cache_fox_k.reshape(depth, n_pool, page, FOX_WIDTH)
    cv4 = cache_fox_v.reshape(depth, n_pool, page, FOX_WIDTH)
    final_g = final_norm_g.reshape(1, D_MODEL)
    xp, xs = x_prompt, x_sample
    yp = ys = None
    outs = {k: [] for k in ("fkp", "fvp", "flp", "mkp", "mvp", "sgp", "scp", "fks", "fvs", "fls", "sgs", "scs")}
    head_mask = (jnp.arange(FOX_HEADS)[:, None] == jnp.arange(FOX_HEADS)[None, :]).astype(F32)
    for l in range(depth):
        lw = _prep_layer(l, norm1_g, w_in, conv_w, gdn_a_log, gdn_dt_bias, gdn_onorm, fox_fbias, fox_onorm,
                         w_out, norm2_g, mem_norm_g, w_xq, w_mkv, w_xo, norm3_g, w_router, b_router,
                         w1, b1, w2, b2)
        tq = min(256, lp)

        def fox_p(fq3, fk3, fv3, ct, lw=lw, tq=tq):
            return _fox_prompt(fq3, fk3, fv3, ct.reshape(bp, FOX_HEADS, lp // tq, tq), lw["fox_onorm"], tq)

        xp, s_new, cbuf, k_new, v_new, lf_new = _mixer(
            xp, lw, jnp.zeros((bp, SUBLANES, CONV_CH), F32),
            jnp.zeros((bp, GDN_HEADS, GDN_DK, GDN_DV), F32), fox_p)
        outs["fkp"].append(k_new); outs["fvp"].append(v_new); outs["flp"].append(lf_new)
        outs["sgp"].append(s_new); outs["scp"].append(cbuf)

        def fox_s(fq3, fk3, fv3, ct, lw=lw, l=l):
            q4 = fq3.reshape(bs, ls, FOX_HEADS, FOX_HEAD_DIM) * (FOX_HEAD_DIM ** -0.5)
            qbd = jnp.einsum("bqhd,hg->bhqgd", q4, head_mask).reshape(bs, FOX_HEADS * ls, FOX_WIDTH)
            return _fox_sample(page_table, l, qbd.astype(BF16), fk3, fv3, ct, lw["fox_onorm"],
                               ck4, cv4, cache_fox_logf, pps=8)

        cst8 = jnp.pad(state_conv[l], ((0, 0), (SUBLANES - (CONV_W - 1), 0), (0, 0)))
        xs, s_new, cbuf, k_new, v_new, lf_new = _mixer(xs, lw, cst8, state_gdn[l], fox_s)
        outs["fks"].append(k_new); outs["fvs"].append(v_new); outs["fls"].append(lf_new)
        outs["sgs"].append(s_new); outs["scs"].append(cbuf)

        mk2, mv2 = _norm_proj(mem_prompt.reshape(bp * n_mem, D_MODEL), lw["mem_norm_g"], lw["w_mkv"],
                              (XA_WIDTH, XA_WIDTH), _row_tile(bp * n_mem, 512))
        outs["mkp"].append(mk2.reshape(bp, n_mem, XA_HEADS, XA_HEAD_DIM))
        outs["mvp"].append(mv2.reshape(bp, n_mem, XA_HEADS, XA_HEAD_DIM))
        xp = _xattn(xp, lw["norm2_g"], lw["w_xq"], lw["w_xo"], mk2.reshape(bp, n_mem, XA_WIDTH),
                    mv2.reshape(bp, n_mem, XA_WIDTH), min(512, lp))
        xs = _xattn(xs, lw["norm2_g"], lw["w_xq"], lw["w_xo"], cache_mem_k[l].reshape(bs, n_mem, XA_WIDTH),
                    cache_mem_v[l].reshape(bs, n_mem, XA_WIDTH), ls)

        xp, yp = _moe(xp, lw, final_g)
        xs, ys = _moe(xs, lw, final_g)
    st = jnp.stack
    return (yp, ys, st(outs["fkp"]), st(outs["fvp"]), st(outs["flp"]), st(outs["mkp"]), st(outs["mvp"]),
            st(outs["sgp"]), st(outs["scp"]), st(outs["fks"]), st(outs["fvs"]), st(outs["fls"]),
            st(outs["sgs"]), st(outs["scs"]))
```

```python
import functools
import math

import jax
import jax.numpy as jnp
from jax import lax
from jax.experimental import pallas as pl
from jax.experimental.pallas import tpu as pltpu

F32 = jnp.float32
BF16 = jnp.bfloat16
I32 = jnp.int32
HIGHEST = lax.Precision.HIGHEST

D_MODEL = 1024
GDN_HEADS = 4
GDN_DK = 128
GDN_DV = 128
GDN_WIDTH = GDN_HEADS * GDN_DV
CONV_W = 4
CONV_CH = 3 * GDN_WIDTH
GDN_CHUNK = 64
FOX_HEADS = 8
FOX_HEAD_DIM = 64
FOX_WIDTH = FOX_HEADS * FOX_HEAD_DIM
XA_HEADS = 4
XA_HEAD_DIM = 128
XA_WIDTH = XA_HEADS * XA_HEAD_DIM
N_EXPERTS = 32
TOP_K = 4
D_FF = D_MODEL
SWIGLU_LIMIT = 7.0
SWIGLU_ALPHA = 1.702
NORM_EPS = 1e-6
NEG_BIG = -1e30

LANES = 128
SUBLANES = 8
VMEM_LIMIT = 52 * 1024 * 1024
MAIN_COLS = 2 * GDN_WIDTH + 2 * GDN_WIDTH + 3 * FOX_WIDTH
EXPERT_BLOCK = 256


def _cparams(sem):
    return pltpu.CompilerParams(dimension_semantics=sem, vmem_limit_bytes=VMEM_LIMIT)


def _dot(a, b):
    return jnp.dot(a.astype(BF16), b.astype(BF16), preferred_element_type=F32)


def _dot_nt(a, b):
    return lax.dot_general(a.astype(BF16), b.astype(BF16), (((1,), (1,)), ((), ())),
                           preferred_element_type=F32)


def _dot_tn(a, b):
    return lax.dot_general(a.astype(BF16), b.astype(BF16), (((0,), (0,)), ((), ())),
                           preferred_element_type=F32)


def _dot_hi(a, b):
    return jnp.dot(a, b, precision=HIGHEST, preferred_element_type=F32)


def _dot_nt_hi(a, b):
    return lax.dot_general(a, b, (((1,), (1,)), ((), ())), precision=HIGHEST,
                           preferred_element_type=F32)


def _rms(x, g):
    return x * lax.rsqrt(jnp.mean(x * x, axis=-1, keepdims=True) + NORM_EPS) * g


def _sigmoid(x):
    return 1.0 / (1.0 + jnp.exp(-x))


def _softplus(x):
    return jnp.maximum(x, 0.0) + jnp.log1p(jnp.exp(-jnp.abs(x)))


def _log_sigmoid(x):
    return jnp.minimum(x, 0.0) - jnp.log1p(jnp.exp(-jnp.abs(x)))


def _iota2(shape, dim):
    return lax.broadcasted_iota(I32, shape, dim)


ROW_TILE = D_MODEL // LANES


def _store_row_tiles(ref, val, n):
    for s in range(ROW_TILE):
        ref[pl.ds(s, n, stride=ROW_TILE), :] = val[:, s * LANES:(s + 1) * LANES]


def _load_row_tiles(ref, n):
    return jnp.concatenate([ref[pl.ds(s, n, stride=ROW_TILE), :] for s in range(ROW_TILE)], axis=-1)


def _in_proj_kernel(x_ref, g_ref, wm_ref, ws_ref, wst_ref,
                    qkv_ref, z_ref, fq_ref, fk_ref, fv_ref, sm_ref, smt_ref):
    hb = _rms(x_ref[...], g_ref[...]).astype(BF16)
    c0 = 0
    for ref, width in ((qkv_ref, CONV_CH), (z_ref, GDN_WIDTH), (fq_ref, FOX_WIDTH),
                       (fk_ref, FOX_WIDTH), (fv_ref, FOX_WIDTH)):
        ref[...] = jnp.dot(hb, wm_ref[:, c0:c0 + width], preferred_element_type=F32)
        c0 += width
    sm_ref[...] = jnp.dot(hb, ws_ref[...], preferred_element_type=F32)
    smt_ref[...] = lax.dot_general(wst_ref[...], hb, (((1,), (1,)), ((), ())),
                                   preferred_element_type=F32)


def _in_proj(x2, g, wm, ws, wst, tm):
    t = x2.shape[0]
    row = lambda w: pl.BlockSpec((tm, w), lambda i: (i, 0))
    full = lambda a: pl.BlockSpec(a.shape, lambda i: (0,) * a.ndim)
    return pl.pallas_call(
        _in_proj_kernel,
        grid=(t // tm,),
        in_specs=[row(D_MODEL), full(g), full(wm), full(ws), full(wst)],
        out_specs=[row(CONV_CH), row(GDN_WIDTH), row(FOX_WIDTH), row(FOX_WIDTH), row(FOX_WIDTH),
                   row(LANES), pl.BlockSpec((16, tm), lambda i: (0, i))],
        out_shape=[jax.ShapeDtypeStruct((t, CONV_CH), F32), jax.ShapeDtypeStruct((t, GDN_WIDTH), F32),
                   jax.ShapeDtypeStruct((t, FOX_WIDTH), F32), jax.ShapeDtypeStruct((t, FOX_WIDTH), F32),
                   jax.ShapeDtypeStruct((t, FOX_WIDTH), F32), jax.ShapeDtypeStruct((t, LANES), F32),
                   jax.ShapeDtypeStruct((16, t), F32)],
        compiler_params=_cparams(("parallel",)),
    )(x2, g, wm, ws, wst)


def _norm_proj_kernel(x_ref, g_ref, w_ref, *out_refs):
    hb = _rms(x_ref[...], g_ref[...]).astype(BF16)
    c0 = 0
    for ref in out_refs:
        width = ref.shape[-1]
        ref[...] = jnp.dot(hb, w_ref[:, c0:c0 + width], preferred_element_type=F32)
        c0 += width


def _norm_proj(x2, g, w, widths, tm):
    t = x2.shape[0]
    return pl.pallas_call(
        _norm_proj_kernel,
        grid=(t // tm,),
        in_specs=[pl.BlockSpec((tm, D_MODEL), lambda i: (i, 0)),
                  pl.BlockSpec(g.shape, lambda i: (0, 0)),
                  pl.BlockSpec(w.shape, lambda i: (0, 0))],
        out_specs=[pl.BlockSpec((tm, wd), lambda i: (i, 0)) for wd in widths],
        out_shape=[jax.ShapeDtypeStruct((t, wd), F32) for wd in widths],
        compiler_params=_cparams(("parallel",)),
    )(x2, g, w)


def _small_mm(a, b):
    acc = a[:, 0:1] * b[0:1, :]
    for i in range(1, a.shape[1]):
        acc = acc + a[:, i:i + 1] * b[i:i + 1, :]
    return acc


def _logf_kernel(sm_ref, smt_ref, fbr_ref, fbc_ref, logf_ref, ct_ref, *, seq, chunk):
    ff = sm_ref[0][:, 8:16] + fbr_ref[:, 8:16]
    logf_ref[0] = _log_sigmoid(ff)
    lft = _log_sigmoid(smt_ref[0][8:16, :] + fbc_ref[8:16, 0:1])
    tri = (_iota2((chunk, chunk), 0) <= _iota2((chunk, chunk), 1)).astype(F32)
    carry = jnp.zeros((FOX_HEADS, 1), F32)
    for c in range(seq // chunk):
        blk = lft[:, c * chunk:(c + 1) * chunk]
        cs = (_small_mm(blk, tri) if chunk <= SUBLANES else _dot_hi(blk, tri)) + carry
        ct_ref[0, :, c * chunk:(c + 1) * chunk] = cs
        carry = cs[:, chunk - 1:chunk]


def _logf(sm3, smt3, fbr, fbc):
    b, seq, _ = sm3.shape
    chunk = min(256, seq)
    return pl.pallas_call(
        functools.partial(_logf_kernel, seq=seq, chunk=chunk),
        grid=(b,),
        in_specs=[pl.BlockSpec((1, seq, LANES), lambda i: (i, 0, 0)),
                  pl.BlockSpec((1, 16, seq), lambda i: (i, 0, 0)),
                  pl.BlockSpec(fbr.shape, lambda i: (0, 0)),
                  pl.BlockSpec(fbc.shape, lambda i: (0, 0))],
        out_specs=[pl.BlockSpec((1, seq, FOX_HEADS), lambda i: (i, 0, 0)),
                   pl.BlockSpec((1, FOX_HEADS, seq), lambda i: (i, 0, 0))],
        out_shape=[jax.ShapeDtypeStruct((b, seq, FOX_HEADS), F32),
                   jax.ShapeDtypeStruct((b, FOX_HEADS, seq), F32)],
        compiler_params=_cparams(("parallel",)),
    )(sm3, smt3, fbr, fbc)


def _unit_lower_inverse(a, c, mm):
    eye = (_iota2((c, c), 0) == _iota2((c, c), 1)).astype(F32)
    n = -a
    t = eye + n
    p = n
    levels = int(math.log2(c))
    for _ in range(levels - 1):
        p = mm(p, p)
        t = t + mm(t, p)
    return t


def _gdn_kernel(qkv_ref, z_ref, sm_ref, smt_ref, cst_ref, s0_ref, cw_ref, gpr_ref, gpc_ref,
                o_ref, sn_ref, cb_ref, s_scr, tail_scr, *, tl, c, exact_small):
    t = pl.program_id(1)
    nt = pl.num_programs(1)

    @pl.when(t == 0)
    def _():
        s_scr[...] = s0_ref[0]
        tail_scr[...] = cst_ref[0]

    if exact_small:
        mm = lambda a, b: jnp.dot(a, b, preferred_element_type=F32)
        mm_nt = lambda a, b: lax.dot_general(a, b, (((1,), (1,)), ((), ())), preferred_element_type=F32)
        mm_tn = lambda a, b: lax.dot_general(a, b, (((0,), (0,)), ((), ())), preferred_element_type=F32)
        rb = lambda v: v.astype(BF16).astype(F32)
    else:
        mm, mm_nt, mm_tn = _dot, _dot_nt, _dot_tn
        rb = lambda v: v

    x = qkv_ref[0]
    tail = tail_scr[...]
    cw = cw_ref[...]
    row8 = _iota2((SUBLANES, CONV_CH), 0)
    acc = x * cw[CONV_W - 1:CONV_W, :]
    for s in range(1, CONV_W):
        xs = pltpu.roll(x, s, 0)
        head = jnp.where(row8 < s, pltpu.roll(tail, s, 0), xs[0:SUBLANES])
        xs = head if tl == SUBLANES else jnp.concatenate([head, xs[SUBLANES:]], axis=0)
        acc = acc + xs * cw[CONV_W - 1 - s:CONV_W - s, :]
    conv = acc * _sigmoid(acc)
    tail_scr[...] = x[tl - SUBLANES:tl, :]

    @pl.when(t == nt - 1)
    def _():
        cb_ref[0] = x[tl - (CONV_W - 1):tl, :]

    sm = sm_ref[0]
    beta_c = _sigmoid(sm)
    g_c = -jnp.exp(gpr_ref[0:1, :]) * _softplus(sm + gpr_ref[1:2, :])
    smt = smt_ref[0]
    g_r = -jnp.exp(gpc_ref[:, 0:1]) * _softplus(smt + gpc_ref[:, 1:2])
    onorm = gpr_ref[2:3, :]

    ii = _iota2((c, c), 0)
    jj = _iota2((c, c), 1)
    tri_c = (jj <= ii).astype(F32)
    tri_r = (ii <= jj).astype(F32)
    small = c <= SUBLANES

    for ci in range(tl // c):
        r0 = ci * c
        gc_blk = g_c[r0:r0 + c, :]
        gr_blk = g_r[:, r0:r0 + c]
        cum_c_all = _small_mm(tri_c, gc_blk) if small else _dot_hi(tri_c, gc_blk)
        cum_r_all = _small_mm(gr_blk, tri_r) if small else _dot_hi(gr_blk, tri_r)
        for h in range(GDN_HEADS):
            lo = h * GDN_DK
            qh = conv[r0:r0 + c, lo:lo + GDN_DK]
            kh = conv[r0:r0 + c, GDN_WIDTH + lo:GDN_WIDTH + lo + GDN_DK]
            vh = conv[r0:r0 + c, 2 * GDN_WIDTH + lo:2 * GDN_WIDTH + lo + GDN_DV]
            qh = qh * lax.rsqrt(jnp.sum(qh * qh, axis=-1, keepdims=True) + NORM_EPS) * (GDN_DK ** -0.5)
            kh = kh * lax.rsqrt(jnp.sum(kh * kh, axis=-1, keepdims=True) + NORM_EPS)
            beta = beta_c[r0:r0 + c, h:h + 1]
            cum_c = cum_c_all[:, GDN_HEADS + h:GDN_HEADS + h + 1]
            cum_r = cum_r_all[GDN_HEADS + h:GDN_HEADS + h + 1, :]
            cum_last = cum_c[c - 1:c, :]
            dec = jnp.exp(jnp.where(ii >= jj, cum_c - cum_r, NEG_BIG))
            dec_strict = jnp.where(ii > jj, dec, 0.0)
            kq, kk, vv = rb(qh), rb(kh), vh
            a_mat = beta * mm_nt(kk, kk) * dec_strict
            tinv = _unit_lower_inverse(a_mat, c, lambda u, w: mm(rb(u), rb(w)))
            e_cum = jnp.exp(cum_c)
            rhs = jnp.concatenate([beta * vv, (beta * e_cum) * kh], axis=-1)
            sol = mm(rb(tinv), rb(rhs))
            w_v, w_k = sol[:, :GDN_DV], sol[:, GDN_DV:]
            p_qk = mm_nt(kq, kk) * dec
            q_g = qh * e_cum
            k_d = kh * jnp.exp(cum_last - cum_c)
            s_h = s_scr[h]
            s_m = rb(s_h)
            u = w_v - mm(rb(w_k), s_m)
            o = mm(rb(q_g), s_m) + mm(rb(p_qk), rb(u))
            s_scr[h] = jnp.exp(cum_last) * s_h + mm_tn(rb(k_d), rb(u))
            zh = z_ref[0, r0:r0 + c, lo:lo + GDN_DV]
            o = _rms(o, onorm) * (zh * _sigmoid(zh))
            o_ref[0, r0:r0 + c, lo:lo + GDN_DV] = o

    @pl.when(t == nt - 1)
    def _():
        sn_ref[0] = s_scr[...]


def _gdn(qkv3, z3, sm3, smt3, cst8, s0, cw, gpr, gpc, tl, c, exact_small):
    b, seq, _ = qkv3.shape
    nt = seq // tl
    full2 = lambda a: pl.BlockSpec(a.shape, lambda i, j: (0, 0))
    return pl.pallas_call(
        functools.partial(_gdn_kernel, tl=tl, c=c, exact_small=exact_small),
        grid=(b, nt),
        in_specs=[pl.BlockSpec((1, tl, CONV_CH), lambda i, j: (i, j, 0)),
                  pl.BlockSpec((1, tl, GDN_WIDTH), lambda i, j: (i, j, 0)),
                  pl.BlockSpec((1, tl, LANES), lambda i, j: (i, j, 0)),
                  pl.BlockSpec((1, 16, tl), lambda i, j: (i, 0, j)),
                  pl.BlockSpec((1, SUBLANES, CONV_CH), lambda i, j: (i, 0, 0)),
                  pl.BlockSpec((1, GDN_HEADS, GDN_DK, GDN_DV), lambda i, j: (i, 0, 0, 0)),
                  full2(cw), full2(gpr), full2(gpc)],
        out_specs=[pl.BlockSpec((1, tl, GDN_WIDTH), lambda i, j: (i, j, 0)),
                   pl.BlockSpec((1, GDN_HEADS, GDN_DK, GDN_DV), lambda i, j: (i, 0, 0, 0)),
                   pl.BlockSpec((1, CONV_W - 1, CONV_CH), lambda i, j: (i, 0, 0))],
        out_shape=[jax.ShapeDtypeStruct((b, seq, GDN_WIDTH), F32),
                   jax.ShapeDtypeStruct((b, GDN_HEADS, GDN_DK, GDN_DV), F32),
                   jax.ShapeDtypeStruct((b, CONV_W - 1, CONV_CH), F32)],
        scratch_shapes=[pltpu.VMEM((GDN_HEADS, GDN_DK, GDN_DV), F32),
                        pltpu.VMEM((SUBLANES, CONV_CH), F32)],
        compiler_params=_cparams(("parallel", "arbitrary")),
    )(qkv3, z3, sm3, smt3, cst8, s0, cw, gpr, gpc)


def _fox_prompt_kernel(fq_ref, fk_ref, fv_ref, ct_ref, on_ref, o_ref, kb_scr, vb_scr, *, tq):
    qi = pl.program_id(1)

    @pl.when(qi == 0)
    def _():
        kb_scr[...] = fk_ref[0].astype(BF16)
        vb_scr[...] = fv_ref[0].astype(BF16)

    q_all = fq_ref[0] * (FOX_HEAD_DIM ** -0.5)
    onorm = on_ref[...]
    causal = _iota2((tq, tq), 1) <= _iota2((tq, tq), 0)
    outs = []
    for h in range(FOX_HEADS):
        lo = h * FOX_HEAD_DIM
        qh = q_all[:, lo:lo + FOX_HEAD_DIM].astype(BF16)

        def step(j, carry, masked, lo=lo, qh=qh, h=h):
            m, l, acc = carry
            r0 = pl.multiple_of(j * tq, tq)
            ks = kb_scr[pl.ds(r0, tq), lo:lo + FOX_HEAD_DIM]
            vs = vb_scr[pl.ds(r0, tq), lo:lo + FOX_HEAD_DIM]
            cj = ct_ref[0, h, pl.ds(j, 1), :]
            s = lax.dot_general(qh, ks, (((1,), (1,)), ((), ())), preferred_element_type=F32) - cj
            if masked:
                s = jnp.where(causal, s, NEG_BIG)
            m_new = jnp.maximum(m, jnp.max(s, axis=-1, keepdims=True))
            p = jnp.exp(s - m_new)
            alpha = jnp.exp(m - m_new)
            l = alpha * l + jnp.sum(p, axis=-1, keepdims=True)
            acc = alpha * acc + jnp.dot(p.astype(BF16), vs, preferred_element_type=F32)
            return m_new, l, acc

        init = (jnp.full((tq, 1), NEG_BIG, F32), jnp.zeros((tq, 1), F32),
                jnp.zeros((tq, FOX_HEAD_DIM), F32))
        carry = lax.fori_loop(0, qi, functools.partial(step, masked=False), init)
        m, l, acc = step(qi, carry, True)
        outs.append(_rms(acc / l, onorm))
    o_ref[0] = jnp.concatenate(outs, axis=-1)


def _fox_prompt(fq3, fk3, fv3, ct4, onorm, tq):
    b, seq, _ = fq3.shape
    nq = seq // tq
    return pl.pallas_call(
        functools.partial(_fox_prompt_kernel, tq=tq),
        grid=(b, nq),
        in_specs=[pl.BlockSpec((1, tq, FOX_WIDTH), lambda i, j: (i, j, 0)),
                  pl.BlockSpec((1, seq, FOX_WIDTH), lambda i, j: (i, 0, 0)),
                  pl.BlockSpec((1, seq, FOX_WIDTH), lambda i, j: (i, 0, 0)),
                  pl.BlockSpec((1, FOX_HEADS, nq, tq), lambda i, j: (i, 0, 0, 0)),
                  pl.BlockSpec(onorm.shape, lambda i, j: (0, 0))],
        out_specs=pl.BlockSpec((1, tq, FOX_WIDTH), lambda i, j: (i, j, 0)),
        out_shape=jax.ShapeDtypeStruct((b, seq, FOX_WIDTH), F32),
        scratch_shapes=[pltpu.VMEM((seq, FOX_WIDTH), BF16), pltpu.VMEM((seq, FOX_WIDTH), BF16)],
        compiler_params=_cparams(("parallel", "arbitrary")),
    )(fq3, fk3, fv3, ct4, onorm)


def _logf_pages_kernel(lf_ref, w_ref, o_ref):
    lf = lf_ref[...]
    a = lf.astype(BF16)
    r1 = lf - a.astype(F32)
    b = r1.astype(BF16)
    c = (r1 - b.astype(F32)).astype(BF16)
    w = w_ref[...]
    o_ref[...] = (jnp.dot(a, w, preferred_element_type=F32) + jnp.dot(b, w, preferred_element_type=F32)
                  + jnp.dot(c, w, preferred_element_type=F32))


def _logf_pages(lf_flat, w2):
    n_pool, width = lf_flat.shape
    tp = next(c for c in (512, 256, 128, 64, 32, 16, 8) if n_pool % c == 0)
    return pl.pallas_call(
        _logf_pages_kernel,
        grid=(n_pool // tp,),
        in_specs=[pl.BlockSpec((tp, width), lambda i: (i, 0)),
                  pl.BlockSpec(w2.shape, lambda i: (0, 0))],
        out_specs=pl.BlockSpec((tp, 2 * width), lambda i: (i, 0)),
        out_shape=jax.ShapeDtypeStruct((n_pool, 2 * width), F32),
        compiler_params=_cparams(("parallel",)),
    )(lf_flat, w2)


def _fox_sample_kernel(pt_ref, q_ref, kn_ref, vn_ref, cn_ref, on_ref, *rest, pps, lq):
    k_refs = rest[0:pps]
    v_refs = rest[pps:2 * pps]
    r_refs = rest[2 * pps:3 * pps]
    o_ref = rest[3 * pps]
    m_scr, l_scr, acc_scr, suf_scr = rest[3 * pps + 1:]
    step = pl.program_id(1)
    nstep = pl.num_programs(1)
    rows = FOX_HEADS * lq
    q = q_ref[0]

    @pl.when(step == 0)
    def _():
        ncol = lq * FOX_HEADS
        knf = kn_ref[0].reshape(ncol, FOX_HEAD_DIM).astype(BF16)
        vnf = vn_ref[0].reshape(ncol, FOX_HEAD_DIM).astype(BF16)
        s = lax.dot_general(q, knf, (((1,), (1,)), ((), ())), preferred_element_type=F32) - cn_ref[0]
        r = _iota2((rows, ncol), 0)
        c = _iota2((rows, ncol), 1)
        valid = (c % FOX_HEADS == r // lq) & (c // FOX_HEADS <= r % lq)
        s = jnp.where(valid, s, NEG_BIG)
        m = jnp.max(s, axis=-1, keepdims=True)
        p = jnp.exp(s - m)
        m_scr[...] = m
        l_scr[...] = jnp.sum(p, axis=-1, keepdims=True)
        acc_scr[...] = jnp.dot(p.astype(BF16), vnf, preferred_element_type=F32)
        suf_scr[...] = jnp.zeros_like(suf_scr)

    page = k_refs[0].shape[2]
    ncol = page * FOX_HEADS
    own_head = _iota2((rows, ncol), 1) % FOX_HEADS == _iota2((rows, ncol), 0) // lq
    suf = suf_scr[...]
    scores = []
    for i in range(pps):
        r2 = r_refs[i][0]
        kf = k_refs[i][0, 0].reshape(ncol, FOX_HEAD_DIM).astype(BF16)
        s = lax.dot_general(q, kf, (((1,), (1,)), ((), ())), preferred_element_type=F32)
        scores.append(jnp.where(own_head, s + (r2[:, :ncol] + suf), NEG_BIG))
        suf = suf + r2[:, ncol:]
    suf_scr[...] = suf
    m = m_scr[...]
    m_new = m
    for s in scores:
        m_new = jnp.maximum(m_new, jnp.max(s, axis=-1, keepdims=True))
    alpha = jnp.exp(m - m_new)
    l = alpha * l_scr[...]
    acc = alpha * acc_scr[...]
    for i in range(pps):
        p = jnp.exp(scores[i] - m_new)
        l = l + jnp.sum(p, axis=-1, keepdims=True)
        vf = v_refs[i][0, 0].reshape(ncol, FOX_HEAD_DIM).astype(BF16)
        acc = acc + jnp.dot(p.astype(BF16), vf, preferred_element_type=F32)
    m_scr[...] = m_new
    l_scr[...] = l
    acc_scr[...] = acc

    @pl.when(step == nstep - 1)
    def _():
        o = _rms(acc / l, on_ref[...])
        o_ref[0] = jnp.concatenate([o[h * lq:(h + 1) * lq, :] for h in range(FOX_HEADS)], axis=-1)


def _fox_sample(page_table, layer, q3, kn4, vn4, cn3, onorm, cache_k, cache_v, r3, pps):
    b, lq = kn4.shape[0], kn4.shape[1]
    n_pages = page_table.shape[1]
    page = cache_k.shape[2]
    nstep = n_pages // pps
    rows = FOX_HEADS * lq
    ncol = page * FOX_HEADS

    def page_map(i):
        return lambda bi, s, pt: (layer, pt[bi, n_pages - 1 - (s * pps + i)], 0, 0, 0)

    def r_map(i):
        return lambda bi, s, pt: (pt[bi, n_pages - 1 - (s * pps + i)], 0, 0)

    in_specs = [pl.BlockSpec((1, rows, FOX_HEAD_DIM), lambda bi, s, pt: (bi, 0, 0)),
                pl.BlockSpec((1, lq, FOX_HEADS, FOX_HEAD_DIM), lambda bi, s, pt: (bi, 0, 0, 0)),
                pl.BlockSpec((1, lq, FOX_HEADS, FOX_HEAD_DIM), lambda bi, s, pt: (bi, 0, 0, 0)),
                pl.BlockSpec((1, 1, lq * FOX_HEADS), lambda bi, s, pt: (bi, 0, 0)),
                pl.BlockSpec(onorm.shape, lambda bi, s, pt: (0, 0))]
    in_specs += [pl.BlockSpec((1, 1, page, FOX_HEADS, FOX_HEAD_DIM), page_map(i)) for i in range(pps)]
    in_specs += [pl.BlockSpec((1, 1, page, FOX_HEADS, FOX_HEAD_DIM), page_map(i)) for i in range(pps)]
    in_specs += [pl.BlockSpec((1, 1, 2 * ncol), r_map(i)) for i in range(pps)]
    grid_spec = pltpu.PrefetchScalarGridSpec(
        num_scalar_prefetch=1,
        grid=(b, nstep),
        in_specs=in_specs,
        out_specs=pl.BlockSpec((1, lq, FOX_WIDTH), lambda bi, s, pt: (bi, 0, 0)),
        scratch_shapes=[pltpu.VMEM((rows, 1), F32), pltpu.VMEM((rows, 1), F32),
                        pltpu.VMEM((rows, FOX_HEAD_DIM), F32), pltpu.VMEM((1, ncol), F32)],
    )
    return pl.pallas_call(
        functools.partial(_fox_sample_kernel, pps=pps, lq=lq),
        grid_spec=grid_spec,
        out_shape=jax.ShapeDtypeStruct((b, lq, FOX_WIDTH), F32),
        compiler_params=_cparams(("parallel", "arbitrary")),
    )(page_table, q3, kn4, vn4, cn3, onorm, *([cache_k] * pps), *([cache_v] * pps), *([r3] * pps))


def _out_proj_kernel(x_ref, a_ref, b_ref, w_ref, o_ref):
    o_ref[...] = (x_ref[...]
                  + jnp.dot(a_ref[...].astype(BF16), w_ref[0:GDN_WIDTH, :], preferred_element_type=F32)
                  + jnp.dot(b_ref[...].astype(BF16), w_ref[GDN_WIDTH:, :], preferred_element_type=F32))


def _out_proj(x2, a2, b2, w, tm):
    t = x2.shape[0]
    return pl.pallas_call(
        _out_proj_kernel,
        grid=(t // tm,),
        in_specs=[pl.BlockSpec((tm, D_MODEL), lambda i: (i, 0)),
                  pl.BlockSpec((tm, GDN_WIDTH), lambda i: (i, 0)),
                  pl.BlockSpec((tm, FOX_WIDTH), lambda i: (i, 0)),
                  pl.BlockSpec(w.shape, lambda i: (0, 0))],
        out_specs=pl.BlockSpec((tm, D_MODEL), lambda i: (i, 0)),
        out_shape=jax.ShapeDtypeStruct((t, D_MODEL), F32),
        compiler_params=_cparams(("parallel",)),
    )(x2, a2, b2, w)


def _xattn_kernel(x_ref, g_ref, wq_ref, wo_ref, mk_ref, mv_ref, o_ref):
    x = x_ref[0]
    hb = _rms(x, g_ref[...]).astype(BF16)
    q = jnp.dot(hb, wq_ref[...], preferred_element_type=F32) * (XA_HEAD_DIM ** -0.5)
    mk = mk_ref[0].astype(BF16)
    mv = mv_ref[0].astype(BF16)
    outs = []
    for h in range(XA_HEADS):
        lo = h * XA_HEAD_DIM
        s = lax.dot_general(q[:, lo:lo + XA_HEAD_DIM].astype(BF16), mk[:, lo:lo + XA_HEAD_DIM],
                            (((1,), (1,)), ((), ())), preferred_element_type=F32)
        p = jnp.exp(s - jnp.max(s, axis=-1, keepdims=True))
        p = p / jnp.sum(p, axis=-1, keepdims=True)
        outs.append(jnp.dot(p.astype(BF16), mv[:, lo:lo + XA_HEAD_DIM], preferred_element_type=F32))
    o = jnp.concatenate(outs, axis=-1).astype(BF16)
    o_ref[0] = x + jnp.dot(o, wo_ref[...], preferred_element_type=F32)


def _xattn(x3, g, wq, wo, mk3, mv3, tq):
    b, seq, _ = x3.shape
    n_mem = mk3.shape[1]
    return pl.pallas_call(
        _xattn_kernel,
        grid=(b, seq // tq),
        in_specs=[pl.BlockSpec((1, tq, D_MODEL), lambda i, j: (i, j, 0)),
                  pl.BlockSpec(g.shape, lambda i, j: (0, 0)),
                  pl.BlockSpec(wq.shape, lambda i, j: (0, 0)),
                  pl.BlockSpec(wo.shape, lambda i, j: (0, 0)),
                  pl.BlockSpec((1, n_mem, XA_WIDTH), lambda i, j: (i, 0, 0)),
                  pl.BlockSpec((1, n_mem, XA_WIDTH), lambda i, j: (i, 0, 0))],
        out_specs=pl.BlockSpec((1, tq, D_MODEL), lambda i, j: (i, j, 0)),
        out_shape=jax.ShapeDtypeStruct((b, seq, D_MODEL), F32),
        compiler_params=_cparams(("parallel", "parallel")),
    )(x3, g, wq, wo, mk3, mv3)


def _router_kernel(x_ref, g_ref, wr_ref, br_ref, h_ref, route_ref, gate_ref, cnt_ref, base_scr, *, tm):
    i = pl.program_id(0)

    @pl.when(i == 0)
    def _():
        base_scr[...] = jnp.zeros_like(base_scr)

    h = _rms(x_ref[...], g_ref[...])
    _store_row_tiles(h_ref, h, tm)
    logits = _dot_hi(h, wr_ref[...]) + br_ref[...]
    lane = _iota2((tm, LANES), 1)
    lane_f = lane.astype(F32)
    vals, hots, idxs = [], [], []
    cur = logits
    for _ in range(TOP_K):
        mx = jnp.max(cur, axis=-1, keepdims=True)
        idx_f = jnp.min(jnp.where(cur == mx, lane_f, float(LANES)), axis=-1, keepdims=True)
        hot = lane_f == idx_f
        vals.append(mx)
        idxs.append(idx_f.astype(I32))
        hots.append(hot)
        cur = jnp.where(hot, -jnp.inf, cur)
    exps = [jnp.exp(v - vals[0]) for v in vals]
    denom = exps[0] + exps[1] + exps[2] + exps[3]
    member = (hots[0] | hots[1] | hots[2] | hots[3]).astype(F32)
    strict = (_iota2((tm, tm), 1) < _iota2((tm, tm), 0)).astype(BF16)
    before = jnp.dot(strict, member.astype(BF16), preferred_element_type=F32) + base_scr[...]
    route = jnp.zeros((tm, LANES), I32)
    gate = jnp.zeros((tm, LANES), F32)
    for k in range(TOP_K):
        rank = jnp.sum(jnp.where(hots[k], before, 0.0), axis=-1, keepdims=True).astype(I32)
        route = jnp.where(lane == k, idxs[k], route)
        route = jnp.where(lane == TOP_K + k, rank, route)
        gate = jnp.where(lane == k, exps[k] / denom, gate)
    route_ref[...] = route
    gate_ref[...] = gate
    base_scr[...] = base_scr[...] + jnp.sum(member, axis=0, keepdims=True)
    cnt_ref[...] = base_scr[...]


def _router(x2, g, wr, br, tm):
    t = x2.shape[0]
    return pl.pallas_call(
        functools.partial(_router_kernel, tm=tm),
        grid=(t // tm,),
        in_specs=[pl.BlockSpec((tm, D_MODEL), lambda i: (i, 0)),
                  pl.BlockSpec(g.shape, lambda i: (0, 0)),
                  pl.BlockSpec(wr.shape, lambda i: (0, 0)),
                  pl.BlockSpec(br.shape, lambda i: (0, 0))],
        out_specs=[pl.BlockSpec((tm * ROW_TILE, LANES), lambda i: (i, 0)),
                   pl.BlockSpec((tm, LANES), lambda i: (i, 0)),
                   pl.BlockSpec((tm, LANES), lambda i: (i, 0)),
                   pl.BlockSpec((1, LANES), lambda i: (0, 0))],
        out_shape=[jax.ShapeDtypeStruct((t * ROW_TILE, LANES), F32), jax.ShapeDtypeStruct((t, LANES), I32),
                   jax.ShapeDtypeStruct((t, LANES), F32), jax.ShapeDtypeStruct((1, LANES), F32)],
        scratch_shapes=[pltpu.VMEM((1, LANES), F32)],
        compiler_params=_cparams(("arbitrary",)),
    )(x2, g, wr, br)


def _dispatch_kernel(dest_ref, h_ref, init_hbm, xs_hbm, sem, *, tm):
    del init_hbm
    npair = tm * TOP_K

    def issue(p, c):
        src = pl.multiple_of((p // TOP_K) * ROW_TILE, ROW_TILE)
        dst = pl.multiple_of(dest_ref[0, 0, p] * ROW_TILE, ROW_TILE)
        pltpu.make_async_copy(h_ref.at[pl.ds(src, ROW_TILE)], xs_hbm.at[pl.ds(dst, ROW_TILE)], sem).start()
        return c

    lax.fori_loop(0, npair, issue, 0, unroll=8)
    for _ in range(TOP_K):
        pltpu.make_async_copy(h_ref, xs_hbm.at[pl.ds(0, tm * ROW_TILE)], sem).wait()


def _dispatch(dest3, h2, rows, tm):
    t = h2.shape[0] // ROW_TILE
    init = jnp.zeros((rows * ROW_TILE, LANES), F32)
    return pl.pallas_call(
        functools.partial(_dispatch_kernel, tm=tm),
        grid=(t // tm,),
        in_specs=[pl.BlockSpec((1, 1, tm * TOP_K), lambda i: (i, 0, 0), memory_space=pltpu.SMEM),
                  pl.BlockSpec((tm * ROW_TILE, LANES), lambda i: (i, 0)),
                  pl.BlockSpec(memory_space=pl.ANY)],
        out_specs=pl.BlockSpec(memory_space=pl.ANY),
        out_shape=jax.ShapeDtypeStruct((rows * ROW_TILE, LANES), F32),
        scratch_shapes=[pltpu.SemaphoreType.DMA(())],
        input_output_aliases={2: 0},
        compiler_params=pltpu.CompilerParams(dimension_semantics=("arbitrary",), vmem_limit_bytes=VMEM_LIMIT,
                                             has_side_effects=True),
    )(dest3, h2, init)


def _ffn_kernel(be_ref, nv_ref, x_ref, w1_ref, b1_ref, w2_ref, b2_ref, o_ref):
    i = pl.program_id(0)
    live = i * EXPERT_BLOCK < nv_ref[0]

    @pl.when(live)
    def _():
        x = _load_row_tiles(x_ref, EXPERT_BLOCK).astype(BF16)
        hb = jnp.dot(x, w1_ref[0], preferred_element_type=F32) + b1_ref[0]
        glu = jnp.minimum(hb[:, :D_FF], SWIGLU_LIMIT)
        lin = jnp.clip(hb[:, D_FF:], -SWIGLU_LIMIT, SWIGLU_LIMIT)
        act = glu * _sigmoid(SWIGLU_ALPHA * glu) * (lin + 1.0)
        y = jnp.dot(act.astype(BF16), w2_ref[0], preferred_element_type=F32) + b2_ref[0]
        _store_row_tiles(o_ref, y, EXPERT_BLOCK)

    @pl.when(jnp.logical_not(live))
    def _():
        o_ref[...] = jnp.zeros_like(o_ref)


def _ffn(block_e, nvalid, xs, w1, b1, w2, b2):
    rows = xs.shape[0] // ROW_TILE
    nb = rows // EXPERT_BLOCK
    blk = pl.BlockSpec((EXPERT_BLOCK * ROW_TILE, LANES), lambda i, be, nv: (i, 0))
    grid_spec = pltpu.PrefetchScalarGridSpec(
        num_scalar_prefetch=2,
        grid=(nb,),
        in_specs=[blk,
                  pl.BlockSpec((1, D_MODEL, 2 * D_FF), lambda i, be, nv: (be[i], 0, 0)),
                  pl.BlockSpec((1, 1, 2 * D_FF), lambda i, be, nv: (be[i], 0, 0)),
                  pl.BlockSpec((1, D_FF, D_MODEL), lambda i, be, nv: (be[i], 0, 0)),
                  pl.BlockSpec((1, 1, D_MODEL), lambda i, be, nv: (be[i], 0, 0))],
        out_specs=blk,
    )
    return pl.pallas_call(
        _ffn_kernel,
        grid_spec=grid_spec,
        out_shape=jax.ShapeDtypeStruct((rows * ROW_TILE, LANES), F32),
        compiler_params=_cparams(("arbitrary",)),
    )(block_e, nvalid, xs, w1, b1, w2, b2)


def _combine_kernel(dcur_ref, dnext_ref, x_ref, gate_ref, fg_ref, os_hbm, y_ref, yn_ref, buf, sem, *, tm):
    i = pl.program_id(0)
    n = pl.num_programs(0)
    npair = tm * TOP_K

    def issue_tile(dref, slot):
        def issue(p, c):
            src = pl.multiple_of(dref[0, 0, p] * ROW_TILE, ROW_TILE)
            dst = pl.multiple_of((p // TOP_K) * ROW_TILE, ROW_TILE)
            pltpu.make_async_copy(os_hbm.at[pl.ds(src, ROW_TILE)],
                                  buf.at[slot, p % TOP_K, pl.ds(dst, ROW_TILE)], sem.at[slot]).start()
            return c

        lax.fori_loop(0, npair, issue, 0, unroll=8)

    @pl.when(i == 0)
    def _():
        issue_tile(dcur_ref, 0)

    @pl.when(i + 1 < n)
    def _():
        issue_tile(dnext_ref, (i + 1) % 2)

    slot = i % 2
    for k in range(TOP_K):
        pltpu.make_async_copy(os_hbm.at[pl.ds(0, tm * ROW_TILE)], buf.at[slot, k], sem.at[slot]).wait()
    gate = gate_ref[...]
    x = x_ref[...]
    pieces = []
    for s in range(ROW_TILE):
        acc = x[:, s * LANES:(s + 1) * LANES]
        for k in range(TOP_K):
            acc = acc + gate[:, k:k + 1] * buf[slot, k, pl.ds(s, tm, stride=ROW_TILE), :]
        pieces.append(acc)
    y = jnp.concatenate(pieces, axis=-1)
    y_ref[...] = y
    yn_ref[...] = _rms(y, fg_ref[...])


def _combine(dest3, x2, gate, fg, os2, tm):
    t = x2.shape[0]
    nt = t // tm
    return pl.pallas_call(
        functools.partial(_combine_kernel, tm=tm),
        grid=(nt,),
        in_specs=[pl.BlockSpec((1, 1, tm * TOP_K), lambda i: (i, 0, 0), memory_space=pltpu.SMEM),
                  pl.BlockSpec((1, 1, tm * TOP_K), lambda i: (jnp.minimum(i + 1, nt - 1), 0, 0),
                               memory_space=pltpu.SMEM),
                  pl.BlockSpec((tm, D_MODEL), lambda i: (i, 0)),
                  pl.BlockSpec((tm, LANES), lambda i: (i, 0)),
                  pl.BlockSpec(fg.shape, lambda i: (0, 0)),
                  pl.BlockSpec(memory_space=pl.ANY)],
        out_specs=[pl.BlockSpec((tm, D_MODEL), lambda i: (i, 0)),
                   pl.BlockSpec((tm, D_MODEL), lambda i: (i, 0))],
        out_shape=[jax.ShapeDtypeStruct((t, D_MODEL), F32), jax.ShapeDtypeStruct((t, D_MODEL), F32)],
        scratch_shapes=[pltpu.VMEM((2, TOP_K, tm * ROW_TILE, LANES), F32), pltpu.SemaphoreType.DMA((2,))],
        compiler_params=_cparams(("arbitrary",)),
    )(dest3, dest3, x2, gate, fg, os2)


def _row_tile(t, want):
    tm = min(want, t)
    assert t % tm == 0
    return tm


def _mixer(x3, lw, conv_state8, s0, fox_fn):
    b, seq, _ = x3.shape
    t = b * seq
    x2 = x3.reshape(t, D_MODEL)
    qkv, z, fq, fk, fv, sm, smt = _in_proj(x2, lw["norm1_g"], lw["w_main"], lw["w_small"], lw["w_small_t"],
                                           _row_tile(t, 256))
    sm3 = sm.reshape(b, seq, LANES)
    smt3 = jnp.transpose(smt.reshape(16, b, seq), (1, 0, 2))
    logf, ct = _logf(sm3, smt3, lw["fb_row"], lw["fb_col"])
    c = min(GDN_CHUNK, seq)
    tl = min(256, seq)
    gdn_out, s_new, cbuf = _gdn(qkv.reshape(b, seq, CONV_CH), z.reshape(b, seq, GDN_WIDTH), sm3, smt3,
                                conv_state8, s0, lw["conv_w"], lw["gp_row"], lw["gp_col"], tl, c,
                                exact_small=(c <= SUBLANES))
    fq3 = fq.reshape(b, seq, FOX_WIDTH)
    fk3 = fk.reshape(b, seq, FOX_WIDTH)
    fv3 = fv.reshape(b, seq, FOX_WIDTH)
    fox_out = fox_fn(fq3, fk3, fv3, ct)
    y2 = _out_proj(x2, gdn_out.reshape(t, GDN_WIDTH), fox_out.reshape(t, FOX_WIDTH), lw["w_out"],
                   _row_tile(t, 512))
    return (y2.reshape(b, seq, D_MODEL), s_new, cbuf,
            fk3.reshape(b, seq, FOX_HEADS, FOX_HEAD_DIM), fv3.reshape(b, seq, FOX_HEADS, FOX_HEAD_DIM), logf)


def _moe(x3, lw, final_g):
    b, seq, _ = x3.shape
    t = b * seq
    x2 = x3.reshape(t, D_MODEL)
    h2, route, gate, counts = _router(x2, lw["norm3_g"], lw["w_router"], lw["b_router"], _row_tile(t, 256))
    cnt = counts[0, :N_EXPERTS].astype(I32)
    padded = (cnt + EXPERT_BLOCK - 1) // EXPERT_BLOCK * EXPERT_BLOCK
    pends = jnp.cumsum(padded)
    pstart = pends - padded
    nb = t * TOP_K // EXPERT_BLOCK + N_EXPERTS
    rows = nb * EXPERT_BLOCK
    block_row0 = jnp.arange(nb, dtype=I32) * EXPERT_BLOCK
    block_e = jnp.minimum(jnp.sum((pends[None, :] <= block_row0[:, None]).astype(I32), axis=1), N_EXPERTS - 1)
    nvalid = pends[-1:].astype(I32)
    hot = route[:, :TOP_K, None] == jnp.arange(N_EXPERTS, dtype=I32)[None, None, :]
    dest = jnp.sum(jnp.where(hot, pstart[None, None, :], 0), axis=-1) + route[:, TOP_K:2 * TOP_K]
    tmd = _row_tile(t, 256)
    xs = _dispatch(dest.reshape(t // tmd, 1, tmd * TOP_K), h2, rows, tmd)
    os2 = _ffn(block_e, nvalid, xs, lw["w1"], lw["b1"], lw["w2"], lw["b2"])
    tmc = _row_tile(t, 256)
    y2, yn2 = _combine(dest.reshape(t // tmc, 1, tmc * TOP_K), x2, gate, final_g, os2, tmc)
    return y2.reshape(b, seq, D_MODEL), yn2.reshape(b, seq, D_MODEL)


def _prep_layer(l, norm1_g, w_in, conv_w, gdn_a_log, gdn_dt_bias, gdn_onorm, fox_fbias, fox_onorm, w_out,
                norm2_g, mem_norm_g, w_xq, w_mkv, w_xo, norm3_g, w_router, b_router, w1, b1, w2, b2):
    wi = w_in[l]
    gq, gk, gv, gz, gb, ga, fq, fk, fv, ff = _split_in(wi)
    w_main = jnp.concatenate([gq, gk, gv, gz, fq, fk, fv], axis=1).astype(BF16)
    w_small = jnp.concatenate([gb, ga, ff, jnp.zeros((D_MODEL, LANES - 16), F32)], axis=1).astype(BF16)
    w_small_t = jnp.transpose(w_small[:, :16])
    lanes = lambda v, off: jnp.zeros((LANES,), F32).at[off:off + v.shape[0]].set(v)
    gp_row = jnp.zeros((SUBLANES, LANES), F32)
    gp_row = gp_row.at[0].set(lanes(gdn_a_log[l], GDN_HEADS)).at[1].set(lanes(gdn_dt_bias[l], GDN_HEADS))
    gp_row = gp_row.at[2].set(gdn_onorm[l])
    gp_col = jnp.zeros((16, LANES), F32)
    gp_col = gp_col.at[GDN_HEADS:2 * GDN_HEADS, 0].set(gdn_a_log[l]).at[GDN_HEADS:2 * GDN_HEADS, 1].set(gdn_dt_bias[l])
    fb_row = jnp.zeros((1, LANES), F32).at[0, 8:16].set(fox_fbias[l])
    fb_col = jnp.zeros((16, LANES), F32).at[8:16, 0].set(fox_fbias[l])
    wr = jnp.concatenate([w_router[l], jnp.zeros((D_MODEL, LANES - N_EXPERTS), F32)], axis=1)
    br = jnp.full((1, LANES), NEG_BIG, F32).at[0, :N_EXPERTS].set(b_router[l])
    return {
        "norm1_g": norm1_g[l].reshape(1, D_MODEL), "w_main": w_main, "w_small": w_small, "w_small_t": w_small_t,
        "conv_w": conv_w[l], "gp_row": gp_row, "gp_col": gp_col, "fb_row": fb_row, "fb_col": fb_col,
        "fox_onorm": fox_onorm[l].reshape(1, FOX_HEAD_DIM), "w_out": w_out[l].astype(BF16),
        "norm2_g": norm2_g[l].reshape(1, D_MODEL), "mem_norm_g": mem_norm_g[l].reshape(1, D_MODEL),
        "w_xq": w_xq[l].astype(BF16), "w_mkv": w_mkv[l].astype(BF16), "w_xo": w_xo[l].astype(BF16),
        "norm3_g": norm3_g[l].reshape(1, D_MODEL), "w_router": wr, "b_router": br,
        "w1": w1[l].astype(BF16), "b1": b1[l].reshape(N_EXPERTS, 1, 2 * D_FF),
        "w2": w2[l].astype(BF16), "b2": b2[l].reshape(N_EXPERTS, 1, D_MODEL),
    }


def _split_in(wi):
    widths = (GDN_WIDTH, GDN_WIDTH, GDN_WIDTH, GDN_WIDTH, GDN_HEADS, GDN_HEADS,
              FOX_WIDTH, FOX_WIDTH, FOX_WIDTH, FOX_HEADS)
    outs, c0 = [], 0
    for w in widths:
        outs.append(wi[:, c0:c0 + w])
        c0 += w
    return outs


def kernel(x_prompt, x_sample, mem_prompt, cache_fox_k, cache_fox_v, cache_fox_logf, cache_mem_k, cache_mem_v, state_gdn, state_conv, page_table, norm1_g, w_in, conv_w, gdn_a_log, gdn_dt_bias, gdn_onorm, fox_fbias, fox_onorm, w_out, norm2_g, mem_norm_g, w_xq, w_mkv, w_xo, norm3_g, w_router, b_router, w1, b1, w2, b2, final_norm_g):
    depth = w_in.shape[0]
    bp, lp, _ = x_prompt.shape
    bs, ls, _ = x_sample.shape
    n_mem = mem_prompt.shape[1]
    n_pool, page = cache_fox_k.shape[1], cache_fox_k.shape[2]
    ncol = page * FOX_HEADS
    lf_flat = cache_fox_logf.reshape(depth, n_pool, ncol)
    col = jnp.arange(ncol, dtype=I32)
    same_head = col[:, None] % FOX_HEADS == col[None, :] % FOX_HEADS
    later = col[:, None] // FOX_HEADS > col[None, :] // FOX_HEADS
    w_suffix = jnp.concatenate([same_head & later, same_head], axis=1).astype(BF16)
    final_g = final_norm_g.reshape(1, D_MODEL)
    xp, xs = x_prompt, x_sample
    yp = ys = None
    outs = {k: [] for k in ("fkp", "fvp", "flp", "mkp", "mvp", "sgp", "scp", "fks", "fvs", "fls", "sgs", "scs")}
    for l in range(depth):
        lw = _prep_layer(l, norm1_g, w_in, conv_w, gdn_a_log, gdn_dt_bias, gdn_onorm, fox_fbias, fox_onorm,
                         w_out, norm2_g, mem_norm_g, w_xq, w_mkv, w_xo, norm3_g, w_router, b_router,
                         w1, b1, w2, b2)
        tq = min(256, lp)

        def fox_p(fq3, fk3, fv3, ct, lw=lw, tq=tq):
            return _fox_prompt(fq3, fk3, fv3, ct.reshape(bp, FOX_HEADS, lp // tq, tq), lw["fox_onorm"], tq)

        xp, s_new, cbuf, k_new, v_new, lf_new = _mixer(
            xp, lw, jnp.zeros((bp, SUBLANES, CONV_CH), F32),
            jnp.zeros((bp, GDN_HEADS, GDN_DK, GDN_DV), F32), fox_p)
        outs["fkp"].append(k_new); outs["fvp"].append(v_new); outs["flp"].append(lf_new)
        outs["sgp"].append(s_new); outs["scp"].append(cbuf)

        r3 = _logf_pages(lf_flat[l], w_suffix).reshape(n_pool, 1, 2 * page * FOX_HEADS)

        def fox_s(fq3, fk3, fv3, ct, lw=lw, l=l, r3=r3):
            q4 = fq3.reshape(bs, ls, FOX_HEADS, FOX_HEAD_DIM) * (FOX_HEAD_DIM ** -0.5)
            q3 = jnp.transpose(q4, (0, 2, 1, 3)).reshape(bs, FOX_HEADS * ls, FOX_HEAD_DIM)
            cn3 = jnp.transpose(ct, (0, 2, 1)).reshape(bs, 1, ls * FOX_HEADS)
            return _fox_sample(page_table, l, q3.astype(BF16), fk3.reshape(bs, ls, FOX_HEADS, FOX_HEAD_DIM),
                               fv3.reshape(bs, ls, FOX_HEADS, FOX_HEAD_DIM), cn3, lw["fox_onorm"],
                               cache_fox_k, cache_fox_v, r3, pps=8)

        cst8 = jnp.pad(state_conv[l], ((0, 0), (SUBLANES - (CONV_W - 1), 0), (0, 0)))
        xs, s_new, cbuf, k_new, v_new, lf_new = _mixer(xs, lw, cst8, state_gdn[l], fox_s)
        outs["fks"].append(k_new); outs["fvs"].append(v_new); outs["fls"].append(lf_new)
        outs["sgs"].append(s_new); outs["scs"].append(cbuf)

        mk2, mv2 = _norm_proj(mem_prompt.reshape(bp * n_mem, D_MODEL), lw["mem_norm_g"], lw["w_mkv"],
                              (XA_WIDTH, XA_WIDTH), _row_tile(bp * n_mem, 512))
        outs["mkp"].append(mk2.reshape(bp, n_mem, XA_HEADS, XA_HEAD_DIM))
        outs["mvp"].append(mv2.reshape(bp, n_mem, XA_HEADS, XA_HEAD_DIM))
        xp = _xattn(xp, lw["norm2_g"], lw["w_xq"], lw["w_xo"], mk2.reshape(bp, n_mem, XA_WIDTH),
                    mv2.reshape(bp, n_mem, XA_WIDTH), min(512, lp))
        xs = _xattn(xs, lw["norm2_g"], lw["w_xq"], lw["w_xo"], cache_mem_k[l].reshape(bs, n_mem, XA_WIDTH),
                    cache_mem_v[l].reshape(bs, n_mem, XA_WIDTH), ls)

        xp, yp = _moe(xp, lw, final_g)
        xs, ys = _moe(xs, lw, final_g)
    st = jnp.stack
    return (yp, ys, st(outs["fkp"]), st(outs["fvp"]), st(outs["flp"]), st(outs["mkp"]), st(outs["mvp"]),
            st(outs["sgp"]), st(outs["scp"]), st(outs["fks"]), st(outs["fvs"]), st(outs["fls"]),
            st(outs["sgs"]), st(outs["scs"]))
```

```python
import functools
import math

import jax
import jax.numpy as jnp
from jax import lax
from jax.experimental import pallas as pl
from jax.experimental.pallas import tpu as pltpu

F32 = jnp.float32
BF16 = jnp.bfloat16
I32 = jnp.int32
HIGHEST = lax.Precision.HIGHEST

D_MODEL = 1024
GDN_HEADS = 4
GDN_DK = 128
GDN_DV = 128
GDN_WIDTH = GDN_HEADS * GDN_DV
CONV_W = 4
CONV_CH = 3 * GDN_WIDTH
GDN_CHUNK = 64
FOX_HEADS = 8
FOX_HEAD_DIM = 64
FOX_WIDTH = FOX_HEADS * FOX_HEAD_DIM
XA_HEADS = 4
XA_HEAD_DIM = 128
XA_WIDTH = XA_HEADS * XA_HEAD_DIM
N_EXPERTS = 32
TOP_K = 4
D_FF = D_MODEL
SWIGLU_LIMIT = 7.0
SWIGLU_ALPHA = 1.702
NORM_EPS = 1e-6
NEG_BIG = -1e30

LANES = 128
SUBLANES = 8
VMEM_LIMIT = 52 * 1024 * 1024
MAIN_COLS = 2 * GDN_WIDTH + 2 * GDN_WIDTH + 3 * FOX_WIDTH
EXPERT_BLOCK = 256


def _cparams(sem):
    return pltpu.CompilerParams(dimension_semantics=sem, vmem_limit_bytes=VMEM_LIMIT)


def _dot(a, b):
    return jnp.dot(a.astype(BF16), b.astype(BF16), preferred_element_type=F32)


def _dot_nt(a, b):
    return lax.dot_general(a.astype(BF16), b.astype(BF16), (((1,), (1,)), ((), ())),
                           preferred_element_type=F32)


def _dot_tn(a, b):
    return lax.dot_general(a.astype(BF16), b.astype(BF16), (((0,), (0,)), ((), ())),
                           preferred_element_type=F32)


def _dot_hi(a, b):
    return jnp.dot(a, b, precision=HIGHEST, preferred_element_type=F32)


def _dot_nt_hi(a, b):
    return lax.dot_general(a, b, (((1,), (1,)), ((), ())), precision=HIGHEST,
                           preferred_element_type=F32)


def _rms(x, g):
    return x * lax.rsqrt(jnp.mean(x * x, axis=-1, keepdims=True) + NORM_EPS) * g


def _sigmoid(x):
    return 1.0 / (1.0 + jnp.exp(-x))


def _softplus(x):
    return jnp.maximum(x, 0.0) + jnp.log1p(jnp.exp(-jnp.abs(x)))


def _log_sigmoid(x):
    return jnp.minimum(x, 0.0) - jnp.log1p(jnp.exp(-jnp.abs(x)))


def _iota2(shape, dim):
    return lax.broadcasted_iota(I32, shape, dim)


ROW_TILE = D_MODEL // LANES


def _store_row_tiles(ref, val, n):
    for s in range(ROW_TILE):
        ref[pl.ds(s, n, stride=ROW_TILE), :] = val[:, s * LANES:(s + 1) * LANES]


def _load_row_tiles(ref, n):
    return jnp.concatenate([ref[pl.ds(s, n, stride=ROW_TILE), :] for s in range(ROW_TILE)], axis=-1)


def _in_proj_kernel(x_ref, g_ref, wm_ref, ws_ref, wst_ref,
                    qkv_ref, z_ref, fq_ref, fk_ref, fv_ref, sm_ref, smt_ref):
    hb = _rms(x_ref[...], g_ref[...]).astype(BF16)
    c0 = 0
    for ref, width in ((qkv_ref, CONV_CH), (z_ref, GDN_WIDTH), (fq_ref, FOX_WIDTH),
                       (fk_ref, FOX_WIDTH), (fv_ref, FOX_WIDTH)):
        ref[...] = jnp.dot(hb, wm_ref[:, c0:c0 + width], preferred_element_type=F32)
        c0 += width
    sm_ref[...] = jnp.dot(hb, ws_ref[...], preferred_element_type=F32)
    smt_ref[...] = lax.dot_general(wst_ref[...], hb, (((1,), (1,)), ((), ())),
                                   preferred_element_type=F32)


def _in_proj(x2, g, wm, ws, wst, tm):
    t = x2.shape[0]
    row = lambda w: pl.BlockSpec((tm, w), lambda i: (i, 0))
    full = lambda a: pl.BlockSpec(a.shape, lambda i: (0,) * a.ndim)
    return pl.pallas_call(
        _in_proj_kernel,
        grid=(t // tm,),
        in_specs=[row(D_MODEL), full(g), full(wm), full(ws), full(wst)],
        out_specs=[row(CONV_CH), row(GDN_WIDTH), row(FOX_WIDTH), row(FOX_WIDTH), row(FOX_WIDTH),
                   row(LANES), pl.BlockSpec((16, tm), lambda i: (0, i))],
        out_shape=[jax.ShapeDtypeStruct((t, CONV_CH), F32), jax.ShapeDtypeStruct((t, GDN_WIDTH), F32),
                   jax.ShapeDtypeStruct((t, FOX_WIDTH), F32), jax.ShapeDtypeStruct((t, FOX_WIDTH), F32),
                   jax.ShapeDtypeStruct((t, FOX_WIDTH), F32), jax.ShapeDtypeStruct((t, LANES), F32),
                   jax.ShapeDtypeStruct((16, t), F32)],
        compiler_params=_cparams(("parallel",)),
    )(x2, g, wm, ws, wst)


def _norm_proj_kernel(x_ref, g_ref, w_ref, *out_refs):
    hb = _rms(x_ref[...], g_ref[...]).astype(BF16)
    c0 = 0
    for ref in out_refs:
        width = ref.shape[-1]
        ref[...] = jnp.dot(hb, w_ref[:, c0:c0 + width], preferred_element_type=F32)
        c0 += width


def _norm_proj(x2, g, w, widths, tm):
    t = x2.shape[0]
    return pl.pallas_call(
        _norm_proj_kernel,
        grid=(t // tm,),
        in_specs=[pl.BlockSpec((tm, D_MODEL), lambda i: (i, 0)),
                  pl.BlockSpec(g.shape, lambda i: (0, 0)),
                  pl.BlockSpec(w.shape, lambda i: (0, 0))],
        out_specs=[pl.BlockSpec((tm, wd), lambda i: (i, 0)) for wd in widths],
        out_shape=[jax.ShapeDtypeStruct((t, wd), F32) for wd in widths],
        compiler_params=_cparams(("parallel",)),
    )(x2, g, w)


def _small_mm(a, b):
    acc = a[:, 0:1] * b[0:1, :]
    for i in range(1, a.shape[1]):
        acc = acc + a[:, i:i + 1] * b[i:i + 1, :]
    return acc


def _logf_kernel(sm_ref, smt_ref, fbr_ref, fbc_ref, logf_ref, ct_ref, *, seq, chunk):
    ff = sm_ref[0][:, 8:16] + fbr_ref[:, 8:16]
    logf_ref[0] = _log_sigmoid(ff)
    lft = _log_sigmoid(smt_ref[0][8:16, :] + fbc_ref[8:16, 0:1])
    tri = (_iota2((chunk, chunk), 0) <= _iota2((chunk, chunk), 1)).astype(F32)
    carry = jnp.zeros((FOX_HEADS, 1), F32)
    for c in range(seq // chunk):
        blk = lft[:, c * chunk:(c + 1) * chunk]
        cs = (_small_mm(blk, tri) if chunk <= SUBLANES else _dot_hi(blk, tri)) + carry
        ct_ref[0, :, c * chunk:(c + 1) * chunk] = cs
        carry = cs[:, chunk - 1:chunk]


def _logf(sm3, smt3, fbr, fbc):
    b, seq, _ = sm3.shape
    chunk = min(256, seq)
    return pl.pallas_call(
        functools.partial(_logf_kernel, seq=seq, chunk=chunk),
        grid=(b,),
        in_specs=[pl.BlockSpec((1, seq, LANES), lambda i: (i, 0, 0)),
                  pl.BlockSpec((1, 16, seq), lambda i: (i, 0, 0)),
                  pl.BlockSpec(fbr.shape, lambda i: (0, 0)),
                  pl.BlockSpec(fbc.shape, lambda i: (0, 0))],
        out_specs=[pl.BlockSpec((1, seq, FOX_HEADS), lambda i: (i, 0, 0)),
                   pl.BlockSpec((1, FOX_HEADS, seq), lambda i: (i, 0, 0))],
        out_shape=[jax.ShapeDtypeStruct((b, seq, FOX_HEADS), F32),
                   jax.ShapeDtypeStruct((b, FOX_HEADS, seq), F32)],
        compiler_params=_cparams(("parallel",)),
    )(sm3, smt3, fbr, fbc)


def _unit_lower_inverse(a, c, mm):
    eye = (_iota2((c, c), 0) == _iota2((c, c), 1)).astype(F32)
    n = -a
    t = eye + n
    p = n
    levels = int(math.log2(c))
    for _ in range(levels - 1):
        p = mm(p, p)
        t = t + mm(t, p)
    return t


def _gdn_kernel(qkv_ref, z_ref, sm_ref, smt_ref, cst_ref, s0_ref, cw_ref, gpr_ref, gpc_ref,
                o_ref, sn_ref, cb_ref, s_scr, tail_scr, *, tl, c, exact_small):
    t = pl.program_id(1)
    nt = pl.num_programs(1)

    @pl.when(t == 0)
    def _():
        s_scr[...] = s0_ref[0]
        tail_scr[...] = cst_ref[0]

    rb = (lambda v: v.astype(BF16).astype(F32)) if exact_small else (lambda v: v.astype(BF16))
    bdims = ((0,), (0,))
    mm = lambda a, b: lax.dot_general(rb(a), rb(b), (((2,), (1,)), bdims), preferred_element_type=F32)
    mm_nt = lambda a, b: lax.dot_general(rb(a), rb(b), (((2,), (2,)), bdims), preferred_element_type=F32)
    mm_tn = lambda a, b: lax.dot_general(rb(a), rb(b), (((1,), (1,)), bdims), preferred_element_type=F32)

    x = qkv_ref[0]
    tail = tail_scr[...]
    cw = cw_ref[...]
    row8 = _iota2((SUBLANES, CONV_CH), 0)
    acc = x * cw[CONV_W - 1:CONV_W, :]
    for s in range(1, CONV_W):
        xs = pltpu.roll(x, s, 0)
        head = jnp.where(row8 < s, pltpu.roll(tail, s, 0), xs[0:SUBLANES])
        xs = head if tl == SUBLANES else jnp.concatenate([head, xs[SUBLANES:]], axis=0)
        acc = acc + xs * cw[CONV_W - 1 - s:CONV_W - s, :]
    conv = acc * _sigmoid(acc)
    tail_scr[...] = x[tl - SUBLANES:tl, :]

    @pl.when(t == nt - 1)
    def _():
        cb_ref[0] = x[tl - (CONV_W - 1):tl, :]

    sm = sm_ref[0]
    beta_c = _sigmoid(sm)
    g_c = -jnp.exp(gpr_ref[0:1, :]) * _softplus(sm + gpr_ref[1:2, :])
    smt = smt_ref[0]
    g_r = -jnp.exp(gpc_ref[:, 0:1]) * _softplus(smt + gpc_ref[:, 1:2])
    onorm = gpr_ref[2:3, :]

    ii = _iota2((c, c), 0)
    jj = _iota2((c, c), 1)
    tri_c = (jj <= ii).astype(F32)
    tri_r = (ii <= jj).astype(F32)
    small = c <= SUBLANES
    nc = tl // c
    cum_cols, cum_rows = [], []
    for ci in range(nc):
        gc_blk = g_c[ci * c:(ci + 1) * c, :]
        gr_blk = g_r[:, ci * c:(ci + 1) * c]
        cum_cols.append(_small_mm(tri_c, gc_blk) if small else _dot_hi(tri_c, gc_blk))
        cum_rows.append(_small_mm(gr_blk, tri_r) if small else _dot_hi(gr_blk, tri_r))

    def per_pair(fn):
        return jnp.stack([fn(ci, h) for ci in range(nc) for h in range(GDN_HEADS)], axis=0)

    rows = lambda ci: slice(ci * c, (ci + 1) * c)
    qs = per_pair(lambda ci, h: conv[rows(ci), h * GDN_DK:(h + 1) * GDN_DK])
    ks = per_pair(lambda ci, h: conv[rows(ci), GDN_WIDTH + h * GDN_DK:GDN_WIDTH + (h + 1) * GDN_DK])
    vs = per_pair(lambda ci, h: conv[rows(ci), 2 * GDN_WIDTH + h * GDN_DV:2 * GDN_WIDTH + (h + 1) * GDN_DV])
    beta = per_pair(lambda ci, h: beta_c[rows(ci), h:h + 1])
    cum_c = per_pair(lambda ci, h: cum_cols[ci][:, GDN_HEADS + h:GDN_HEADS + h + 1])
    cum_r = per_pair(lambda ci, h: cum_rows[ci][GDN_HEADS + h:GDN_HEADS + h + 1, :])
    qs = qs * lax.rsqrt(jnp.sum(qs * qs, axis=-1, keepdims=True) + NORM_EPS) * (GDN_DK ** -0.5)
    ks = ks * lax.rsqrt(jnp.sum(ks * ks, axis=-1, keepdims=True) + NORM_EPS)
    cum_last = cum_c[:, c - 1:c, :]
    dec = jnp.exp(jnp.where(ii >= jj, cum_c - cum_r, NEG_BIG))
    dec_strict = jnp.where(ii > jj, dec, 0.0)
    a_mat = beta * mm_nt(ks, ks) * dec_strict
    tinv = _unit_lower_inverse(a_mat, c, mm)
    e_cum = jnp.exp(cum_c)
    rhs = jnp.concatenate([beta * vs, (beta * e_cum) * ks], axis=-1)
    sol = mm(tinv, rhs)
    w_v, w_k = sol[:, :, :GDN_DV], sol[:, :, GDN_DV:]
    p_qk = mm_nt(qs, ks) * dec
    q_g = qs * e_cum
    k_d = ks * jnp.exp(cum_last - cum_c)
    g_end = jnp.exp(cum_last)

    state = s_scr[...]
    for ci in range(nc):
        pr = slice(ci * GDN_HEADS, (ci + 1) * GDN_HEADS)
        u = w_v[pr] - mm(w_k[pr], state)
        o = mm(q_g[pr], state) + mm(p_qk[pr], u)
        state = g_end[pr] * state + mm_tn(k_d[pr], u)
        for h in range(GDN_HEADS):
            lo = h * GDN_DV
            zh = z_ref[0, rows(ci), lo:lo + GDN_DV]
            o_ref[0, rows(ci), lo:lo + GDN_DV] = _rms(o[h], onorm) * (zh * _sigmoid(zh))
    s_scr[...] = state

    @pl.when(t == nt - 1)
    def _():
        sn_ref[0] = state


def _gdn(qkv3, z3, sm3, smt3, cst8, s0, cw, gpr, gpc, tl, c, exact_small):
    b, seq, _ = qkv3.shape
    nt = seq // tl
    full2 = lambda a: pl.BlockSpec(a.shape, lambda i, j: (0, 0))
    return pl.pallas_call(
        functools.partial(_gdn_kernel, tl=tl, c=c, exact_small=exact_small),
        grid=(b, nt),
        in_specs=[pl.BlockSpec((1, tl, CONV_CH), lambda i, j: (i, j, 0)),
                  pl.BlockSpec((1, tl, GDN_WIDTH), lambda i, j: (i, j, 0)),
                  pl.BlockSpec((1, tl, LANES), lambda i, j: (i, j, 0)),
                  pl.BlockSpec((1, 16, tl), lambda i, j: (i, 0, j)),
                  pl.BlockSpec((1, SUBLANES, CONV_CH), lambda i, j: (i, 0, 0)),
                  pl.BlockSpec((1, GDN_HEADS, GDN_DK, GDN_DV), lambda i, j: (i, 0, 0, 0)),
                  full2(cw), full2(gpr), full2(gpc)],
        out_specs=[pl.BlockSpec((1, tl, GDN_WIDTH), lambda i, j: (i, j, 0)),
                   pl.BlockSpec((1, GDN_HEADS, GDN_DK, GDN_DV), lambda i, j: (i, 0, 0, 0)),
                   pl.BlockSpec((1, CONV_W - 1, CONV_CH), lambda i, j: (i, 0, 0))],
        out_shape=[jax.ShapeDtypeStruct((b, seq, GDN_WIDTH), F32),
                   jax.ShapeDtypeStruct((b, GDN_HEADS, GDN_DK, GDN_DV), F32),
                   jax.ShapeDtypeStruct((b, CONV_W - 1, CONV_CH), F32)],
        scratch_shapes=[pltpu.VMEM((GDN_HEADS, GDN_DK, GDN_DV), F32),
                        pltpu.VMEM((SUBLANES, CONV_CH), F32)],
        compiler_params=_cparams(("parallel", "arbitrary")),
    )(qkv3, z3, sm3, smt3, cst8, s0, cw, gpr, gpc)


def _fox_prompt_kernel(fq_ref, fk_ref, fv_ref, ct_ref, on_ref, o_ref, kb_scr, vb_scr, *, tq):
    qi = pl.program_id(1)

    @pl.when(qi == 0)
    def _():
        kb_scr[...] = fk_ref[0].astype(BF16)
        vb_scr[...] = fv_ref[0].astype(BF16)

    q_all = fq_ref[0] * (FOX_HEAD_DIM ** -0.5)
    onorm = on_ref[...]
    causal = _iota2((tq, tq), 1) <= _iota2((tq, tq), 0)
    outs = []
    for h in range(FOX_HEADS):
        lo = h * FOX_HEAD_DIM
        qh = q_all[:, lo:lo + FOX_HEAD_DIM].astype(BF16)

        def step(j, carry, masked, lo=lo, qh=qh, h=h):
            m, l, acc = carry
            r0 = pl.multiple_of(j * tq, tq)
            ks = kb_scr[pl.ds(r0, tq), lo:lo + FOX_HEAD_DIM]
            vs = vb_scr[pl.ds(r0, tq), lo:lo + FOX_HEAD_DIM]
            cj = ct_ref[0, h, pl.ds(j, 1), :]
            s = lax.dot_general(qh, ks, (((1,), (1,)), ((), ())), preferred_element_type=F32) - cj
            if masked:
                s = jnp.where(causal, s, NEG_BIG)
            m_new = jnp.maximum(m, jnp.max(s, axis=-1, keepdims=True))
            p = jnp.exp(s - m_new)
            alpha = jnp.exp(m - m_new)
            l = alpha * l + jnp.sum(p, axis=-1, keepdims=True)
            acc = alpha * acc + jnp.dot(p.astype(BF16), vs, preferred_element_type=F32)
            return m_new, l, acc

        init = (jnp.full((tq, 1), NEG_BIG, F32), jnp.zeros((tq, 1), F32),
                jnp.zeros((tq, FOX_HEAD_DIM), F32))
        carry = lax.fori_loop(0, qi, functools.partial(step, masked=False), init)
        m, l, acc = step(qi, carry, True)
        outs.append(_rms(acc / l, onorm))
    o_ref[0] = jnp.concatenate(outs, axis=-1)


def _fox_prompt(fq3, fk3, fv3, ct4, onorm, tq):
    b, seq, _ = fq3.shape
    nq = seq // tq
    return pl.pallas_call(
        functools.partial(_fox_prompt_kernel, tq=tq),
        grid=(b, nq),
        in_specs=[pl.BlockSpec((1, tq, FOX_WIDTH), lambda i, j: (i, j, 0)),
                  pl.BlockSpec((1, seq, FOX_WIDTH), lambda i, j: (i, 0, 0)),
                  pl.BlockSpec((1, seq, FOX_WIDTH), lambda i, j: (i, 0, 0)),
                  pl.BlockSpec((1, FOX_HEADS, nq, tq), lambda i, j: (i, 0, 0, 0)),
                  pl.BlockSpec(onorm.shape, lambda i, j: (0, 0))],
        out_specs=pl.BlockSpec((1, tq, FOX_WIDTH), lambda i, j: (i, j, 0)),
        out_shape=jax.ShapeDtypeStruct((b, seq, FOX_WIDTH), F32),
        scratch_shapes=[pltpu.VMEM((seq, FOX_WIDTH), BF16), pltpu.VMEM((seq, FOX_WIDTH), BF16)],
        compiler_params=_cparams(("parallel", "arbitrary")),
    )(fq3, fk3, fv3, ct4, onorm)


def _logf_pages_kernel(lf_ref, w_ref, o_ref):
    lf = lf_ref[...]
    a = lf.astype(BF16)
    r1 = lf - a.astype(F32)
    b = r1.astype(BF16)
    c = (r1 - b.astype(F32)).astype(BF16)
    w = w_ref[...]
    o_ref[...] = (jnp.dot(a, w, preferred_element_type=F32) + jnp.dot(b, w, preferred_element_type=F32)
                  + jnp.dot(c, w, preferred_element_type=F32))


def _logf_pages(lf_flat, w2):
    n_pool, width = lf_flat.shape
    tp = next(c for c in (2048, 1024, 512, 256, 128, 64, 32, 16, 8) if n_pool % c == 0)
    return pl.pallas_call(
        _logf_pages_kernel,
        grid=(n_pool // tp,),
        in_specs=[pl.BlockSpec((tp, width), lambda i: (i, 0)),
                  pl.BlockSpec(w2.shape, lambda i: (0, 0))],
        out_specs=pl.BlockSpec((tp, 2 * width), lambda i: (i, 0)),
        out_shape=jax.ShapeDtypeStruct((n_pool, 2 * width), F32),
        compiler_params=_cparams(("parallel",)),
    )(lf_flat, w2)


def _fox_sample_kernel(pt_ref, qbd_ref, kn_ref, vn_ref, cn_ref, on_ref, *rest, pps, lq):
    k_refs = rest[0:pps]
    v_refs = rest[pps:2 * pps]
    r_refs = rest[2 * pps:3 * pps]
    o_ref = rest[3 * pps]
    m_scr, l_scr, acc_scr, suf_scr = rest[3 * pps + 1:]
    step = pl.program_id(1)
    nstep = pl.num_programs(1)
    rows = lq * FOX_HEADS
    qbd = qbd_ref[0]
    tile_rows = lambda v: jnp.concatenate([v] * lq, axis=0)

    @pl.when(step == 0)
    def _():
        s = lax.dot_general(qbd, kn_ref[0].astype(BF16), (((1,), (1,)), ((), ())),
                            preferred_element_type=F32)
        s = s - tile_rows(cn_ref[0])
        s = jnp.where(_iota2((rows, lq), 1) <= _iota2((rows, lq), 0) // FOX_HEADS, s, NEG_BIG)
        m = jnp.max(s, axis=-1, keepdims=True)
        p = jnp.exp(s - m)
        m_scr[...] = m
        l_scr[...] = jnp.sum(p, axis=-1, keepdims=True)
        acc_scr[...] = jnp.dot(p.astype(BF16), vn_ref[0].astype(BF16), preferred_element_type=F32)
        suf_scr[...] = jnp.zeros_like(suf_scr)

    page = k_refs[0].shape[3]
    suf = suf_scr[...]
    scores = []
    for i in range(pps):
        r2 = r_refs[i][0]
        s = jnp.dot(qbd, k_refs[i][0, 0].astype(BF16), preferred_element_type=F32)
        scores.append(s + tile_rows(r2[:, :page] + suf))
        suf = suf + r2[:, page:]
    suf_scr[...] = suf
    m = m_scr[...]
    m_new = m
    for s in scores:
        m_new = jnp.maximum(m_new, jnp.max(s, axis=-1, keepdims=True))
    alpha = jnp.exp(m - m_new)
    l = alpha * l_scr[...]
    acc = alpha * acc_scr[...]
    for i in range(pps):
        p = jnp.exp(scores[i] - m_new)
        l = l + jnp.sum(p, axis=-1, keepdims=True)
        acc = acc + lax.dot_general(p.astype(BF16), v_refs[i][0, 0].astype(BF16), (((1,), (1,)), ((), ())),
                                    preferred_element_type=F32)
    m_scr[...] = m_new
    l_scr[...] = l
    acc_scr[...] = acc

    @pl.when(step == nstep - 1)
    def _():
        own = _iota2((rows, FOX_WIDTH), 1) // FOX_HEAD_DIM == _iota2((rows, FOX_WIDTH), 0) % FOX_HEADS
        o = jnp.where(own, acc / l, 0.0)
        ms = jnp.sum(o * o, axis=-1, keepdims=True) * (1.0 / FOX_HEAD_DIM)
        o = o * lax.rsqrt(ms + NORM_EPS) * on_ref[...]
        o_ref[0] = jnp.sum(o.reshape(lq, FOX_HEADS, FOX_WIDTH), axis=1)


def _fox_sample(page_table, layer, qbd, kn3, vn3, cn3, onorm, ckt, cvt, r3, pps):
    b, lq, _ = kn3.shape
    n_pages = page_table.shape[1]
    page = ckt.shape[3]
    nstep = n_pages // pps
    rows = FOX_HEADS * lq

    def page_map(i):
        return lambda bi, s, pt: (layer, pt[bi, n_pages - 1 - (s * pps + i)], 0, 0)

    def r_map(i):
        return lambda bi, s, pt: (pt[bi, n_pages - 1 - (s * pps + i)], 0, 0)

    in_specs = [pl.BlockSpec((1, rows, FOX_WIDTH), lambda bi, s, pt: (bi, 0, 0)),
                pl.BlockSpec((1, lq, FOX_WIDTH), lambda bi, s, pt: (bi, 0, 0)),
                pl.BlockSpec((1, lq, FOX_WIDTH), lambda bi, s, pt: (bi, 0, 0)),
                pl.BlockSpec((1, FOX_HEADS, lq), lambda bi, s, pt: (bi, 0, 0)),
                pl.BlockSpec(onorm.shape, lambda bi, s, pt: (0, 0))]
    in_specs += [pl.BlockSpec((1, 1, FOX_WIDTH, page), page_map(i)) for i in range(pps)]
    in_specs += [pl.BlockSpec((1, 1, FOX_WIDTH, page), page_map(i)) for i in range(pps)]
    in_specs += [pl.BlockSpec((1, FOX_HEADS, 2 * page), r_map(i)) for i in range(pps)]
    grid_spec = pltpu.PrefetchScalarGridSpec(
        num_scalar_prefetch=1,
        grid=(b, nstep),
        in_specs=in_specs,
        out_specs=pl.BlockSpec((1, lq, FOX_WIDTH), lambda bi, s, pt: (bi, 0, 0)),
        scratch_shapes=[pltpu.VMEM((rows, 1), F32), pltpu.VMEM((rows, 1), F32),
                        pltpu.VMEM((rows, FOX_WIDTH), F32), pltpu.VMEM((FOX_HEADS, page), F32)],
    )
    return pl.pallas_call(
        functools.partial(_fox_sample_kernel, pps=pps, lq=lq),
        grid_spec=grid_spec,
        out_shape=jax.ShapeDtypeStruct((b, lq, FOX_WIDTH), F32),
        compiler_params=_cparams(("parallel", "arbitrary")),
    )(page_table, qbd, kn3, vn3, cn3, onorm, *([ckt] * pps), *([cvt] * pps), *([r3] * pps))


def _out_proj_kernel(x_ref, a_ref, b_ref, w_ref, o_ref):
    o_ref[...] = (x_ref[...]
                  + jnp.dot(a_ref[...].astype(BF16), w_ref[0:GDN_WIDTH, :], preferred_element_type=F32)
                  + jnp.dot(b_ref[...].astype(BF16), w_ref[GDN_WIDTH:, :], preferred_element_type=F32))


def _out_proj(x2, a2, b2, w, tm):
    t = x2.shape[0]
    return pl.pallas_call(
        _out_proj_kernel,
        grid=(t // tm,),
        in_specs=[pl.BlockSpec((tm, D_MODEL), lambda i: (i, 0)),
                  pl.BlockSpec((tm, GDN_WIDTH), lambda i: (i, 0)),
                  pl.BlockSpec((tm, FOX_WIDTH), lambda i: (i, 0)),
                  pl.BlockSpec(w.shape, lambda i: (0, 0))],
        out_specs=pl.BlockSpec((tm, D_MODEL), lambda i: (i, 0)),
        out_shape=jax.ShapeDtypeStruct((t, D_MODEL), F32),
        compiler_params=_cparams(("parallel",)),
    )(x2, a2, b2, w)


def _xattn_kernel(x_ref, g_ref, wq_ref, wo_ref, mk_ref, mv_ref, o_ref):
    x = x_ref[0]
    hb = _rms(x, g_ref[...]).astype(BF16)
    q = jnp.dot(hb, wq_ref[...], preferred_element_type=F32) * (XA_HEAD_DIM ** -0.5)
    mk = mk_ref[0].astype(BF16)
    mv = mv_ref[0].astype(BF16)
    outs = []
    for h in range(XA_HEADS):
        lo = h * XA_HEAD_DIM
        s = lax.dot_general(q[:, lo:lo + XA_HEAD_DIM].astype(BF16), mk[:, lo:lo + XA_HEAD_DIM],
                            (((1,), (1,)), ((), ())), preferred_element_type=F32)
        p = jnp.exp(s - jnp.max(s, axis=-1, keepdims=True))
        p = p / jnp.sum(p, axis=-1, keepdims=True)
        outs.append(jnp.dot(p.astype(BF16), mv[:, lo:lo + XA_HEAD_DIM], preferred_element_type=F32))
    o = jnp.concatenate(outs, axis=-1).astype(BF16)
    o_ref[0] = x + jnp.dot(o, wo_ref[...], preferred_element_type=F32)


def _xattn(x3, g, wq, wo, mk3, mv3, tq):
    b, seq, _ = x3.shape
    n_mem = mk3.shape[1]
    return pl.pallas_call(
        _xattn_kernel,
        grid=(b, seq // tq),
        in_specs=[pl.BlockSpec((1, tq, D_MODEL), lambda i, j: (i, j, 0)),
                  pl.BlockSpec(g.shape, lambda i, j: (0, 0)),
                  pl.BlockSpec(wq.shape, lambda i, j: (0, 0)),
                  pl.BlockSpec(wo.shape, lambda i, j: (0, 0)),
                  pl.BlockSpec((1, n_mem, XA_WIDTH), lambda i, j: (i, 0, 0)),
                  pl.BlockSpec((1, n_mem, XA_WIDTH), lambda i, j: (i, 0, 0))],
        out_specs=pl.BlockSpec((1, tq, D_MODEL), lambda i, j: (i, j, 0)),
        out_shape=jax.ShapeDtypeStruct((b, seq, D_MODEL), F32),
        compiler_params=_cparams(("parallel", "parallel")),
    )(x3, g, wq, wo, mk3, mv3)


def _router_kernel(x_ref, g_ref, wr_ref, br_ref, h_ref, route_ref, gate_ref, cnt_ref, base_scr, *, tm):
    i = pl.program_id(0)

    @pl.when(i == 0)
    def _():
        base_scr[...] = jnp.zeros_like(base_scr)

    h = _rms(x_ref[...], g_ref[...])
    _store_row_tiles(h_ref, h, tm)
    logits = _dot_hi(h, wr_ref[...]) + br_ref[...]
    lane = _iota2((tm, LANES), 1)
    lane_f = lane.astype(F32)
    vals, hots, idxs = [], [], []
    cur = logits
    for _ in range(TOP_K):
        mx = jnp.max(cur, axis=-1, keepdims=True)
        idx_f = jnp.min(jnp.where(cur == mx, lane_f, float(LANES)), axis=-1, keepdims=True)
        hot = lane_f == idx_f
        vals.append(mx)
        idxs.append(idx_f.astype(I32))
        hots.append(hot)
        cur = jnp.where(hot, -jnp.inf, cur)
    exps = [jnp.exp(v - vals[0]) for v in vals]
    denom = exps[0] + exps[1] + exps[2] + exps[3]
    member = (hots[0] | hots[1] | hots[2] | hots[3]).astype(F32)
    strict = (_iota2((tm, tm), 1) < _iota2((tm, tm), 0)).astype(BF16)
    before = jnp.dot(strict, member.astype(BF16), preferred_element_type=F32) + base_scr[...]
    route = jnp.zeros((tm, LANES), I32)
    gate = jnp.zeros((tm, LANES), F32)
    for k in range(TOP_K):
        rank = jnp.sum(jnp.where(hots[k], before, 0.0), axis=-1, keepdims=True).astype(I32)
        route = jnp.where(lane == k, idxs[k], route)
        route = jnp.where(lane == TOP_K + k, rank, route)
        gate = jnp.where(lane == k, exps[k] / denom, gate)
    route_ref[...] = route
    gate_ref[...] = gate
    base_scr[...] = base_scr[...] + jnp.sum(member, axis=0, keepdims=True)
    cnt_ref[...] = base_scr[...]


def _router(x2, g, wr, br, tm):
    t = x2.shape[0]
    return pl.pallas_call(
        functools.partial(_router_kernel, tm=tm),
        grid=(t // tm,),
        in_specs=[pl.BlockSpec((tm, D_MODEL), lambda i: (i, 0)),
                  pl.BlockSpec(g.shape, lambda i: (0, 0)),
                  pl.BlockSpec(wr.shape, lambda i: (0, 0)),
                  pl.BlockSpec(br.shape, lambda i: (0, 0))],
        out_specs=[pl.BlockSpec((tm * ROW_TILE, LANES), lambda i: (i, 0)),
                   pl.BlockSpec((tm, LANES), lambda i: (i, 0)),
                   pl.BlockSpec((tm, LANES), lambda i: (i, 0)),
                   pl.BlockSpec((1, LANES), lambda i: (0, 0))],
        out_shape=[jax.ShapeDtypeStruct((t * ROW_TILE, LANES), F32), jax.ShapeDtypeStruct((t, LANES), I32),
                   jax.ShapeDtypeStruct((t, LANES), F32), jax.ShapeDtypeStruct((1, LANES), F32)],
        scratch_shapes=[pltpu.VMEM((1, LANES), F32)],
        compiler_params=_cparams(("arbitrary",)),
    )(x2, g, wr, br)


def _dispatch_kernel(dest_ref, h_ref, init_hbm, xs_hbm, sem, *, tm):
    del init_hbm
    npair = tm * TOP_K

    def issue(p, c):
        src = pl.multiple_of((p // TOP_K) * ROW_TILE, ROW_TILE)
        dst = pl.multiple_of(dest_ref[0, 0, p] * ROW_TILE, ROW_TILE)
        pltpu.make_async_copy(h_ref.at[pl.ds(src, ROW_TILE)], xs_hbm.at[pl.ds(dst, ROW_TILE)], sem).start()
        return c

    lax.fori_loop(0, npair, issue, 0, unroll=8)
    for _ in range(TOP_K):
        pltpu.make_async_copy(h_ref, xs_hbm.at[pl.ds(0, tm * ROW_TILE)], sem).wait()


def _dispatch(dest3, h2, rows, tm):
    t = h2.shape[0] // ROW_TILE
    init = jnp.zeros((rows * ROW_TILE, LANES), F32)
    return pl.pallas_call(
        functools.partial(_dispatch_kernel, tm=tm),
        grid=(t // tm,),
        in_specs=[pl.BlockSpec((1, 1, tm * TOP_K), lambda i: (i, 0, 0), memory_space=pltpu.SMEM),
                  pl.BlockSpec((tm * ROW_TILE, LANES), lambda i: (i, 0)),
                  pl.BlockSpec(memory_space=pl.ANY)],
        out_specs=pl.BlockSpec(memory_space=pl.ANY),
        out_shape=jax.ShapeDtypeStruct((rows * ROW_TILE, LANES), F32),
        scratch_shapes=[pltpu.SemaphoreType.DMA(())],
        input_output_aliases={2: 0},
        compiler_params=pltpu.CompilerParams(dimension_semantics=("arbitrary",), vmem_limit_bytes=VMEM_LIMIT,
                                             has_side_effects=True),
    )(dest3, h2, init)


def _ffn_kernel(be_ref, nv_ref, x_ref, w1_ref, b1_ref, w2_ref, b2_ref, o_ref):
    i = pl.program_id(0)
    live = i * EXPERT_BLOCK < nv_ref[0]

    @pl.when(live)
    def _():
        x = _load_row_tiles(x_ref, EXPERT_BLOCK).astype(BF16)
        hb = jnp.dot(x, w1_ref[0], preferred_element_type=F32) + b1_ref[0]
        glu = jnp.minimum(hb[:, :D_FF], SWIGLU_LIMIT)
        lin = jnp.clip(hb[:, D_FF:], -SWIGLU_LIMIT, SWIGLU_LIMIT)
        act = glu * _sigmoid(SWIGLU_ALPHA * glu) * (lin + 1.0)
        y = jnp.dot(act.astype(BF16), w2_ref[0], preferred_element_type=F32) + b2_ref[0]
        _store_row_tiles(o_ref, y, EXPERT_BLOCK)

    @pl.when(jnp.logical_not(live))
    def _():
        o_ref[...] = jnp.zeros_like(o_ref)


def _ffn(block_e, nvalid, xs, w1, b1, w2, b2):
    rows = xs.shape[0] // ROW_TILE
    nb = rows // EXPERT_BLOCK
    blk = pl.BlockSpec((EXPERT_BLOCK * ROW_TILE, LANES), lambda i, be, nv: (i, 0))
    grid_spec = pltpu.PrefetchScalarGridSpec(
        num_scalar_prefetch=2,
        grid=(nb,),
        in_specs=[blk,
                  pl.BlockSpec((1, D_MODEL, 2 * D_FF), lambda i, be, nv: (be[i], 0, 0)),
                  pl.BlockSpec((1, 1, 2 * D_FF), lambda i, be, nv: (be[i], 0, 0)),
                  pl.BlockSpec((1, D_FF, D_MODEL), lambda i, be, nv: (be[i], 0, 0)),
                  pl.BlockSpec((1, 1, D_MODEL), lambda i, be, nv: (be[i], 0, 0))],
        out_specs=blk,
    )
    return pl.pallas_call(
        _ffn_kernel,
        grid_spec=grid_spec,
        out_shape=jax.ShapeDtypeStruct((rows * ROW_TILE, LANES), F32),
        compiler_params=_cparams(("arbitrary",)),
    )(block_e, nvalid, xs, w1, b1, w2, b2)


def _combine_kernel(dcur_ref, dnext_ref, x_ref, gate_ref, fg_ref, os_hbm, y_ref, yn_ref, buf, sem, *, tm):
    i = pl.program_id(0)
    n = pl.num_programs(0)
    npair = tm * TOP_K

    def issue_tile(dref, slot):
        def issue(p, c):
            src = pl.multiple_of(dref[0, 0, p] * ROW_TILE, ROW_TILE)
            dst = pl.multiple_of((p // TOP_K) * ROW_TILE, ROW_TILE)
            pltpu.make_async_copy(os_hbm.at[pl.ds(src, ROW_TILE)],
                                  buf.at[slot, p % TOP_K, pl.ds(dst, ROW_TILE)], sem.at[slot]).start()
            return c

        lax.fori_loop(0, npair, issue, 0, unroll=8)

    @pl.when(i == 0)
    def _():
        issue_tile(dcur_ref, 0)

    @pl.when(i + 1 < n)
    def _():
        issue_tile(dnext_ref, (i + 1) % 2)

    slot = i % 2
    for k in range(TOP_K):
        pltpu.make_async_copy(os_hbm.at[pl.ds(0, tm * ROW_TILE)], buf.at[slot, k], sem.at[slot]).wait()
    gate = gate_ref[...]
    x = x_ref[...]
    pieces = []
    for s in range(ROW_TILE):
        acc = x[:, s * LANES:(s + 1) * LANES]
        for k in range(TOP_K):
            acc = acc + gate[:, k:k + 1] * buf[slot, k, pl.ds(s, tm, stride=ROW_TILE), :]
        pieces.append(acc)
    y = jnp.concatenate(pieces, axis=-1)
    y_ref[...] = y
    yn_ref[...] = _rms(y, fg_ref[...])


def _combine(dest3, x2, gate, fg, os2, tm):
    t = x2.shape[0]
    nt = t // tm
    return pl.pallas_call(
        functools.partial(_combine_kernel, tm=tm),
        grid=(nt,),
        in_specs=[pl.BlockSpec((1, 1, tm * TOP_K), lambda i: (i, 0, 0), memory_space=pltpu.SMEM),
                  pl.BlockSpec((1, 1, tm * TOP_K), lambda i: (jnp.minimum(i + 1, nt - 1), 0, 0),
                               memory_space=pltpu.SMEM),
                  pl.BlockSpec((tm, D_MODEL), lambda i: (i, 0)),
                  pl.BlockSpec((tm, LANES), lambda i: (i, 0)),
                  pl.BlockSpec(fg.shape, lambda i: (0, 0)),
                  pl.BlockSpec(memory_space=pl.ANY)],
        out_specs=[pl.BlockSpec((tm, D_MODEL), lambda i: (i, 0)),
                   pl.BlockSpec((tm, D_MODEL), lambda i: (i, 0))],
        out_shape=[jax.ShapeDtypeStruct((t, D_MODEL), F32), jax.ShapeDtypeStruct((t, D_MODEL), F32)],
        scratch_shapes=[pltpu.VMEM((2, TOP_K, tm * ROW_TILE, LANES), F32), pltpu.SemaphoreType.DMA((2,))],
        compiler_params=_cparams(("arbitrary",)),
    )(dest3, dest3, x2, gate, fg, os2)


def _row_tile(t, want):
    tm = min(want, t)
    assert t % tm == 0
    return tm


def _mixer(x3, lw, conv_state8, s0, fox_fn):
    b, seq, _ = x3.shape
    t = b * seq
    x2 = x3.reshape(t, D_MODEL)
    qkv, z, fq, fk, fv, sm, smt = _in_proj(x2, lw["norm1_g"], lw["w_main"], lw["w_small"], lw["w_small_t"],
                                           _row_tile(t, 256))
    sm3 = sm.reshape(b, seq, LANES)
    smt3 = jnp.transpose(smt.reshape(16, b, seq), (1, 0, 2))
    logf, ct = _logf(sm3, smt3, lw["fb_row"], lw["fb_col"])
    c = min(GDN_CHUNK, seq)
    tl = min(256, seq)
    gdn_out, s_new, cbuf = _gdn(qkv.reshape(b, seq, CONV_CH), z.reshape(b, seq, GDN_WIDTH), sm3, smt3,
                                conv_state8, s0, lw["conv_w"], lw["gp_row"], lw["gp_col"], tl, c,
                                exact_small=(c <= SUBLANES))
    fq3 = fq.reshape(b, seq, FOX_WIDTH)
    fk3 = fk.reshape(b, seq, FOX_WIDTH)
    fv3 = fv.reshape(b, seq, FOX_WIDTH)
    fox_out = fox_fn(fq3, fk3, fv3, ct)
    y2 = _out_proj(x2, gdn_out.reshape(t, GDN_WIDTH), fox_out.reshape(t, FOX_WIDTH), lw["w_out"],
                   _row_tile(t, 512))
    return (y2.reshape(b, seq, D_MODEL), s_new, cbuf,
            fk3.reshape(b, seq, FOX_HEADS, FOX_HEAD_DIM), fv3.reshape(b, seq, FOX_HEADS, FOX_HEAD_DIM), logf)


def _moe(x3, lw, final_g):
    b, seq, _ = x3.shape
    t = b * seq
    x2 = x3.reshape(t, D_MODEL)
    h2, route, gate, counts = _router(x2, lw["norm3_g"], lw["w_router"], lw["b_router"], _row_tile(t, 256))
    cnt = counts[0, :N_EXPERTS].astype(I32)
    padded = (cnt + EXPERT_BLOCK - 1) // EXPERT_BLOCK * EXPERT_BLOCK
    pends = jnp.cumsum(padded)
    pstart = pends - padded
    nb = t * TOP_K // EXPERT_BLOCK + N_EXPERTS
    rows = nb * EXPERT_BLOCK
    block_row0 = jnp.arange(nb, dtype=I32) * EXPERT_BLOCK
    block_e = jnp.minimum(jnp.sum((pends[None, :] <= block_row0[:, None]).astype(I32), axis=1), N_EXPERTS - 1)
    nvalid = pends[-1:].astype(I32)
    hot = route[:, :TOP_K, None] == jnp.arange(N_EXPERTS, dtype=I32)[None, None, :]
    dest = jnp.sum(jnp.where(hot, pstart[None, None, :], 0), axis=-1) + route[:, TOP_K:2 * TOP_K]
    tmd = _row_tile(t, 256)
    xs = _dispatch(dest.reshape(t // tmd, 1, tmd * TOP_K), h2, rows, tmd)
    os2 = _ffn(block_e, nvalid, xs, lw["w1"], lw["b1"], lw["w2"], lw["b2"])
    tmc = _row_tile(t, 256)
    y2, yn2 = _combine(dest.reshape(t // tmc, 1, tmc * TOP_K), x2, gate, final_g, os2, tmc)
    return y2.reshape(b, seq, D_MODEL), yn2.reshape(b, seq, D_MODEL)


def _prep_layer(l, norm1_g, w_in, conv_w, gdn_a_log, gdn_dt_bias, gdn_onorm, fox_fbias, fox_onorm, w_out,
                norm2_g, mem_norm_g, w_xq, w_mkv, w_xo, norm3_g, w_router, b_router, w1, b1, w2, b2):
    wi = w_in[l]
    gq, gk, gv, gz, gb, ga, fq, fk, fv, ff = _split_in(wi)
    w_main = jnp.concatenate([gq, gk, gv, gz, fq, fk, fv], axis=1).astype(BF16)
    w_small = jnp.concatenate([gb, ga, ff, jnp.zeros((D_MODEL, LANES - 16), F32)], axis=1).astype(BF16)
    w_small_t = jnp.transpose(w_small[:, :16])
    lanes = lambda v, off: jnp.zeros((LANES,), F32).at[off:off + v.shape[0]].set(v)
    gp_row = jnp.zeros((SUBLANES, LANES), F32)
    gp_row = gp_row.at[0].set(lanes(gdn_a_log[l], GDN_HEADS)).at[1].set(lanes(gdn_dt_bias[l], GDN_HEADS))
    gp_row = gp_row.at[2].set(gdn_onorm[l])
    gp_col = jnp.zeros((16, LANES), F32)
    gp_col = gp_col.at[GDN_HEADS:2 * GDN_HEADS, 0].set(gdn_a_log[l]).at[GDN_HEADS:2 * GDN_HEADS, 1].set(gdn_dt_bias[l])
    fb_row = jnp.zeros((1, LANES), F32).at[0, 8:16].set(fox_fbias[l])
    fb_col = jnp.zeros((16, LANES), F32).at[8:16, 0].set(fox_fbias[l])
    wr = jnp.concatenate([w_router[l], jnp.zeros((D_MODEL, LANES - N_EXPERTS), F32)], axis=1)
    br = jnp.full((1, LANES), NEG_BIG, F32).at[0, :N_EXPERTS].set(b_router[l])
    return {
        "norm1_g": norm1_g[l].reshape(1, D_MODEL), "w_main": w_main, "w_small": w_small, "w_small_t": w_small_t,
        "conv_w": conv_w[l], "gp_row": gp_row, "gp_col": gp_col, "fb_row": fb_row, "fb_col": fb_col,
        "fox_onorm": fox_onorm[l].reshape(1, FOX_HEAD_DIM), "w_out": w_out[l].astype(BF16),
        "norm2_g": norm2_g[l].reshape(1, D_MODEL), "mem_norm_g": mem_norm_g[l].reshape(1, D_MODEL),
        "w_xq": w_xq[l].astype(BF16), "w_mkv": w_mkv[l].astype(BF16), "w_xo": w_xo[l].astype(BF16),
        "norm3_g": norm3_g[l].reshape(1, D_MODEL), "w_router": wr, "b_router": br,
        "w1": w1[l].astype(BF16), "b1": b1[l].reshape(N_EXPERTS, 1, 2 * D_FF),
        "w2": w2[l].astype(BF16), "b2": b2[l].reshape(N_EXPERTS, 1, D_MODEL),
    }


def _split_in(wi):
    widths = (GDN_WIDTH, GDN_WIDTH, GDN_WIDTH, GDN_WIDTH, GDN_HEADS, GDN_HEADS,
              FOX_WIDTH, FOX_WIDTH, FOX_WIDTH, FOX_HEADS)
    outs, c0 = [], 0
    for w in widths:
        outs.append(wi[:, c0:c0 + w])
        c0 += w
    return outs


def kernel(x_prompt, x_sample, mem_prompt, cache_fox_k, cache_fox_v, cache_fox_logf, cache_mem_k, cache_mem_v, state_gdn, state_conv, page_table, norm1_g, w_in, conv_w, gdn_a_log, gdn_dt_bias, gdn_onorm, fox_fbias, fox_onorm, w_out, norm2_g, mem_norm_g, w_xq, w_mkv, w_xo, norm3_g, w_router, b_router, w1, b1, w2, b2, final_norm_g):
    depth = w_in.shape[0]
    bp, lp, _ = x_prompt.shape
    bs, ls, _ = x_sample.shape
    n_mem = mem_prompt.shape[1]
    n_pool, page = cache_fox_k.shape[1], cache_fox_k.shape[2]
    ckt = jnp.transpose(cache_fox_k, (0, 1, 3, 4, 2)).reshape(depth, n_pool, FOX_WIDTH, page)
    cvt = jnp.transpose(cache_fox_v, (0, 1, 3, 4, 2)).reshape(depth, n_pool, FOX_WIDTH, page)
    lf_rows = jnp.transpose(cache_fox_logf, (0, 1, 3, 2)).reshape(depth, n_pool * FOX_HEADS, page)
    tok = jnp.arange(page, dtype=I32)
    later = tok[:, None] > tok[None, :]
    w_suffix = jnp.concatenate([later, jnp.ones_like(later)], axis=1).astype(BF16)
    head_mask = (jnp.arange(FOX_HEADS)[:, None] == jnp.arange(FOX_HEADS)[None, :]).astype(F32)
    final_g = final_norm_g.reshape(1, D_MODEL)
    xp, xs = x_prompt, x_sample
    yp = ys = None
    outs = {k: [] for k in ("fkp", "fvp", "flp", "mkp", "mvp", "sgp", "scp", "fks", "fvs", "fls", "sgs", "scs")}
    for l in range(depth):
        lw = _prep_layer(l, norm1_g, w_in, conv_w, gdn_a_log, gdn_dt_bias, gdn_onorm, fox_fbias, fox_onorm,
                         w_out, norm2_g, mem_norm_g, w_xq, w_mkv, w_xo, norm3_g, w_router, b_router,
                         w1, b1, w2, b2)
        tq = min(256, lp)

        def fox_p(fq3, fk3, fv3, ct, lw=lw, tq=tq):
            return _fox_prompt(fq3, fk3, fv3, ct.reshape(bp, FOX_HEADS, lp // tq, tq), lw["fox_onorm"], tq)

        xp, s_new, cbuf, k_new, v_new, lf_new = _mixer(
            xp, lw, jnp.zeros((bp, SUBLANES, CONV_CH), F32),
            jnp.zeros((bp, GDN_HEADS, GDN_DK, GDN_DV), F32), fox_p)
        outs["fkp"].append(k_new); outs["fvp"].append(v_new); outs["flp"].append(lf_new)
        outs["sgp"].append(s_new); outs["scp"].append(cbuf)

        r3 = _logf_pages(lf_rows[l], w_suffix).reshape(n_pool, FOX_HEADS, 2 * page)
        on_tiled = jnp.tile(lw["fox_onorm"], (1, FOX_HEADS))

        def fox_s(fq3, fk3, fv3, ct, lw=lw, l=l, r3=r3, on_tiled=on_tiled):
            q4 = fq3.reshape(bs, ls, FOX_HEADS, FOX_HEAD_DIM) * (FOX_HEAD_DIM ** -0.5)
            qbd = jnp.einsum("bqhd,hg->bqhgd", q4, head_mask).reshape(bs, ls * FOX_HEADS, FOX_WIDTH)
            return _fox_sample(page_table, l, qbd.astype(BF16), fk3, fv3, ct, on_tiled, ckt, cvt, r3, pps=8)

        cst8 = jnp.pad(state_conv[l], ((0, 0), (SUBLANES - (CONV_W - 1), 0), (0, 0)))
        xs, s_new, cbuf, k_new, v_new, lf_new = _mixer(xs, lw, cst8, state_gdn[l], fox_s)
        outs["fks"].append(k_new); outs["fvs"].append(v_new); outs["fls"].append(lf_new)
        outs["sgs"].append(s_new); outs["scs"].append(cbuf)

        mk2, mv2 = _norm_proj(mem_prompt.reshape(bp * n_mem, D_MODEL), lw["mem_norm_g"], lw["w_mkv"],
                              (XA_WIDTH, XA_WIDTH), _row_tile(bp * n_mem, 512))
        outs["mkp"].append(mk2.reshape(bp, n_mem, XA_HEADS, XA_HEAD_DIM))
        outs["mvp"].append(mv2.reshape(bp, n_mem, XA_HEADS, XA_HEAD_DIM))
        xp = _xattn(xp, lw["norm2_g"], lw["w_xq"], lw["w_xo"], mk2.reshape(bp, n_mem, XA_WIDTH),
                    mv2.reshape(bp, n_mem, XA_WIDTH), min(512, lp))
        xs = _xattn(xs, lw["norm2_g"], lw["w_xq"], lw["w_xo"], cache_mem_k[l].reshape(bs, n_mem, XA_WIDTH),
                    cache_mem_v[l].reshape(bs, n_mem, XA_WIDTH), ls)

        xp, yp = _moe(xp, lw, final_g)
        xs, ys = _moe(xs, lw, final_g)
    st = jnp.stack
    return (yp, ys, st(outs["fkp"]), st(outs["fvp"]), st(outs["flp"]), st(outs["mkp"]), st(outs["mvp"]),
            st(outs["sgp"]), st(outs["scp"]), st(outs["fks"]), st(outs["fvs"]), st(outs["fls"]),
            st(outs["sgs"]), st(outs["scs"]))
```

```python
import functools
import math

import jax
import jax.numpy as jnp
from jax import lax
from jax.experimental import pallas as pl
from jax.experimental.pallas import tpu as pltpu

F32 = jnp.float32
BF16 = jnp.bfloat16
I32 = jnp.int32
HIGHEST = lax.Precision.HIGHEST

D_MODEL = 1024
GDN_HEADS = 4
GDN_DK = 128
GDN_DV = 128
GDN_WIDTH = GDN_HEADS * GDN_DV
CONV_W = 4
CONV_CH = 3 * GDN_WIDTH
GDN_CHUNK = 64
FOX_HEADS = 8
FOX_HEAD_DIM = 64
FOX_WIDTH = FOX_HEADS * FOX_HEAD_DIM
XA_HEADS = 4
XA_HEAD_DIM = 128
XA_WIDTH = XA_HEADS * XA_HEAD_DIM
N_EXPERTS = 32
TOP_K = 4
D_FF = D_MODEL
SWIGLU_LIMIT = 7.0
SWIGLU_ALPHA = 1.702
NORM_EPS = 1e-6
NEG_BIG = -1e30

LANES = 128
SUBLANES = 8
VMEM_LIMIT = 52 * 1024 * 1024
MAIN_COLS = 2 * GDN_WIDTH + 2 * GDN_WIDTH + 3 * FOX_WIDTH
EXPERT_BLOCK = 256


def _cparams(sem):
    return pltpu.CompilerParams(dimension_semantics=sem, vmem_limit_bytes=VMEM_LIMIT)


def _dot(a, b):
    return jnp.dot(a.astype(BF16), b.astype(BF16), preferred_element_type=F32)


def _dot_nt(a, b):
    return lax.dot_general(a.astype(BF16), b.astype(BF16), (((1,), (1,)), ((), ())),
                           preferred_element_type=F32)


def _dot_tn(a, b):
    return lax.dot_general(a.astype(BF16), b.astype(BF16), (((0,), (0,)), ((), ())),
                           preferred_element_type=F32)


def _dot_hi(a, b):
    return jnp.dot(a, b, precision=HIGHEST, preferred_element_type=F32)


def _dot_nt_hi(a, b):
    return lax.dot_general(a, b, (((1,), (1,)), ((), ())), precision=HIGHEST,
                           preferred_element_type=F32)


def _rms(x, g):
    return x * lax.rsqrt(jnp.mean(x * x, axis=-1, keepdims=True) + NORM_EPS) * g


def _sigmoid(x):
    return 1.0 / (1.0 + jnp.exp(-x))


def _softplus(x):
    return jnp.maximum(x, 0.0) + jnp.log1p(jnp.exp(-jnp.abs(x)))


def _log_sigmoid(x):
    return jnp.minimum(x, 0.0) - jnp.log1p(jnp.exp(-jnp.abs(x)))


def _iota2(shape, dim):
    return lax.broadcasted_iota(I32, shape, dim)


ROW_TILE = D_MODEL // LANES


def _store_row_tiles(ref, val, n):
    for s in range(ROW_TILE):
        ref[pl.ds(s, n, stride=ROW_TILE), :] = val[:, s * LANES:(s + 1) * LANES]


def _load_row_tiles(ref, n):
    return jnp.concatenate([ref[pl.ds(s, n, stride=ROW_TILE), :] for s in range(ROW_TILE)], axis=-1)


def _in_proj_kernel(x_ref, g_ref, wm_ref, wkvt_ref, ws_ref, wst_ref,
                    qkv_ref, z_ref, fq_ref, fk_ref, fv_ref, sm_ref, smt_ref, *, kv_t):
    hb = _rms(x_ref[...], g_ref[...]).astype(BF16)
    nt = lambda w: lax.dot_general(w, hb, (((1,), (1,)), ((), ())), preferred_element_type=F32)
    c0 = 0
    for ref, width in ((qkv_ref, CONV_CH), (z_ref, GDN_WIDTH), (fq_ref, FOX_WIDTH)):
        ref[...] = jnp.dot(hb, wm_ref[:, c0:c0 + width], preferred_element_type=F32)
        c0 += width
    if kv_t:
        fk_ref[0] = nt(wkvt_ref[0:FOX_WIDTH, :])
        fv_ref[0] = nt(wkvt_ref[FOX_WIDTH:, :])
    else:
        fk_ref[...] = jnp.dot(hb, wm_ref[:, c0:c0 + FOX_WIDTH], preferred_element_type=F32)
        fv_ref[...] = jnp.dot(hb, wm_ref[:, c0 + FOX_WIDTH:], preferred_element_type=F32)
    sm_ref[...] = jnp.dot(hb, ws_ref[...], preferred_element_type=F32)
    smt_ref[...] = nt(wst_ref[...])


def _in_proj(x2, g, wm, wkvt, ws, wst, tm, kv_t_seq):
    t = x2.shape[0]
    row = lambda w: pl.BlockSpec((tm, w), lambda i: (i, 0))
    full = lambda a: pl.BlockSpec(a.shape, lambda i: (0,) * a.ndim)
    if kv_t_seq:
        per_b = kv_t_seq // tm
        kv_spec = pl.BlockSpec((1, FOX_WIDTH, tm), lambda i: (i // per_b, 0, i % per_b))
        kv_shape = jax.ShapeDtypeStruct((t // kv_t_seq, FOX_WIDTH, kv_t_seq), F32)
    else:
        kv_spec = row(FOX_WIDTH)
        kv_shape = jax.ShapeDtypeStruct((t, FOX_WIDTH), F32)
    return pl.pallas_call(
        functools.partial(_in_proj_kernel, kv_t=bool(kv_t_seq)),
        grid=(t // tm,),
        in_specs=[row(D_MODEL), full(g), full(wm), full(wkvt), full(ws), full(wst)],
        out_specs=[row(CONV_CH), row(GDN_WIDTH), row(FOX_WIDTH), kv_spec, kv_spec,
                   row(LANES), pl.BlockSpec((16, tm), lambda i: (0, i))],
        out_shape=[jax.ShapeDtypeStruct((t, CONV_CH), F32), jax.ShapeDtypeStruct((t, GDN_WIDTH), F32),
                   jax.ShapeDtypeStruct((t, FOX_WIDTH), F32), kv_shape, kv_shape,
                   jax.ShapeDtypeStruct((t, LANES), F32), jax.ShapeDtypeStruct((16, t), F32)],
        compiler_params=_cparams(("parallel",)),
    )(x2, g, wm, wkvt, ws, wst)


def _norm_proj_kernel(x_ref, g_ref, w_ref, *out_refs):
    hb = _rms(x_ref[...], g_ref[...]).astype(BF16)
    c0 = 0
    for ref in out_refs:
        width = ref.shape[-1]
        ref[...] = jnp.dot(hb, w_ref[:, c0:c0 + width], preferred_element_type=F32)
        c0 += width


def _norm_proj(x2, g, w, widths, tm):
    t = x2.shape[0]
    return pl.pallas_call(
        _norm_proj_kernel,
        grid=(t // tm,),
        in_specs=[pl.BlockSpec((tm, D_MODEL), lambda i: (i, 0)),
                  pl.BlockSpec(g.shape, lambda i: (0, 0)),
                  pl.BlockSpec(w.shape, lambda i: (0, 0))],
        out_specs=[pl.BlockSpec((tm, wd), lambda i: (i, 0)) for wd in widths],
        out_shape=[jax.ShapeDtypeStruct((t, wd), F32) for wd in widths],
        compiler_params=_cparams(("parallel",)),
    )(x2, g, w)


def _small_mm(a, b):
    acc = a[:, 0:1] * b[0:1, :]
    for i in range(1, a.shape[1]):
        acc = acc + a[:, i:i + 1] * b[i:i + 1, :]
    return acc


def _logf_kernel(smt_ref, fbc_ref, logf_ref, ct_ref, *, seq, chunk):
    lft = _log_sigmoid(smt_ref[0][8:16, :] + fbc_ref[8:16, 0:1])
    logf_ref[0] = lft
    tri = (_iota2((chunk, chunk), 0) <= _iota2((chunk, chunk), 1)).astype(F32)
    carry = jnp.zeros((FOX_HEADS, 1), F32)
    for c in range(seq // chunk):
        blk = lft[:, c * chunk:(c + 1) * chunk]
        cs = (_small_mm(blk, tri) if chunk <= SUBLANES else _dot_hi(blk, tri)) + carry
        ct_ref[0, :, c * chunk:(c + 1) * chunk] = cs
        carry = cs[:, chunk - 1:chunk]


def _logf(smt3, fbc):
    b, _, seq = smt3.shape
    chunk = min(256, seq)
    return pl.pallas_call(
        functools.partial(_logf_kernel, seq=seq, chunk=chunk),
        grid=(b,),
        in_specs=[pl.BlockSpec((1, 16, seq), lambda i: (i, 0, 0)),
                  pl.BlockSpec(fbc.shape, lambda i: (0, 0))],
        out_specs=[pl.BlockSpec((1, FOX_HEADS, seq), lambda i: (i, 0, 0)),
                   pl.BlockSpec((1, FOX_HEADS, seq), lambda i: (i, 0, 0))],
        out_shape=[jax.ShapeDtypeStruct((b, FOX_HEADS, seq), F32),
                   jax.ShapeDtypeStruct((b, FOX_HEADS, seq), F32)],
        compiler_params=_cparams(("parallel",)),
    )(smt3, fbc)


def _unit_lower_inverse(a, c, mm):
    eye = (_iota2((c, c), 0) == _iota2((c, c), 1)).astype(F32)
    n = -a
    t = eye + n
    p = n
    levels = int(math.log2(c))
    for _ in range(levels - 1):
        p = mm(p, p)
        t = t + mm(t, p)
    return t


def _gdn_kernel(qkv_ref, z_ref, sm_ref, smt_ref, cst_ref, s0_ref, cw_ref, gpr_ref, gpc_ref,
                o_ref, sn_ref, cb_ref, s_scr, tail_scr, *, tl, c, exact_small):
    t = pl.program_id(1)
    nt = pl.num_programs(1)

    @pl.when(t == 0)
    def _():
        s_scr[...] = s0_ref[0]
        tail_scr[...] = cst_ref[0]

    rb = (lambda v: v.astype(BF16).astype(F32)) if exact_small else (lambda v: v.astype(BF16))
    bdims = ((0,), (0,))
    mm = lambda a, b: lax.dot_general(rb(a), rb(b), (((2,), (1,)), bdims), preferred_element_type=F32)
    mm_nt = lambda a, b: lax.dot_general(rb(a), rb(b), (((2,), (2,)), bdims), preferred_element_type=F32)
    mm_tn = lambda a, b: lax.dot_general(rb(a), rb(b), (((1,), (1,)), bdims), preferred_element_type=F32)

    x = qkv_ref[0]
    tail = tail_scr[...]
    cw = cw_ref[...]
    row8 = _iota2((SUBLANES, CONV_CH), 0)
    acc = x * cw[CONV_W - 1:CONV_W, :]
    for s in range(1, CONV_W):
        xs = pltpu.roll(x, s, 0)
        head = jnp.where(row8 < s, pltpu.roll(tail, s, 0), xs[0:SUBLANES])
        xs = head if tl == SUBLANES else jnp.concatenate([head, xs[SUBLANES:]], axis=0)
        acc = acc + xs * cw[CONV_W - 1 - s:CONV_W - s, :]
    conv = acc * _sigmoid(acc)
    tail_scr[...] = x[tl - SUBLANES:tl, :]

    @pl.when(t == nt - 1)
    def _():
        cb_ref[0] = x[tl - (CONV_W - 1):tl, :]

    sm = sm_ref[0]
    beta_c = _sigmoid(sm)
    g_c = -jnp.exp(gpr_ref[0:1, :]) * _softplus(sm + gpr_ref[1:2, :])
    smt = smt_ref[0]
    g_r = -jnp.exp(gpc_ref[:, 0:1]) * _softplus(smt + gpc_ref[:, 1:2])
    onorm = gpr_ref[2:3, :]

    ii = _iota2((c, c), 0)
    jj = _iota2((c, c), 1)
    tri_c = (jj <= ii).astype(F32)
    tri_r = (ii <= jj).astype(F32)
    small = c <= SUBLANES
    nc = tl // c
    cum_cols, cum_rows = [], []
    for ci in range(nc):
        gc_blk = g_c[ci * c:(ci + 1) * c, :]
        gr_blk = g_r[:, ci * c:(ci + 1) * c]
        cum_cols.append(_small_mm(tri_c, gc_blk) if small else _dot_hi(tri_c, gc_blk))
        cum_rows.append(_small_mm(gr_blk, tri_r) if small else _dot_hi(gr_blk, tri_r))

    def per_pair(fn):
        return jnp.stack([fn(ci, h) for ci in range(nc) for h in range(GDN_HEADS)], axis=0)

    rows = lambda ci: slice(ci * c, (ci + 1) * c)
    qs = per_pair(lambda ci, h: conv[rows(ci), h * GDN_DK:(h + 1) * GDN_DK])
    ks = per_pair(lambda ci, h: conv[rows(ci), GDN_WIDTH + h * GDN_DK:GDN_WIDTH + (h + 1) * GDN_DK])
    vs = per_pair(lambda ci, h: conv[rows(ci), 2 * GDN_WIDTH + h * GDN_DV:2 * GDN_WIDTH + (h + 1) * GDN_DV])
    beta = per_pair(lambda ci, h: beta_c[rows(ci), h:h + 1])
    cum_c = per_pair(lambda ci, h: cum_cols[ci][:, GDN_HEADS + h:GDN_HEADS + h + 1])
    cum_r = per_pair(lambda ci, h: cum_rows[ci][GDN_HEADS + h:GDN_HEADS + h + 1, :])
    qs = qs * lax.rsqrt(jnp.sum(qs * qs, axis=-1, keepdims=True) + NORM_EPS) * (GDN_DK ** -0.5)
    ks = ks * lax.rsqrt(jnp.sum(ks * ks, axis=-1, keepdims=True) + NORM_EPS)
    cum_last = cum_c[:, c - 1:c, :]
    dec = jnp.exp(jnp.where(ii >= jj, cum_c - cum_r, NEG_BIG))
    dec_strict = jnp.where(ii > jj, dec, 0.0)
    a_mat = beta * mm_nt(ks, ks) * dec_strict
    tinv = _unit_lower_inverse(a_mat, c, mm)
    e_cum = jnp.exp(cum_c)
    rhs = jnp.concatenate([beta * vs, (beta * e_cum) * ks], axis=-1)
    sol = mm(tinv, rhs)
    w_v, w_k = sol[:, :, :GDN_DV], sol[:, :, GDN_DV:]
    p_qk = mm_nt(qs, ks) * dec
    q_g = qs * e_cum
    k_d = ks * jnp.exp(cum_last - cum_c)
    g_end = jnp.exp(cum_last)

    state = s_scr[...]
    for ci in range(nc):
        pr = slice(ci * GDN_HEADS, (ci + 1) * GDN_HEADS)
        u = w_v[pr] - mm(w_k[pr], state)
        o = mm(q_g[pr], state) + mm(p_qk[pr], u)
        state = g_end[pr] * state + mm_tn(k_d[pr], u)
        for h in range(GDN_HEADS):
            lo = h * GDN_DV
            zh = z_ref[0, rows(ci), lo:lo + GDN_DV]
            o_ref[0, rows(ci), lo:lo + GDN_DV] = _rms(o[h], onorm) * (zh * _sigmoid(zh))
    s_scr[...] = state

    @pl.when(t == nt - 1)
    def _():
        sn_ref[0] = state


def _gdn(qkv3, z3, sm3, smt3, cst8, s0, cw, gpr, gpc, tl, c, exact_small):
    b, seq, _ = qkv3.shape
    nt = seq // tl
    full2 = lambda a: pl.BlockSpec(a.shape, lambda i, j: (0, 0))
    return pl.pallas_call(
        functools.partial(_gdn_kernel, tl=tl, c=c, exact_small=exact_small),
        grid=(b, nt),
        in_specs=[pl.BlockSpec((1, tl, CONV_CH), lambda i, j: (i, j, 0)),
                  pl.BlockSpec((1, tl, GDN_WIDTH), lambda i, j: (i, j, 0)),
                  pl.BlockSpec((1, tl, LANES), lambda i, j: (i, j, 0)),
                  pl.BlockSpec((1, 16, tl), lambda i, j: (i, 0, j)),
                  pl.BlockSpec((1, SUBLANES, CONV_CH), lambda i, j: (i, 0, 0)),
                  pl.BlockSpec((1, GDN_HEADS, GDN_DK, GDN_DV), lambda i, j: (i, 0, 0, 0)),
                  full2(cw), full2(gpr), full2(gpc)],
        out_specs=[pl.BlockSpec((1, tl, GDN_WIDTH), lambda i, j: (i, j, 0)),
                   pl.BlockSpec((1, GDN_HEADS, GDN_DK, GDN_DV), lambda i, j: (i, 0, 0, 0)),
                   pl.BlockSpec((1, CONV_W - 1, CONV_CH), lambda i, j: (i, 0, 0))],
        out_shape=[jax.ShapeDtypeStruct((b, seq, GDN_WIDTH), F32),
                   jax.ShapeDtypeStruct((b, GDN_HEADS, GDN_DK, GDN_DV), F32),
                   jax.ShapeDtypeStruct((b, CONV_W - 1, CONV_CH), F32)],
        scratch_shapes=[pltpu.VMEM((GDN_HEADS, GDN_DK, GDN_DV), F32),
                        pltpu.VMEM((SUBLANES, CONV_CH), F32)],
        compiler_params=_cparams(("parallel", "arbitrary")),
    )(qkv3, z3, sm3, smt3, cst8, s0, cw, gpr, gpc)


FOX_HEAD_GROUP = 4


def _fox_prompt_kernel(fq_ref, fkt_ref, fvt_ref, ct_ref, on_ref, o_ref, kb_scr, vb_scr, *, tq):
    qi = pl.program_id(1)
    nk = kb_scr.shape[0]

    @pl.when(qi == 0)
    def _():
        for j in range(nk):
            kb_scr[j] = fkt_ref[0, :, j * tq:(j + 1) * tq].astype(BF16)
            vb_scr[j] = fvt_ref[0, :, j * tq:(j + 1) * tq].astype(BF16)

    q_all = fq_ref[0] * (FOX_HEAD_DIM ** -0.5)
    onorm = on_ref[...]
    causal = _iota2((tq, tq), 1) <= _iota2((tq, tq), 0)
    outs = []
    for g0 in range(0, FOX_HEADS, FOX_HEAD_GROUP):
        heads = range(g0, g0 + FOX_HEAD_GROUP)
        qs = [q_all[:, h * FOX_HEAD_DIM:(h + 1) * FOX_HEAD_DIM].astype(BF16) for h in heads]

        def step(j, carry, masked, heads=heads, qs=qs):
            new = []
            for (m, l, acc), h, qh in zip(carry, heads, qs):
                kt = kb_scr[j, h * FOX_HEAD_DIM:(h + 1) * FOX_HEAD_DIM, :]
                vt = vb_scr[j, h * FOX_HEAD_DIM:(h + 1) * FOX_HEAD_DIM, :]
                s = jnp.dot(qh, kt, preferred_element_type=F32) - ct_ref[0, h, pl.ds(j, 1), :]
                if masked:
                    s = jnp.where(causal, s, NEG_BIG)
                m_new = jnp.maximum(m, jnp.max(s, axis=-1, keepdims=True))
                p = jnp.exp(s - m_new)
                alpha = jnp.exp(m - m_new)
                l = alpha * l + jnp.sum(p, axis=-1, keepdims=True)
                acc = alpha * acc + lax.dot_general(p.astype(BF16), vt, (((1,), (1,)), ((), ())),
                                                    preferred_element_type=F32)
                new.append((m_new, l, acc))
            return tuple(new)

        init = tuple((jnp.full((tq, 1), NEG_BIG, F32), jnp.zeros((tq, 1), F32),
                      jnp.zeros((tq, FOX_HEAD_DIM), F32)) for _ in heads)
        carry = lax.fori_loop(0, qi, functools.partial(step, masked=False), init)
        for m, l, acc in step(qi, carry, True):
            outs.append(_rms(acc / l, onorm))
    o_ref[0] = jnp.concatenate(outs, axis=-1)


def _fox_prompt(fq3, fkt, fvt, ct4, onorm, tq):
    b, seq, _ = fq3.shape
    nq = seq // tq
    return pl.pallas_call(
        functools.partial(_fox_prompt_kernel, tq=tq),
        grid=(b, nq),
        in_specs=[pl.BlockSpec((1, tq, FOX_WIDTH), lambda i, j: (i, j, 0)),
                  pl.BlockSpec((1, FOX_WIDTH, seq), lambda i, j: (i, 0, 0)),
                  pl.BlockSpec((1, FOX_WIDTH, seq), lambda i, j: (i, 0, 0)),
                  pl.BlockSpec((1, FOX_HEADS, nq, tq), lambda i, j: (i, 0, 0, 0)),
                  pl.BlockSpec(onorm.shape, lambda i, j: (0, 0))],
        out_specs=pl.BlockSpec((1, tq, FOX_WIDTH), lambda i, j: (i, j, 0)),
        out_shape=jax.ShapeDtypeStruct((b, seq, FOX_WIDTH), F32),
        scratch_shapes=[pltpu.VMEM((nq, FOX_WIDTH, tq), BF16), pltpu.VMEM((nq, FOX_WIDTH, tq), BF16)],
        compiler_params=_cparams(("parallel", "arbitrary")),
    )(fq3, fkt, fvt, ct4, onorm)


def _logf_pages_kernel(lf_ref, w_ref, o_ref):
    lf = lf_ref[...]
    a = lf.astype(BF16)
    r1 = lf - a.astype(F32)
    b = r1.astype(BF16)
    c = (r1 - b.astype(F32)).astype(BF16)
    w = w_ref[...]
    o_ref[...] = (jnp.dot(a, w, preferred_element_type=F32) + jnp.dot(b, w, preferred_element_type=F32)
                  + jnp.dot(c, w, preferred_element_type=F32))


def _logf_pages(lf_flat, w2):
    n_pool, width = lf_flat.shape
    tp = next(c for c in (2048, 1024, 512, 256, 128, 64, 32, 16, 8) if n_pool % c == 0)
    return pl.pallas_call(
        _logf_pages_kernel,
        grid=(n_pool // tp,),
        in_specs=[pl.BlockSpec((tp, width), lambda i: (i, 0)),
                  pl.BlockSpec(w2.shape, lambda i: (0, 0))],
        out_specs=pl.BlockSpec((tp, 2 * width), lambda i: (i, 0)),
        out_shape=jax.ShapeDtypeStruct((n_pool, 2 * width), F32),
        compiler_params=_cparams(("parallel",)),
    )(lf_flat, w2)


def _fox_sample_kernel(pt_ref, qbd_ref, kn_ref, vn_ref, cn_ref, on_ref, *rest, pps, lq):
    k_refs = rest[0:pps]
    v_refs = rest[pps:2 * pps]
    r_refs = rest[2 * pps:3 * pps]
    o_ref = rest[3 * pps]
    m_scr, l_scr, acc_scr, suf_scr = rest[3 * pps + 1:]
    step = pl.program_id(1)
    nstep = pl.num_programs(1)
    rows = lq * FOX_HEADS
    qbd = qbd_ref[0]
    tile_rows = lambda v: jnp.concatenate([v] * lq, axis=0)

    @pl.when(step == 0)
    def _():
        s = lax.dot_general(qbd, kn_ref[0].astype(BF16), (((1,), (1,)), ((), ())),
                            preferred_element_type=F32)
        s = s - tile_rows(cn_ref[0])
        s = jnp.where(_iota2((rows, lq), 1) <= _iota2((rows, lq), 0) // FOX_HEADS, s, NEG_BIG)
        m = jnp.max(s, axis=-1, keepdims=True)
        p = jnp.exp(s - m)
        m_scr[...] = m
        l_scr[...] = jnp.sum(p, axis=-1, keepdims=True)
        acc_scr[...] = jnp.dot(p.astype(BF16), vn_ref[0].astype(BF16), preferred_element_type=F32)
        suf_scr[...] = jnp.zeros_like(suf_scr)

    page = k_refs[0].shape[3]
    suf = suf_scr[...]
    scores = []
    for i in range(pps):
        r2 = r_refs[i][0]
        s = jnp.dot(qbd, k_refs[i][0, 0].astype(BF16), preferred_element_type=F32)
        scores.append(s + tile_rows(r2[:, :page] + suf))
        suf = suf + r2[:, page:]
    suf_scr[...] = suf
    m = m_scr[...]
    m_new = m
    for s in scores:
        m_new = jnp.maximum(m_new, jnp.max(s, axis=-1, keepdims=True))
    alpha = jnp.exp(m - m_new)
    l = alpha * l_scr[...]
    acc = alpha * acc_scr[...]
    for i in range(pps):
        p = jnp.exp(scores[i] - m_new)
        l = l + jnp.sum(p, axis=-1, keepdims=True)
        acc = acc + lax.dot_general(p.astype(BF16), v_refs[i][0, 0].astype(BF16), (((1,), (1,)), ((), ())),
                                    preferred_element_type=F32)
    m_scr[...] = m_new
    l_scr[...] = l
    acc_scr[...] = acc

    @pl.when(step == nstep - 1)
    def _():
        own = _iota2((rows, FOX_WIDTH), 1) // FOX_HEAD_DIM == _iota2((rows, FOX_WIDTH), 0) % FOX_HEADS
        o = jnp.where(own, acc / l, 0.0)
        ms = jnp.sum(o * o, axis=-1, keepdims=True) * (1.0 / FOX_HEAD_DIM)
        o = o * lax.rsqrt(ms + NORM_EPS) * on_ref[...]
        o_ref[0] = jnp.sum(o.reshape(lq, FOX_HEADS, FOX_WIDTH), axis=1)


def _fox_sample(page_table, layer, qbd, kn3, vn3, cn3, onorm, ckt, cvt, r3, pps):
    b, lq, _ = kn3.shape
    n_pages = page_table.shape[1]
    page = ckt.shape[3]
    nstep = n_pages // pps
    rows = FOX_HEADS * lq

    def page_map(i):
        return lambda bi, s, pt: (layer, pt[bi, n_pages - 1 - (s * pps + i)], 0, 0)

    def r_map(i):
        return lambda bi, s, pt: (pt[bi, n_pages - 1 - (s * pps + i)], 0, 0)

    in_specs = [pl.BlockSpec((1, rows, FOX_WIDTH), lambda bi, s, pt: (bi, 0, 0)),
                pl.BlockSpec((1, lq, FOX_WIDTH), lambda bi, s, pt: (bi, 0, 0)),
                pl.BlockSpec((1, lq, FOX_WIDTH), lambda bi, s, pt: (bi, 0, 0)),
                pl.BlockSpec((1, FOX_HEADS, lq), lambda bi, s, pt: (bi, 0, 0)),
                pl.BlockSpec(onorm.shape, lambda bi, s, pt: (0, 0))]
    in_specs += [pl.BlockSpec((1, 1, FOX_WIDTH, page), page_map(i)) for i in range(pps)]
    in_specs += [pl.BlockSpec((1, 1, FOX_WIDTH, page), page_map(i)) for i in range(pps)]
    in_specs += [pl.BlockSpec((1, FOX_HEADS, 2 * page), r_map(i)) for i in range(pps)]
    grid_spec = pltpu.PrefetchScalarGridSpec(
        num_scalar_prefetch=1,
        grid=(b, nstep),
        in_specs=in_specs,
        out_specs=pl.BlockSpec((1, lq, FOX_WIDTH), lambda bi, s, pt: (bi, 0, 0)),
        scratch_shapes=[pltpu.VMEM((rows, 1), F32), pltpu.VMEM((rows, 1), F32),
                        pltpu.VMEM((rows, FOX_WIDTH), F32), pltpu.VMEM((FOX_HEADS, page), F32)],
    )
    return pl.pallas_call(
        functools.partial(_fox_sample_kernel, pps=pps, lq=lq),
        grid_spec=grid_spec,
        out_shape=jax.ShapeDtypeStruct((b, lq, FOX_WIDTH), F32),
        compiler_params=_cparams(("parallel", "arbitrary")),
    )(page_table, qbd, kn3, vn3, cn3, onorm, *([ckt] * pps), *([cvt] * pps), *([r3] * pps))


def _out_proj_kernel(x_ref, a_ref, b_ref, w_ref, o_ref):
    o_ref[...] = (x_ref[...]
                  + jnp.dot(a_ref[...].astype(BF16), w_ref[0:GDN_WIDTH, :], preferred_element_type=F32)
                  + jnp.dot(b_ref[...].astype(BF16), w_ref[GDN_WIDTH:, :], preferred_element_type=F32))


def _out_proj(x2, a2, b2, w, tm):
    t = x2.shape[0]
    return pl.pallas_call(
        _out_proj_kernel,
        grid=(t // tm,),
        in_specs=[pl.BlockSpec((tm, D_MODEL), lambda i: (i, 0)),
                  pl.BlockSpec((tm, GDN_WIDTH), lambda i: (i, 0)),
                  pl.BlockSpec((tm, FOX_WIDTH), lambda i: (i, 0)),
                  pl.BlockSpec(w.shape, lambda i: (0, 0))],
        out_specs=pl.BlockSpec((tm, D_MODEL), lambda i: (i, 0)),
        out_shape=jax.ShapeDtypeStruct((t, D_MODEL), F32),
        compiler_params=_cparams(("parallel",)),
    )(x2, a2, b2, w)


def _xattn_kernel(x_ref, g_ref, wq_ref, wo_ref, mk_ref, mv_ref, o_ref):
    x = x_ref[0]
    hb = _rms(x, g_ref[...]).astype(BF16)
    q = jnp.dot(hb, wq_ref[...], preferred_element_type=F32) * (XA_HEAD_DIM ** -0.5)
    mk = mk_ref[0].astype(BF16)
    mv = mv_ref[0].astype(BF16)
    outs = []
    for h in range(XA_HEADS):
        lo = h * XA_HEAD_DIM
        s = lax.dot_general(q[:, lo:lo + XA_HEAD_DIM].astype(BF16), mk[:, lo:lo + XA_HEAD_DIM],
                            (((1,), (1,)), ((), ())), preferred_element_type=F32)
        p = jnp.exp(s - jnp.max(s, axis=-1, keepdims=True))
        p = p / jnp.sum(p, axis=-1, keepdims=True)
        outs.append(jnp.dot(p.astype(BF16), mv[:, lo:lo + XA_HEAD_DIM], preferred_element_type=F32))
    o = jnp.concatenate(outs, axis=-1).astype(BF16)
    o_ref[0] = x + jnp.dot(o, wo_ref[...], preferred_element_type=F32)


def _xattn(x3, g, wq, wo, mk3, mv3, tq):
    b, seq, _ = x3.shape
    n_mem = mk3.shape[1]
    return pl.pallas_call(
        _xattn_kernel,
        grid=(b, seq // tq),
        in_specs=[pl.BlockSpec((1, tq, D_MODEL), lambda i, j: (i, j, 0)),
                  pl.BlockSpec(g.shape, lambda i, j: (0, 0)),
                  pl.BlockSpec(wq.shape, lambda i, j: (0, 0)),
                  pl.BlockSpec(wo.shape, lambda i, j: (0, 0)),
                  pl.BlockSpec((1, n_mem, XA_WIDTH), lambda i, j: (i, 0, 0)),
                  pl.BlockSpec((1, n_mem, XA_WIDTH), lambda i, j: (i, 0, 0))],
        out_specs=pl.BlockSpec((1, tq, D_MODEL), lambda i, j: (i, j, 0)),
        out_shape=jax.ShapeDtypeStruct((b, seq, D_MODEL), F32),
        compiler_params=_cparams(("parallel", "parallel")),
    )(x3, g, wq, wo, mk3, mv3)


def _router_kernel(x_ref, g_ref, wr_ref, br_ref, h_ref, route_ref, gate_ref, cnt_ref, base_scr, *, tm):
    i = pl.program_id(0)

    @pl.when(i == 0)
    def _():
        base_scr[...] = jnp.zeros_like(base_scr)

    h = _rms(x_ref[...], g_ref[...])
    _store_row_tiles(h_ref, h, tm)
    logits = _dot_hi(h, wr_ref[...]) + br_ref[...]
    lane = _iota2((tm, LANES), 1)
    lane_f = lane.astype(F32)
    vals, hots, idxs = [], [], []
    cur = logits
    for _ in range(TOP_K):
        mx = jnp.max(cur, axis=-1, keepdims=True)
        idx_f = jnp.min(jnp.where(cur == mx, lane_f, float(LANES)), axis=-1, keepdims=True)
        hot = lane_f == idx_f
        vals.append(mx)
        idxs.append(idx_f.astype(I32))
        hots.append(hot)
        cur = jnp.where(hot, -jnp.inf, cur)
    exps = [jnp.exp(v - vals[0]) for v in vals]
    denom = exps[0] + exps[1] + exps[2] + exps[3]
    member = (hots[0] | hots[1] | hots[2] | hots[3]).astype(F32)
    strict = (_iota2((tm, tm), 1) < _iota2((tm, tm), 0)).astype(BF16)
    before = jnp.dot(strict, member.astype(BF16), preferred_element_type=F32) + base_scr[...]
    route = jnp.zeros((tm, LANES), I32)
    gate = jnp.zeros((tm, LANES), F32)
    for k in range(TOP_K):
        rank = jnp.sum(jnp.where(hots[k], before, 0.0), axis=-1, keepdims=True).astype(I32)
        route = jnp.where(lane == k, idxs[k], route)
        route = jnp.where(lane == TOP_K + k, rank, route)
        gate = jnp.where(lane == k, exps[k] / denom, gate)
    route_ref[...] = route
    gate_ref[...] = gate
    base_scr[...] = base_scr[...] + jnp.sum(member, axis=0, keepdims=True)
    cnt_ref[...] = base_scr[...]


def _router(x2, g, wr, br, tm):
    t = x2.shape[0]
    return pl.pallas_call(
        functools.partial(_router_kernel, tm=tm),
        grid=(t // tm,),
        in_specs=[pl.BlockSpec((tm, D_MODEL), lambda i: (i, 0)),
                  pl.BlockSpec(g.shape, lambda i: (0, 0)),
                  pl.BlockSpec(wr.shape, lambda i: (0, 0)),
                  pl.BlockSpec(br.shape, lambda i: (0, 0))],
        out_specs=[pl.BlockSpec((tm * ROW_TILE, LANES), lambda i: (i, 0)),
                   pl.BlockSpec((tm, LANES), lambda i: (i, 0)),
                   pl.BlockSpec((tm, LANES), lambda i: (i, 0)),
                   pl.BlockSpec((1, LANES), lambda i: (0, 0))],
        out_shape=[jax.ShapeDtypeStruct((t * ROW_TILE, LANES), F32), jax.ShapeDtypeStruct((t, LANES), I32),
                   jax.ShapeDtypeStruct((t, LANES), F32), jax.ShapeDtypeStruct((1, LANES), F32)],
        scratch_shapes=[pltpu.VMEM((1, LANES), F32)],
        compiler_params=_cparams(("arbitrary",)),
    )(x2, g, wr, br)


def _dispatch_kernel(dest_ref, h_ref, init_hbm, xs_hbm, sem, *, tm):
    del init_hbm
    npair = tm * TOP_K

    def issue(p, c):
        src = pl.multiple_of((p // TOP_K) * ROW_TILE, ROW_TILE)
        dst = pl.multiple_of(dest_ref[0, 0, p] * ROW_TILE, ROW_TILE)
        pltpu.make_async_copy(h_ref.at[pl.ds(src, ROW_TILE)], xs_hbm.at[pl.ds(dst, ROW_TILE)], sem).start()
        return c

    lax.fori_loop(0, npair, issue, 0, unroll=8)
    for _ in range(TOP_K):
        pltpu.make_async_copy(h_ref, xs_hbm.at[pl.ds(0, tm * ROW_TILE)], sem).wait()


def _dispatch(dest3, h2, rows, tm):
    t = h2.shape[0] // ROW_TILE
    init = jnp.zeros((rows * ROW_TILE, LANES), F32)
    return pl.pallas_call(
        functools.partial(_dispatch_kernel, tm=tm),
        grid=(t // tm,),
        in_specs=[pl.BlockSpec((1, 1, tm * TOP_K), lambda i: (i, 0, 0), memory_space=pltpu.SMEM),
                  pl.BlockSpec((tm * ROW_TILE, LANES), lambda i: (i, 0)),
                  pl.BlockSpec(memory_space=pl.ANY)],
        out_specs=pl.BlockSpec(memory_space=pl.ANY),
        out_shape=jax.ShapeDtypeStruct((rows * ROW_TILE, LANES), F32),
        scratch_shapes=[pltpu.SemaphoreType.DMA(())],
        input_output_aliases={2: 0},
        compiler_params=pltpu.CompilerParams(dimension_semantics=("arbitrary",), vmem_limit_bytes=VMEM_LIMIT,
                                             has_side_effects=True),
    )(dest3, h2, init)


def _ffn_kernel(be_ref, nv_ref, x_ref, w1_ref, b1_ref, w2_ref, b2_ref, o_ref):
    i = pl.program_id(0)
    live = i * EXPERT_BLOCK < nv_ref[0]

    @pl.when(live)
    def _():
        x = _load_row_tiles(x_ref, EXPERT_BLOCK).astype(BF16)
        hb = jnp.dot(x, w1_ref[0], preferred_element_type=F32) + b1_ref[0]
        glu = jnp.minimum(hb[:, :D_FF], SWIGLU_LIMIT)
        lin = jnp.clip(hb[:, D_FF:], -SWIGLU_LIMIT, SWIGLU_LIMIT)
        act = glu * _sigmoid(SWIGLU_ALPHA * glu) * (lin + 1.0)
        y = jnp.dot(act.astype(BF16), w2_ref[0], preferred_element_type=F32) + b2_ref[0]
        _store_row_tiles(o_ref, y, EXPERT_BLOCK)

    @pl.when(jnp.logical_not(live))
    def _():
        o_ref[...] = jnp.zeros_like(o_ref)


def _ffn(block_e, nvalid, xs, w1, b1, w2, b2):
    rows = xs.shape[0] // ROW_TILE
    nb = rows // EXPERT_BLOCK
    blk = pl.BlockSpec((EXPERT_BLOCK * ROW_TILE, LANES), lambda i, be, nv: (i, 0))
    grid_spec = pltpu.PrefetchScalarGridSpec(
        num_scalar_prefetch=2,
        grid=(nb,),
        in_specs=[blk,
                  pl.BlockSpec((1, D_MODEL, 2 * D_FF), lambda i, be, nv: (be[i], 0, 0)),
                  pl.BlockSpec((1, 1, 2 * D_FF), lambda i, be, nv: (be[i], 0, 0)),
                  pl.BlockSpec((1, D_FF, D_MODEL), lambda i, be, nv: (be[i], 0, 0)),
                  pl.BlockSpec((1, 1, D_MODEL), lambda i, be, nv: (be[i], 0, 0))],
        out_specs=blk,
    )
    return pl.pallas_call(
        _ffn_kernel,
        grid_spec=grid_spec,
        out_shape=jax.ShapeDtypeStruct((rows * ROW_TILE, LANES), F32),
        compiler_params=_cparams(("arbitrary",)),
    )(block_e, nvalid, xs, w1, b1, w2, b2)


def _combine_kernel(dcur_ref, dnext_ref, x_ref, gate_ref, fg_ref, os_hbm, y_ref, yn_ref, buf, sem, *, tm):
    i = pl.program_id(0)
    n = pl.num_programs(0)
    npair = tm * TOP_K

    def issue_tile(dref, slot):
        def issue(p, c):
            src = pl.multiple_of(dref[0, 0, p] * ROW_TILE, ROW_TILE)
            dst = pl.multiple_of((p // TOP_K) * ROW_TILE, ROW_TILE)
            pltpu.make_async_copy(os_hbm.at[pl.ds(src, ROW_TILE)],
                                  buf.at[slot, p % TOP_K, pl.ds(dst, ROW_TILE)], sem.at[slot]).start()
            return c

        lax.fori_loop(0, npair, issue, 0, unroll=8)

    @pl.when(i == 0)
    def _():
        issue_tile(dcur_ref, 0)

    @pl.when(i + 1 < n)
    def _():
        issue_tile(dnext_ref, (i + 1) % 2)

    slot = i % 2
    for k in range(TOP_K):
        pltpu.make_async_copy(os_hbm.at[pl.ds(0, tm * ROW_TILE)], buf.at[slot, k], sem.at[slot]).wait()
    gate = gate_ref[...]
    x = x_ref[...]
    pieces = []
    for s in range(ROW_TILE):
        acc = x[:, s * LANES:(s + 1) * LANES]
        for k in range(TOP_K):
            acc = acc + gate[:, k:k + 1] * buf[slot, k, pl.ds(s, tm, stride=ROW_TILE), :]
        pieces.append(acc)
    y = jnp.concatenate(pieces, axis=-1)
    y_ref[...] = y
    yn_ref[...] = _rms(y, fg_ref[...])


def _combine(dest3, x2, gate, fg, os2, tm):
    t = x2.shape[0]
    nt = t // tm
    return pl.pallas_call(
        functools.partial(_combine_kernel, tm=tm),
        grid=(nt,),
        in_specs=[pl.BlockSpec((1, 1, tm * TOP_K), lambda i: (i, 0, 0), memory_space=pltpu.SMEM),
                  pl.BlockSpec((1, 1, tm * TOP_K), lambda i: (jnp.minimum(i + 1, nt - 1), 0, 0),
                               memory_space=pltpu.SMEM),
                  pl.BlockSpec((tm, D_MODEL), lambda i: (i, 0)),
                  pl.BlockSpec((tm, LANES), lambda i: (i, 0)),
                  pl.BlockSpec(fg.shape, lambda i: (0, 0)),
                  pl.BlockSpec(memory_space=pl.ANY)],
        out_specs=[pl.BlockSpec((tm, D_MODEL), lambda i: (i, 0)),
                   pl.BlockSpec((tm, D_MODEL), lambda i: (i, 0))],
        out_shape=[jax.ShapeDtypeStruct((t, D_MODEL), F32), jax.ShapeDtypeStruct((t, D_MODEL), F32)],
        scratch_shapes=[pltpu.VMEM((2, TOP_K, tm * ROW_TILE, LANES), F32), pltpu.SemaphoreType.DMA((2,))],
        compiler_params=_cparams(("arbitrary",)),
    )(dest3, dest3, x2, gate, fg, os2)


def _row_tile(t, want):
    tm = min(want, t)
    assert t % tm == 0
    return tm


def _mixer(x3, lw, conv_state8, s0, fox_fn, kv_t):
    b, seq, _ = x3.shape
    t = b * seq
    x2 = x3.reshape(t, D_MODEL)
    qkv, z, fq, fk, fv, sm, smt = _in_proj(x2, lw["norm1_g"], lw["w_main"], lw["w_kv_t"], lw["w_small"],
                                           lw["w_small_t"], _row_tile(t, 256), seq if kv_t else None)
    sm3 = sm.reshape(b, seq, LANES)
    smt3 = jnp.transpose(smt.reshape(16, b, seq), (1, 0, 2))
    logf_t, ct = _logf(smt3, lw["fb_col"])
    logf = jnp.transpose(logf_t, (0, 2, 1))
    c = min(GDN_CHUNK, seq)
    tl = min(256, seq)
    gdn_out, s_new, cbuf = _gdn(qkv.reshape(b, seq, CONV_CH), z.reshape(b, seq, GDN_WIDTH), sm3, smt3,
                                conv_state8, s0, lw["conv_w"], lw["gp_row"], lw["gp_col"], tl, c,
                                exact_small=(c <= SUBLANES))
    fq3 = fq.reshape(b, seq, FOX_WIDTH)
    if kv_t:
        fk3, fv3 = fk, fv
        as_out = lambda a: jnp.transpose(a.reshape(b, FOX_HEADS, FOX_HEAD_DIM, seq), (0, 3, 1, 2))
    else:
        fk3, fv3 = fk.reshape(b, seq, FOX_WIDTH), fv.reshape(b, seq, FOX_WIDTH)
        as_out = lambda a: a.reshape(b, seq, FOX_HEADS, FOX_HEAD_DIM)
    fox_out = fox_fn(fq3, fk3, fv3, ct)
    y2 = _out_proj(x2, gdn_out.reshape(t, GDN_WIDTH), fox_out.reshape(t, FOX_WIDTH), lw["w_out"],
                   _row_tile(t, 512))
    return y2.reshape(b, seq, D_MODEL), s_new, cbuf, as_out(fk3), as_out(fv3), logf


def _moe(x3, lw, final_g):
    b, seq, _ = x3.shape
    t = b * seq
    x2 = x3.reshape(t, D_MODEL)
    h2, route, gate, counts = _router(x2, lw["norm3_g"], lw["w_router"], lw["b_router"], _row_tile(t, 256))
    cnt = counts[0, :N_EXPERTS].astype(I32)
    padded = (cnt + EXPERT_BLOCK - 1) // EXPERT_BLOCK * EXPERT_BLOCK
    pends = jnp.cumsum(padded)
    pstart = pends - padded
    nb = t * TOP_K // EXPERT_BLOCK + N_EXPERTS
    rows = nb * EXPERT_BLOCK
    block_row0 = jnp.arange(nb, dtype=I32) * EXPERT_BLOCK
    block_e = jnp.minimum(jnp.sum((pends[None, :] <= block_row0[:, None]).astype(I32), axis=1), N_EXPERTS - 1)
    nvalid = pends[-1:].astype(I32)
    hot = route[:, :TOP_K, None] == jnp.arange(N_EXPERTS, dtype=I32)[None, None, :]
    dest = jnp.sum(jnp.where(hot, pstart[None, None, :], 0), axis=-1) + route[:, TOP_K:2 * TOP_K]
    tmd = _row_tile(t, 256)
    xs = _dispatch(dest.reshape(t // tmd, 1, tmd * TOP_K), h2, rows, tmd)
    os2 = _ffn(block_e, nvalid, xs, lw["w1"], lw["b1"], lw["w2"], lw["b2"])
    tmc = _row_tile(t, 256)
    y2, yn2 = _combine(dest.reshape(t // tmc, 1, tmc * TOP_K), x2, gate, final_g, os2, tmc)
    return y2.reshape(b, seq, D_MODEL), yn2.reshape(b, seq, D_MODEL)


def _prep_layer(l, norm1_g, w_in, conv_w, gdn_a_log, gdn_dt_bias, gdn_onorm, fox_fbias, fox_onorm, w_out,
                norm2_g, mem_norm_g, w_xq, w_mkv, w_xo, norm3_g, w_router, b_router, w1, b1, w2, b2):
    wi = w_in[l]
    gq, gk, gv, gz, gb, ga, fq, fk, fv, ff = _split_in(wi)
    w_main = jnp.concatenate([gq, gk, gv, gz, fq, fk, fv], axis=1).astype(BF16)
    w_kv_t = jnp.transpose(jnp.concatenate([fk, fv], axis=1)).astype(BF16)
    w_small = jnp.concatenate([gb, ga, ff, jnp.zeros((D_MODEL, LANES - 16), F32)], axis=1).astype(BF16)
    w_small_t = jnp.transpose(w_small[:, :16])
    lanes = lambda v, off: jnp.zeros((LANES,), F32).at[off:off + v.shape[0]].set(v)
    gp_row = jnp.zeros((SUBLANES, LANES), F32)
    gp_row = gp_row.at[0].set(lanes(gdn_a_log[l], GDN_HEADS)).at[1].set(lanes(gdn_dt_bias[l], GDN_HEADS))
    gp_row = gp_row.at[2].set(gdn_onorm[l])
    gp_col = jnp.zeros((16, LANES), F32)
    gp_col = gp_col.at[GDN_HEADS:2 * GDN_HEADS, 0].set(gdn_a_log[l]).at[GDN_HEADS:2 * GDN_HEADS, 1].set(gdn_dt_bias[l])
    fb_col = jnp.zeros((16, LANES), F32).at[8:16, 0].set(fox_fbias[l])
    wr = jnp.concatenate([w_router[l], jnp.zeros((D_MODEL, LANES - N_EXPERTS), F32)], axis=1)
    br = jnp.full((1, LANES), NEG_BIG, F32).at[0, :N_EXPERTS].set(b_router[l])
    return {
        "norm1_g": norm1_g[l].reshape(1, D_MODEL), "w_main": w_main, "w_kv_t": w_kv_t,
        "w_small": w_small, "w_small_t": w_small_t,
        "conv_w": conv_w[l], "gp_row": gp_row, "gp_col": gp_col, "fb_col": fb_col,
        "fox_onorm": fox_onorm[l].reshape(1, FOX_HEAD_DIM), "w_out": w_out[l].astype(BF16),
        "norm2_g": norm2_g[l].reshape(1, D_MODEL), "mem_norm_g": mem_norm_g[l].reshape(1, D_MODEL),
        "w_xq": w_xq[l].astype(BF16), "w_mkv": w_mkv[l].astype(BF16), "w_xo": w_xo[l].astype(BF16),
        "norm3_g": norm3_g[l].reshape(1, D_MODEL), "w_router": wr, "b_router": br,
        "w1": w1[l].astype(BF16), "b1": b1[l].reshape(N_EXPERTS, 1, 2 * D_FF),
        "w2": w2[l].astype(BF16), "b2": b2[l].reshape(N_EXPERTS, 1, D_MODEL),
    }


def _split_in(wi):
    widths = (GDN_WIDTH, GDN_WIDTH, GDN_WIDTH, GDN_WIDTH, GDN_HEADS, GDN_HEADS,
              FOX_WIDTH, FOX_WIDTH, FOX_WIDTH, FOX_HEADS)
    outs, c0 = [], 0
    for w in widths:
        outs.append(wi[:, c0:c0 + w])
        c0 += w
    return outs


def kernel(x_prompt, x_sample, mem_prompt, cache_fox_k, cache_fox_v, cache_fox_logf, cache_mem_k, cache_mem_v, state_gdn, state_conv, page_table, norm1_g, w_in, conv_w, gdn_a_log, gdn_dt_bias, gdn_onorm, fox_fbias, fox_onorm, w_out, norm2_g, mem_norm_g, w_xq, w_mkv, w_xo, norm3_g, w_router, b_router, w1, b1, w2, b2, final_norm_g):
    depth = w_in.shape[0]
    bp, lp, _ = x_prompt.shape
    bs, ls, _ = x_sample.shape
    n_mem = mem_prompt.shape[1]
    n_pool, page = cache_fox_k.shape[1], cache_fox_k.shape[2]
    ckt = jnp.transpose(cache_fox_k, (0, 1, 3, 4, 2)).reshape(depth, n_pool, FOX_WIDTH, page)
    cvt = jnp.transpose(cache_fox_v, (0, 1, 3, 4, 2)).reshape(depth, n_pool, FOX_WIDTH, page)
    lf_rows = jnp.transpose(cache_fox_logf, (0, 1, 3, 2)).reshape(depth, n_pool * FOX_HEADS, page)
    tok = jnp.arange(page, dtype=I32)
    later = tok[:, None] > tok[None, :]
    w_suffix = jnp.concatenate([later, jnp.ones_like(later)], axis=1).astype(BF16)
    head_mask = (jnp.arange(FOX_HEADS)[:, None] == jnp.arange(FOX_HEADS)[None, :]).astype(F32)
    final_g = final_norm_g.reshape(1, D_MODEL)
    xp, xs = x_prompt, x_sample
    yp = ys = None
    outs = {k: [] for k in ("fkp", "fvp", "flp", "mkp", "mvp", "sgp", "scp", "fks", "fvs", "fls", "sgs", "scs")}
    for l in range(depth):
        lw = _prep_layer(l, norm1_g, w_in, conv_w, gdn_a_log, gdn_dt_bias, gdn_onorm, fox_fbias, fox_onorm,
                         w_out, norm2_g, mem_norm_g, w_xq, w_mkv, w_xo, norm3_g, w_router, b_router,
                         w1, b1, w2, b2)
        tq = min(256, lp)

        def fox_p(fq3, fk3, fv3, ct, lw=lw, tq=tq):
            return _fox_prompt(fq3, fk3, fv3, ct.reshape(bp, FOX_HEADS, lp // tq, tq), lw["fox_onorm"], tq)

        xp, s_new, cbuf, k_new, v_new, lf_new = _mixer(
            xp, lw, jnp.zeros((bp, SUBLANES, CONV_CH), F32),
            jnp.zeros((bp, GDN_HEADS, GDN_DK, GDN_DV), F32), fox_p, kv_t=True)
        outs["fkp"].append(k_new); outs["fvp"].append(v_new); outs["flp"].append(lf_new)
        outs["sgp"].append(s_new); outs["scp"].append(cbuf)

        r3 = _logf_pages(lf_rows[l], w_suffix).reshape(n_pool, FOX_HEADS, 2 * page)
        on_tiled = jnp.tile(lw["fox_onorm"], (1, FOX_HEADS))

        def fox_s(fq3, fk3, fv3, ct, lw=lw, l=l, r3=r3, on_tiled=on_tiled):
            q4 = fq3.reshape(bs, ls, FOX_HEADS, FOX_HEAD_DIM) * (FOX_HEAD_DIM ** -0.5)
            qbd = jnp.einsum("bqhd,hg->bqhgd", q4, head_mask).reshape(bs, ls * FOX_HEADS, FOX_WIDTH)
            return _fox_sample(page_table, l, qbd.astype(BF16), fk3, fv3, ct, on_tiled, ckt, cvt, r3, pps=8)

        cst8 = jnp.pad(state_conv[l], ((0, 0), (SUBLANES - (CONV_W - 1), 0), (0, 0)))
        xs, s_new, cbuf, k_new, v_new, lf_new = _mixer(xs, lw, cst8, state_gdn[l], fox_s, kv_t=False)
        outs["fks"].append(k_new); outs["fvs"].append(v_new); outs["fls"].append(lf_new)
        outs["sgs"].append(s_new); outs["scs"].append(cbuf)

        mk2, mv2 = _norm_proj(mem_prompt.reshape(bp * n_mem, D_MODEL), lw["mem_norm_g"], lw["w_mkv"],
                              (XA_WIDTH, XA_WIDTH), _row_tile(bp * n_mem, 512))
        outs["mkp"].append(mk2.reshape(bp, n_mem, XA_HEADS, XA_HEAD_DIM))
        outs["mvp"].append(mv2.reshape(bp, n_mem, XA_HEADS, XA_HEAD_DIM))
        xp = _xattn(xp, lw["norm2_g"], lw["w_xq"], lw["w_xo"], mk2.reshape(bp, n_mem, XA_WIDTH),
                    mv2.reshape(bp, n_mem, XA_WIDTH), min(512, lp))
        xs = _xattn(xs, lw["norm2_g"], lw["w_xq"], lw["w_xo"], cache_mem_k[l].reshape(bs, n_mem, XA_WIDTH),
                    cache_mem_v[l].reshape(bs, n_mem, XA_WIDTH), ls)

        xp, yp = _moe(xp, lw, final_g)
        xs, ys = _moe(xs, lw, final_g)
    st = jnp.stack
    return (yp, ys, st(outs["fkp"]), st(outs["fvp"]), st(outs["flp"]), st(outs["mkp"]), st(outs["mvp"]),
            st(outs["sgp"]), st(outs["scp"]), st(outs["fks"]), st(outs["fvs"]), st(outs["fls"]),
            st(outs["sgs"]), st(outs["scs"]))
```

```python
import functools
import math

import jax
import jax.numpy as jnp
from jax import lax
from jax.experimental import pallas as pl
from jax.experimental.pallas import tpu as pltpu

F32 = jnp.float32
BF16 = jnp.bfloat16
I32 = jnp.int32
HIGHEST = lax.Precision.HIGHEST

D_MODEL = 1024
GDN_HEADS = 4
GDN_DK = 128
GDN_DV = 128
GDN_WIDTH = GDN_HEADS * GDN_DV
CONV_W = 4
CONV_CH = 3 * GDN_WIDTH
GDN_CHUNK = 64
FOX_HEADS = 8
FOX_HEAD_DIM = 64
FOX_WIDTH = FOX_HEADS * FOX_HEAD_DIM
XA_HEADS = 4
XA_HEAD_DIM = 128
XA_WIDTH = XA_HEADS * XA_HEAD_DIM
N_EXPERTS = 32
TOP_K = 4
D_FF = D_MODEL
SWIGLU_LIMIT = 7.0
SWIGLU_ALPHA = 1.702
NORM_EPS = 1e-6
NEG_BIG = -1e30

LANES = 128
SUBLANES = 8
VMEM_LIMIT = 52 * 1024 * 1024
MAIN_COLS = 2 * GDN_WIDTH + 2 * GDN_WIDTH + 3 * FOX_WIDTH
EXPERT_BLOCK = 256
EXPERT_BLOCK_LARGE = 512


def _cparams(sem):
    return pltpu.CompilerParams(dimension_semantics=sem, vmem_limit_bytes=VMEM_LIMIT)


def _dot(a, b):
    return jnp.dot(a.astype(BF16), b.astype(BF16), preferred_element_type=F32)


def _dot_nt(a, b):
    return lax.dot_general(a.astype(BF16), b.astype(BF16), (((1,), (1,)), ((), ())),
                           preferred_element_type=F32)


def _dot_tn(a, b):
    return lax.dot_general(a.astype(BF16), b.astype(BF16), (((0,), (0,)), ((), ())),
                           preferred_element_type=F32)


def _dot_hi(a, b):
    return jnp.dot(a, b, precision=HIGHEST, preferred_element_type=F32)


def _dot_nt_hi(a, b):
    return lax.dot_general(a, b, (((1,), (1,)), ((), ())), precision=HIGHEST,
                           preferred_element_type=F32)


def _rms(x, g):
    return x * lax.rsqrt(jnp.mean(x * x, axis=-1, keepdims=True) + NORM_EPS) * g


def _sigmoid(x):
    return 1.0 / (1.0 + jnp.exp(-x))


def _softplus(x):
    return jnp.maximum(x, 0.0) + jnp.log1p(jnp.exp(-jnp.abs(x)))


def _log_sigmoid(x):
    return jnp.minimum(x, 0.0) - jnp.log1p(jnp.exp(-jnp.abs(x)))


def _iota2(shape, dim):
    return lax.broadcasted_iota(I32, shape, dim)


ROW_TILE = D_MODEL // LANES


def _store_row_tiles(ref, val, n):
    for s in range(ROW_TILE):
        ref[pl.ds(s, n, stride=ROW_TILE), :] = val[:, s * LANES:(s + 1) * LANES]


def _load_row_tiles(ref, n):
    return jnp.concatenate([ref[pl.ds(s, n, stride=ROW_TILE), :] for s in range(ROW_TILE)], axis=-1)


def _in_proj_kernel(x_ref, g_ref, wm_ref, wkvt_ref, ws_ref, wst_ref,
                    qkv_ref, z_ref, fq_ref, fk_ref, fv_ref, sm_ref, smt_ref, *, kv_t):
    hb = _rms(x_ref[...], g_ref[...]).astype(BF16)
    nt = lambda w: lax.dot_general(w, hb, (((1,), (1,)), ((), ())), preferred_element_type=F32)
    c0 = 0
    for ref, width in ((qkv_ref, CONV_CH), (z_ref, GDN_WIDTH), (fq_ref, FOX_WIDTH)):
        ref[...] = jnp.dot(hb, wm_ref[:, c0:c0 + width], preferred_element_type=F32)
        c0 += width
    if kv_t:
        fk_ref[0] = nt(wkvt_ref[0:FOX_WIDTH, :])
        fv_ref[0] = nt(wkvt_ref[FOX_WIDTH:, :])
    else:
        fk_ref[...] = jnp.dot(hb, wm_ref[:, c0:c0 + FOX_WIDTH], preferred_element_type=F32)
        fv_ref[...] = jnp.dot(hb, wm_ref[:, c0 + FOX_WIDTH:], preferred_element_type=F32)
    sm_ref[...] = jnp.dot(hb, ws_ref[...], preferred_element_type=F32)
    smt_ref[...] = nt(wst_ref[...])


def _in_proj(x2, g, wm, wkvt, ws, wst, tm, kv_t_seq):
    t = x2.shape[0]
    row = lambda w: pl.BlockSpec((tm, w), lambda i: (i, 0))
    full = lambda a: pl.BlockSpec(a.shape, lambda i: (0,) * a.ndim)
    if kv_t_seq:
        per_b = kv_t_seq // tm
        kv_spec = pl.BlockSpec((1, FOX_WIDTH, tm), lambda i: (i // per_b, 0, i % per_b))
        kv_shape = jax.ShapeDtypeStruct((t // kv_t_seq, FOX_WIDTH, kv_t_seq), F32)
    else:
        kv_spec = row(FOX_WIDTH)
        kv_shape = jax.ShapeDtypeStruct((t, FOX_WIDTH), F32)
    return pl.pallas_call(
        functools.partial(_in_proj_kernel, kv_t=bool(kv_t_seq)),
        grid=(t // tm,),
        in_specs=[row(D_MODEL), full(g), full(wm), full(wkvt), full(ws), full(wst)],
        out_specs=[row(CONV_CH), row(GDN_WIDTH), row(FOX_WIDTH), kv_spec, kv_spec,
                   row(LANES), pl.BlockSpec((16, tm), lambda i: (0, i))],
        out_shape=[jax.ShapeDtypeStruct((t, CONV_CH), F32), jax.ShapeDtypeStruct((t, GDN_WIDTH), F32),
                   jax.ShapeDtypeStruct((t, FOX_WIDTH), F32), kv_shape, kv_shape,
                   jax.ShapeDtypeStruct((t, LANES), F32), jax.ShapeDtypeStruct((16, t), F32)],
        compiler_params=_cparams(("parallel",)),
    )(x2, g, wm, wkvt, ws, wst)


def _norm_proj_kernel(x_ref, g_ref, w_ref, *out_refs):
    hb = _rms(x_ref[...], g_ref[...]).astype(BF16)
    c0 = 0
    for ref in out_refs:
        width = ref.shape[-1]
        ref[...] = jnp.dot(hb, w_ref[:, c0:c0 + width], preferred_element_type=F32)
        c0 += width


def _norm_proj(x2, g, w, widths, tm):
    t = x2.shape[0]
    return pl.pallas_call(
        _norm_proj_kernel,
        grid=(t // tm,),
        in_specs=[pl.BlockSpec((tm, D_MODEL), lambda i: (i, 0)),
                  pl.BlockSpec(g.shape, lambda i: (0, 0)),
                  pl.BlockSpec(w.shape, lambda i: (0, 0))],
        out_specs=[pl.BlockSpec((tm, wd), lambda i: (i, 0)) for wd in widths],
        out_shape=[jax.ShapeDtypeStruct((t, wd), F32) for wd in widths],
        compiler_params=_cparams(("parallel",)),
    )(x2, g, w)


def _small_mm(a, b):
    acc = a[:, 0:1] * b[0:1, :]
    for i in range(1, a.shape[1]):
        acc = acc + a[:, i:i + 1] * b[i:i + 1, :]
    return acc


def _logf_kernel(smt_ref, fbc_ref, logf_ref, ct_ref, *, seq, chunk):
    lft = _log_sigmoid(smt_ref[0][8:16, :] + fbc_ref[8:16, 0:1])
    logf_ref[0] = lft
    tri = (_iota2((chunk, chunk), 0) <= _iota2((chunk, chunk), 1)).astype(F32)
    carry = jnp.zeros((FOX_HEADS, 1), F32)
    for c in range(seq // chunk):
        blk = lft[:, c * chunk:(c + 1) * chunk]
        cs = (_small_mm(blk, tri) if chunk <= SUBLANES else _dot_hi(blk, tri)) + carry
        ct_ref[0, :, c * chunk:(c + 1) * chunk] = cs
        carry = cs[:, chunk - 1:chunk]


def _logf(smt3, fbc):
    b, _, seq = smt3.shape
    chunk = min(256, seq)
    return pl.pallas_call(
        functools.partial(_logf_kernel, seq=seq, chunk=chunk),
        grid=(b,),
        in_specs=[pl.BlockSpec((1, 16, seq), lambda i: (i, 0, 0)),
                  pl.BlockSpec(fbc.shape, lambda i: (0, 0))],
        out_specs=[pl.BlockSpec((1, FOX_HEADS, seq), lambda i: (i, 0, 0)),
                   pl.BlockSpec((1, FOX_HEADS, seq), lambda i: (i, 0, 0))],
        out_shape=[jax.ShapeDtypeStruct((b, FOX_HEADS, seq), F32),
                   jax.ShapeDtypeStruct((b, FOX_HEADS, seq), F32)],
        compiler_params=_cparams(("parallel",)),
    )(smt3, fbc)


def _unit_lower_inverse(a, c, mm):
    eye = (_iota2((c, c), 0) == _iota2((c, c), 1)).astype(F32)
    n = -a
    t = eye + n
    p = n
    levels = int(math.log2(c))
    for _ in range(levels - 1):
        p = mm(p, p)
        t = t + mm(t, p)
    return t


def _gdn_kernel(qkv_ref, z_ref, sm_ref, smt_ref, cst_ref, s0_ref, cw_ref, gpr_ref, gpc_ref,
                o_ref, sn_ref, cb_ref, s_scr, tail_scr, *, tl, c, exact_small):
    t = pl.program_id(1)
    nt = pl.num_programs(1)

    @pl.when(t == 0)
    def _():
        s_scr[...] = s0_ref[0]
        tail_scr[...] = cst_ref[0]

    rb = (lambda v: v.astype(BF16).astype(F32)) if exact_small else (lambda v: v.astype(BF16))
    bdims = ((0,), (0,))
    mm = lambda a, b: lax.dot_general(rb(a), rb(b), (((2,), (1,)), bdims), preferred_element_type=F32)
    mm_nt = lambda a, b: lax.dot_general(rb(a), rb(b), (((2,), (2,)), bdims), preferred_element_type=F32)
    mm_tn = lambda a, b: lax.dot_general(rb(a), rb(b), (((1,), (1,)), bdims), preferred_element_type=F32)

    x = qkv_ref[0]
    tail = tail_scr[...]
    cw = cw_ref[...]
    row8 = _iota2((SUBLANES, CONV_CH), 0)
    acc = x * cw[CONV_W - 1:CONV_W, :]
    for s in range(1, CONV_W):
        xs = pltpu.roll(x, s, 0)
        head = jnp.where(row8 < s, pltpu.roll(tail, s, 0), xs[0:SUBLANES])
        xs = head if tl == SUBLANES else jnp.concatenate([head, xs[SUBLANES:]], axis=0)
        acc = acc + xs * cw[CONV_W - 1 - s:CONV_W - s, :]
    conv = acc * _sigmoid(acc)
    tail_scr[...] = x[tl - SUBLANES:tl, :]

    @pl.when(t == nt - 1)
    def _():
        cb_ref[0] = x[tl - (CONV_W - 1):tl, :]

    sm = sm_ref[0]
    beta_c = _sigmoid(sm)
    g_c = -jnp.exp(gpr_ref[0:1, :]) * _softplus(sm + gpr_ref[1:2, :])
    smt = smt_ref[0]
    g_r = -jnp.exp(gpc_ref[:, 0:1]) * _softplus(smt + gpc_ref[:, 1:2])
    onorm = gpr_ref[2:3, :]

    ii = _iota2((c, c), 0)
    jj = _iota2((c, c), 1)
    tri_c = (jj <= ii).astype(F32)
    tri_r = (ii <= jj).astype(F32)
    small = c <= SUBLANES
    nc = tl // c
    cum_cols, cum_rows = [], []
    for ci in range(nc):
        gc_blk = g_c[ci * c:(ci + 1) * c, :]
        gr_blk = g_r[:, ci * c:(ci + 1) * c]
        cum_cols.append(_small_mm(tri_c, gc_blk) if small else _dot_hi(tri_c, gc_blk))
        cum_rows.append(_small_mm(gr_blk, tri_r) if small else _dot_hi(gr_blk, tri_r))

    def per_pair(fn):
        return jnp.stack([fn(ci, h) for ci in range(nc) for h in range(GDN_HEADS)], axis=0)

    rows = lambda ci: slice(ci * c, (ci + 1) * c)
    qs = per_pair(lambda ci, h: conv[rows(ci), h * GDN_DK:(h + 1) * GDN_DK])
    ks = per_pair(lambda ci, h: conv[rows(ci), GDN_WIDTH + h * GDN_DK:GDN_WIDTH + (h + 1) * GDN_DK])
    vs = per_pair(lambda ci, h: conv[rows(ci), 2 * GDN_WIDTH + h * GDN_DV:2 * GDN_WIDTH + (h + 1) * GDN_DV])
    beta = per_pair(lambda ci, h: beta_c[rows(ci), h:h + 1])
    cum_c = per_pair(lambda ci, h: cum_cols[ci][:, GDN_HEADS + h:GDN_HEADS + h + 1])
    cum_r = per_pair(lambda ci, h: cum_rows[ci][GDN_HEADS + h:GDN_HEADS + h + 1, :])
    qs = qs * lax.rsqrt(jnp.sum(qs * qs, axis=-1, keepdims=True) + NORM_EPS) * (GDN_DK ** -0.5)
    ks = ks * lax.rsqrt(jnp.sum(ks * ks, axis=-1, keepdims=True) + NORM_EPS)
    cum_last = cum_c[:, c - 1:c, :]
    dec = jnp.exp(jnp.where(ii >= jj, cum_c - cum_r, NEG_BIG))
    dec_strict = jnp.where(ii > jj, dec, 0.0)
    a_mat = beta * mm_nt(ks, ks) * dec_strict
    tinv = _unit_lower_inverse(a_mat, c, mm)
    e_cum = jnp.exp(cum_c)
    rhs = jnp.concatenate([beta * vs, (beta * e_cum) * ks], axis=-1)
    sol = mm(tinv, rhs)
    w_v, w_k = sol[:, :, :GDN_DV], sol[:, :, GDN_DV:]
    p_qk = mm_nt(qs, ks) * dec
    q_g = qs * e_cum
    k_d = ks * jnp.exp(cum_last - cum_c)
    g_end = jnp.exp(cum_last)

    state = s_scr[...]
    for ci in range(nc):
        pr = slice(ci * GDN_HEADS, (ci + 1) * GDN_HEADS)
        u = w_v[pr] - mm(w_k[pr], state)
        o = mm(q_g[pr], state) + mm(p_qk[pr], u)
        state = g_end[pr] * state + mm_tn(k_d[pr], u)
        for h in range(GDN_HEADS):
            lo = h * GDN_DV
            zh = z_ref[0, rows(ci), lo:lo + GDN_DV]
            o_ref[0, rows(ci), lo:lo + GDN_DV] = _rms(o[h], onorm) * (zh * _sigmoid(zh))
    s_scr[...] = state

    @pl.when(t == nt - 1)
    def _():
        sn_ref[0] = state


def _gdn(qkv3, z3, sm3, smt3, cst8, s0, cw, gpr, gpc, tl, c, exact_small):
    b, seq, _ = qkv3.shape
    nt = seq // tl
    full2 = lambda a: pl.BlockSpec(a.shape, lambda i, j: (0, 0))
    return pl.pallas_call(
        functools.partial(_gdn_kernel, tl=tl, c=c, exact_small=exact_small),
        grid=(b, nt),
        in_specs=[pl.BlockSpec((1, tl, CONV_CH), lambda i, j: (i, j, 0)),
                  pl.BlockSpec((1, tl, GDN_WIDTH), lambda i, j: (i, j, 0)),
                  pl.BlockSpec((1, tl, LANES), lambda i, j: (i, j, 0)),
                  pl.BlockSpec((1, 16, tl), lambda i, j: (i, 0, j)),
                  pl.BlockSpec((1, SUBLANES, CONV_CH), lambda i, j: (i, 0, 0)),
                  pl.BlockSpec((1, GDN_HEADS, GDN_DK, GDN_DV), lambda i, j: (i, 0, 0, 0)),
                  full2(cw), full2(gpr), full2(gpc)],
        out_specs=[pl.BlockSpec((1, tl, GDN_WIDTH), lambda i, j: (i, j, 0)),
                   pl.BlockSpec((1, GDN_HEADS, GDN_DK, GDN_DV), lambda i, j: (i, 0, 0, 0)),
                   pl.BlockSpec((1, CONV_W - 1, CONV_CH), lambda i, j: (i, 0, 0))],
        out_shape=[jax.ShapeDtypeStruct((b, seq, GDN_WIDTH), F32),
                   jax.ShapeDtypeStruct((b, GDN_HEADS, GDN_DK, GDN_DV), F32),
                   jax.ShapeDtypeStruct((b, CONV_W - 1, CONV_CH), F32)],
        scratch_shapes=[pltpu.VMEM((GDN_HEADS, GDN_DK, GDN_DV), F32),
                        pltpu.VMEM((SUBLANES, CONV_CH), F32)],
        compiler_params=_cparams(("parallel", "arbitrary")),
    )(qkv3, z3, sm3, smt3, cst8, s0, cw, gpr, gpc)


FOX_HEAD_GROUP = 4


def _fox_prompt_kernel(fq_ref, fkt_ref, fvt_ref, ct_ref, on_ref, o_ref, kb_scr, vb_scr, *, tq):
    qi = pl.program_id(1)
    nk = kb_scr.shape[0]

    @pl.when(qi == 0)
    def _():
        for j in range(nk):
            kb_scr[j] = fkt_ref[0, :, j * tq:(j + 1) * tq].astype(BF16)
            vb_scr[j] = fvt_ref[0, :, j * tq:(j + 1) * tq].astype(BF16)

    q_all = fq_ref[0] * (FOX_HEAD_DIM ** -0.5)
    onorm = on_ref[...]
    causal = _iota2((tq, tq), 1) <= _iota2((tq, tq), 0)
    outs = []
    for g0 in range(0, FOX_HEADS, FOX_HEAD_GROUP):
        heads = range(g0, g0 + FOX_HEAD_GROUP)
        qs = [q_all[:, h * FOX_HEAD_DIM:(h + 1) * FOX_HEAD_DIM].astype(BF16) for h in heads]

        def step(j, carry, masked, heads=heads, qs=qs):
            new = []
            for (m, l, acc), h, qh in zip(carry, heads, qs):
                kt = kb_scr[j, h * FOX_HEAD_DIM:(h + 1) * FOX_HEAD_DIM, :]
                vt = vb_scr[j, h * FOX_HEAD_DIM:(h + 1) * FOX_HEAD_DIM, :]
                s = jnp.dot(qh, kt, preferred_element_type=F32) - ct_ref[0, h, pl.ds(j, 1), :]
                if masked:
                    s = jnp.where(causal, s, NEG_BIG)
                m_new = jnp.maximum(m, jnp.max(s, axis=-1, keepdims=True))
                p = jnp.exp(s - m_new)
                alpha = jnp.exp(m - m_new)
                l = alpha * l + jnp.sum(p, axis=-1, keepdims=True)
                acc = alpha * acc + lax.dot_general(p.astype(BF16), vt, (((1,), (1,)), ((), ())),
                                                    preferred_element_type=F32)
                new.append((m_new, l, acc))
            return tuple(new)

        init = tuple((jnp.full((tq, 1), NEG_BIG, F32), jnp.zeros((tq, 1), F32),
                      jnp.zeros((tq, FOX_HEAD_DIM), F32)) for _ in heads)
        carry = lax.fori_loop(0, qi, functools.partial(step, masked=False), init)
        for m, l, acc in step(qi, carry, True):
            outs.append(_rms(acc / l, onorm))
    o_ref[0] = jnp.concatenate(outs, axis=-1)


def _fox_prompt(fq3, fkt, fvt, ct4, onorm, tq):
    b, seq, _ = fq3.shape
    nq = seq // tq
    return pl.pallas_call(
        functools.partial(_fox_prompt_kernel, tq=tq),
        grid=(b, nq),
        in_specs=[pl.BlockSpec((1, tq, FOX_WIDTH), lambda i, j: (i, j, 0)),
                  pl.BlockSpec((1, FOX_WIDTH, seq), lambda i, j: (i, 0, 0)),
                  pl.BlockSpec((1, FOX_WIDTH, seq), lambda i, j: (i, 0, 0)),
                  pl.BlockSpec((1, FOX_HEADS, nq, tq), lambda i, j: (i, 0, 0, 0)),
                  pl.BlockSpec(onorm.shape, lambda i, j: (0, 0))],
        out_specs=pl.BlockSpec((1, tq, FOX_WIDTH), lambda i, j: (i, j, 0)),
        out_shape=jax.ShapeDtypeStruct((b, seq, FOX_WIDTH), F32),
        scratch_shapes=[pltpu.VMEM((nq, FOX_WIDTH, tq), BF16), pltpu.VMEM((nq, FOX_WIDTH, tq), BF16)],
        compiler_params=_cparams(("parallel", "arbitrary")),
    )(fq3, fkt, fvt, ct4, onorm)


def _logf_pages_kernel(lf_ref, w_ref, o_ref):
    lf = lf_ref[...]
    a = lf.astype(BF16)
    r1 = lf - a.astype(F32)
    b = r1.astype(BF16)
    c = (r1 - b.astype(F32)).astype(BF16)
    w = w_ref[...]
    o_ref[...] = (jnp.dot(a, w, preferred_element_type=F32) + jnp.dot(b, w, preferred_element_type=F32)
                  + jnp.dot(c, w, preferred_element_type=F32))


def _logf_pages(lf_flat, w2):
    n_pool, width = lf_flat.shape
    tp = next(c for c in (2048, 1024, 512, 256, 128, 64, 32, 16, 8) if n_pool % c == 0)
    return pl.pallas_call(
        _logf_pages_kernel,
        grid=(n_pool // tp,),
        in_specs=[pl.BlockSpec((tp, width), lambda i: (i, 0)),
                  pl.BlockSpec(w2.shape, lambda i: (0, 0))],
        out_specs=pl.BlockSpec((tp, 2 * width), lambda i: (i, 0)),
        out_shape=jax.ShapeDtypeStruct((n_pool, 2 * width), F32),
        compiler_params=_cparams(("parallel",)),
    )(lf_flat, w2)


def _fox_sample_kernel(pt_ref, qbd_ref, kn_ref, vn_ref, cn_ref, on_ref, *rest, pps, lq):
    k_refs = rest[0:pps]
    v_refs = rest[pps:2 * pps]
    r_refs = rest[2 * pps:3 * pps]
    o_ref = rest[3 * pps]
    m_scr, l_scr, acc_scr, suf_scr = rest[3 * pps + 1:]
    step = pl.program_id(1)
    nstep = pl.num_programs(1)
    rows = lq * FOX_HEADS
    qbd = qbd_ref[0]
    tile_rows = lambda v: jnp.concatenate([v] * lq, axis=0)

    @pl.when(step == 0)
    def _():
        s = lax.dot_general(qbd, kn_ref[0].astype(BF16), (((1,), (1,)), ((), ())),
                            preferred_element_type=F32)
        s = s - tile_rows(cn_ref[0])
        s = jnp.where(_iota2((rows, lq), 1) <= _iota2((rows, lq), 0) // FOX_HEADS, s, NEG_BIG)
        m = jnp.max(s, axis=-1, keepdims=True)
        p = jnp.exp(s - m)
        m_scr[...] = m
        l_scr[...] = jnp.sum(p, axis=-1, keepdims=True)
        acc_scr[...] = jnp.dot(p.astype(BF16), vn_ref[0].astype(BF16), preferred_element_type=F32)
        suf_scr[...] = jnp.zeros_like(suf_scr)

    page = k_refs[0].shape[3]
    suf = suf_scr[...]
    scores = []
    for i in range(pps):
        r2 = r_refs[i][0]
        s = jnp.dot(qbd, k_refs[i][0, 0].astype(BF16), preferred_element_type=F32)
        scores.append(s + tile_rows(r2[:, :page] + suf))
        suf = suf + r2[:, page:]
    suf_scr[...] = suf
    m = m_scr[...]
    m_new = m
    for s in scores:
        m_new = jnp.maximum(m_new, jnp.max(s, axis=-1, keepdims=True))
    alpha = jnp.exp(m - m_new)
    l = alpha * l_scr[...]
    acc = alpha * acc_scr[...]
    for i in range(pps):
        p = jnp.exp(scores[i] - m_new)
        l = l + jnp.sum(p, axis=-1, keepdims=True)
        acc = acc + lax.dot_general(p.astype(BF16), v_refs[i][0, 0].astype(BF16), (((1,), (1,)), ((), ())),
                                    preferred_element_type=F32)
    m_scr[...] = m_new
    l_scr[...] = l
    acc_scr[...] = acc

    @pl.when(step == nstep - 1)
    def _():
        own = _iota2((rows, FOX_WIDTH), 1) // FOX_HEAD_DIM == _iota2((rows, FOX_WIDTH), 0) % FOX_HEADS
        o = jnp.where(own, acc / l, 0.0)
        ms = jnp.sum(o * o, axis=-1, keepdims=True) * (1.0 / FOX_HEAD_DIM)
        o = o * lax.rsqrt(ms + NORM_EPS) * on_ref[...]
        o_ref[0] = jnp.sum(o.reshape(lq, FOX_HEADS, FOX_WIDTH), axis=1)


def _fox_sample(page_table, layer, qbd, kn3, vn3, cn3, onorm, ckt, cvt, r3, pps):
    b, lq, _ = kn3.shape
    n_pages = page_table.shape[1]
    page = ckt.shape[3]
    nstep = n_pages // pps
    rows = FOX_HEADS * lq

    def page_map(i):
        return lambda bi, s, pt: (layer, pt[bi, n_pages - 1 - (s * pps + i)], 0, 0)

    def r_map(i):
        return lambda bi, s, pt: (pt[bi, n_pages - 1 - (s * pps + i)], 0, 0)

    in_specs = [pl.BlockSpec((1, rows, FOX_WIDTH), lambda bi, s, pt: (bi, 0, 0)),
                pl.BlockSpec((1, lq, FOX_WIDTH), lambda bi, s, pt: (bi, 0, 0)),
                pl.BlockSpec((1, lq, FOX_WIDTH), lambda bi, s, pt: (bi, 0, 0)),
                pl.BlockSpec((1, FOX_HEADS, lq), lambda bi, s, pt: (bi, 0, 0)),
                pl.BlockSpec(onorm.shape, lambda bi, s, pt: (0, 0))]
    in_specs += [pl.BlockSpec((1, 1, FOX_WIDTH, page), page_map(i)) for i in range(pps)]
    in_specs += [pl.BlockSpec((1, 1, FOX_WIDTH, page), page_map(i)) for i in range(pps)]
    in_specs += [pl.BlockSpec((1, FOX_HEADS, 2 * page), r_map(i)) for i in range(pps)]
    grid_spec = pltpu.PrefetchScalarGridSpec(
        num_scalar_prefetch=1,
        grid=(b, nstep),
        in_specs=in_specs,
        out_specs=pl.BlockSpec((1, lq, FOX_WIDTH), lambda bi, s, pt: (bi, 0, 0)),
        scratch_shapes=[pltpu.VMEM((rows, 1), F32), pltpu.VMEM((rows, 1), F32),
                        pltpu.VMEM((rows, FOX_WIDTH), F32), pltpu.VMEM((FOX_HEADS, page), F32)],
    )
    return pl.pallas_call(
        functools.partial(_fox_sample_kernel, pps=pps, lq=lq),
        grid_spec=grid_spec,
        out_shape=jax.ShapeDtypeStruct((b, lq, FOX_WIDTH), F32),
        compiler_params=_cparams(("parallel", "arbitrary")),
    )(page_table, qbd, kn3, vn3, cn3, onorm, *([ckt] * pps), *([cvt] * pps), *([r3] * pps))


def _out_proj_kernel(x_ref, a_ref, b_ref, w_ref, o_ref):
    o_ref[...] = (x_ref[...]
                  + jnp.dot(a_ref[...].astype(BF16), w_ref[0:GDN_WIDTH, :], preferred_element_type=F32)
                  + jnp.dot(b_ref[...].astype(BF16), w_ref[GDN_WIDTH:, :], preferred_element_type=F32))


def _out_proj(x2, a2, b2, w, tm):
    t = x2.shape[0]
    return pl.pallas_call(
        _out_proj_kernel,
        grid=(t // tm,),
        in_specs=[pl.BlockSpec((tm, D_MODEL), lambda i: (i, 0)),
                  pl.BlockSpec((tm, GDN_WIDTH), lambda i: (i, 0)),
                  pl.BlockSpec((tm, FOX_WIDTH), lambda i: (i, 0)),
                  pl.BlockSpec(w.shape, lambda i: (0, 0))],
        out_specs=pl.BlockSpec((tm, D_MODEL), lambda i: (i, 0)),
        out_shape=jax.ShapeDtypeStruct((t, D_MODEL), F32),
        compiler_params=_cparams(("parallel",)),
    )(x2, a2, b2, w)


def _xattn_kernel(x_ref, g_ref, wq_ref, wo_ref, mk_ref, mv_ref, o_ref):
    x = x_ref[0]
    hb = _rms(x, g_ref[...]).astype(BF16)
    q = jnp.dot(hb, wq_ref[...], preferred_element_type=F32) * (XA_HEAD_DIM ** -0.5)
    mk = mk_ref[0].astype(BF16)
    mv = mv_ref[0].astype(BF16)
    outs = []
    for h in range(XA_HEADS):
        lo = h * XA_HEAD_DIM
        s = lax.dot_general(q[:, lo:lo + XA_HEAD_DIM].astype(BF16), mk[:, lo:lo + XA_HEAD_DIM],
                            (((1,), (1,)), ((), ())), preferred_element_type=F32)
        p = jnp.exp(s - jnp.max(s, axis=-1, keepdims=True))
        p = p / jnp.sum(p, axis=-1, keepdims=True)
        outs.append(jnp.dot(p.astype(BF16), mv[:, lo:lo + XA_HEAD_DIM], preferred_element_type=F32))
    o = jnp.concatenate(outs, axis=-1).astype(BF16)
    o_ref[0] = x + jnp.dot(o, wo_ref[...], preferred_element_type=F32)


def _xattn(x3, g, wq, wo, mk3, mv3, tq):
    b, seq, _ = x3.shape
    n_mem = mk3.shape[1]
    return pl.pallas_call(
        _xattn_kernel,
        grid=(b, seq // tq),
        in_specs=[pl.BlockSpec((1, tq, D_MODEL), lambda i, j: (i, j, 0)),
                  pl.BlockSpec(g.shape, lambda i, j: (0, 0)),
                  pl.BlockSpec(wq.shape, lambda i, j: (0, 0)),
                  pl.BlockSpec(wo.shape, lambda i, j: (0, 0)),
                  pl.BlockSpec((1, n_mem, XA_WIDTH), lambda i, j: (i, 0, 0)),
                  pl.BlockSpec((1, n_mem, XA_WIDTH), lambda i, j: (i, 0, 0))],
        out_specs=pl.BlockSpec((1, tq, D_MODEL), lambda i, j: (i, j, 0)),
        out_shape=jax.ShapeDtypeStruct((b, seq, D_MODEL), F32),
        compiler_params=_cparams(("parallel", "parallel")),
    )(x3, g, wq, wo, mk3, mv3)


def _router_kernel(x_ref, g_ref, wr_ref, br_ref, h_ref, route_ref, gate_ref, cnt_ref, base_scr, *, tm):
    i = pl.program_id(0)

    @pl.when(i == 0)
    def _():
        base_scr[...] = jnp.zeros_like(base_scr)

    h = _rms(x_ref[...], g_ref[...])
    _store_row_tiles(h_ref, h, tm)
    logits = _dot_hi(h, wr_ref[...]) + br_ref[...]
    lane = _iota2((tm, LANES), 1)
    lane_f = lane.astype(F32)
    vals, hots, idxs = [], [], []
    cur = logits
    for _ in range(TOP_K):
        mx = jnp.max(cur, axis=-1, keepdims=True)
        idx_f = jnp.min(jnp.where(cur == mx, lane_f, float(LANES)), axis=-1, keepdims=True)
        hot = lane_f == idx_f
        vals.append(mx)
        idxs.append(idx_f.astype(I32))
        hots.append(hot)
        cur = jnp.where(hot, -jnp.inf, cur)
    exps = [jnp.exp(v - vals[0]) for v in vals]
    denom = exps[0] + exps[1] + exps[2] + exps[3]
    member = (hots[0] | hots[1] | hots[2] | hots[3]).astype(F32)
    strict = (_iota2((tm, tm), 1) < _iota2((tm, tm), 0)).astype(BF16)
    before = jnp.dot(strict, member.astype(BF16), preferred_element_type=F32) + base_scr[...]
    route = jnp.zeros((tm, LANES), I32)
    gate = jnp.zeros((tm, LANES), F32)
    for k in range(TOP_K):
        rank = jnp.sum(jnp.where(hots[k], before, 0.0), axis=-1, keepdims=True).astype(I32)
        route = jnp.where(lane == k, idxs[k], route)
        route = jnp.where(lane == TOP_K + k, rank, route)
        gate = jnp.where(lane == k, exps[k] / denom, gate)
    route_ref[...] = route
    gate_ref[...] = gate
    base_scr[...] = base_scr[...] + jnp.sum(member, axis=0, keepdims=True)
    cnt_ref[...] = base_scr[...]


def _router(x2, g, wr, br, tm):
    t = x2.shape[0]
    return pl.pallas_call(
        functools.partial(_router_kernel, tm=tm),
        grid=(t // tm,),
        in_specs=[pl.BlockSpec((tm, D_MODEL), lambda i: (i, 0)),
                  pl.BlockSpec(g.shape, lambda i: (0, 0)),
                  pl.BlockSpec(wr.shape, lambda i: (0, 0)),
                  pl.BlockSpec(br.shape, lambda i: (0, 0))],
        out_specs=[pl.BlockSpec((tm * ROW_TILE, LANES), lambda i: (i, 0)),
                   pl.BlockSpec((tm, LANES), lambda i: (i, 0)),
                   pl.BlockSpec((tm, LANES), lambda i: (i, 0)),
                   pl.BlockSpec((1, LANES), lambda i: (0, 0))],
        out_shape=[jax.ShapeDtypeStruct((t * ROW_TILE, LANES), F32), jax.ShapeDtypeStruct((t, LANES), I32),
                   jax.ShapeDtypeStruct((t, LANES), F32), jax.ShapeDtypeStruct((1, LANES), F32)],
        scratch_shapes=[pltpu.VMEM((1, LANES), F32)],
        compiler_params=_cparams(("arbitrary",)),
    )(x2, g, wr, br)


def _dispatch_kernel(dest_ref, h_ref, init_hbm, xs_hbm, sem, *, tm):
    del init_hbm
    def issue(t, c):
        src = pl.multiple_of(t * ROW_TILE, ROW_TILE)
        for k in range(TOP_K):
            dst = pl.multiple_of(dest_ref[0, 0, t * TOP_K + k] * ROW_TILE, ROW_TILE)
            pltpu.make_async_copy(h_ref.at[pl.ds(src, ROW_TILE)], xs_hbm.at[pl.ds(dst, ROW_TILE)], sem).start()
        return c

    lax.fori_loop(0, tm, issue, 0, unroll=2)
    for _ in range(TOP_K):
        pltpu.make_async_copy(h_ref, xs_hbm.at[pl.ds(0, tm * ROW_TILE)], sem).wait()


def _dispatch(dest3, h2, rows, tm):
    t = h2.shape[0] // ROW_TILE
    init = jnp.zeros((rows * ROW_TILE, LANES), F32)
    return pl.pallas_call(
        functools.partial(_dispatch_kernel, tm=tm),
        grid=(t // tm,),
        in_specs=[pl.BlockSpec((1, 1, tm * TOP_K), lambda i: (i, 0, 0), memory_space=pltpu.SMEM),
                  pl.BlockSpec((tm * ROW_TILE, LANES), lambda i: (i, 0)),
                  pl.BlockSpec(memory_space=pl.ANY)],
        out_specs=pl.BlockSpec(memory_space=pl.ANY),
        out_shape=jax.ShapeDtypeStruct((rows * ROW_TILE, LANES), F32),
        scratch_shapes=[pltpu.SemaphoreType.DMA(())],
        input_output_aliases={2: 0},
        compiler_params=pltpu.CompilerParams(dimension_semantics=("arbitrary",), vmem_limit_bytes=VMEM_LIMIT,
                                             has_side_effects=True),
    )(dest3, h2, init)


def _ffn_kernel(be_ref, nv_ref, x_ref, w1_ref, b1_ref, w2_ref, b2_ref, o_ref, *, tb):
    i = pl.program_id(0)
    live = i * tb < nv_ref[0]

    @pl.when(live)
    def _():
        x = _load_row_tiles(x_ref, tb).astype(BF16)
        hb = jnp.dot(x, w1_ref[0], preferred_element_type=F32) + b1_ref[0]
        glu = jnp.minimum(hb[:, :D_FF], SWIGLU_LIMIT)
        lin = jnp.clip(hb[:, D_FF:], -SWIGLU_LIMIT, SWIGLU_LIMIT)
        act = glu * _sigmoid(SWIGLU_ALPHA * glu) * (lin + 1.0)
        y = jnp.dot(act.astype(BF16), w2_ref[0], preferred_element_type=F32) + b2_ref[0]
        _store_row_tiles(o_ref, y, tb)

    @pl.when(jnp.logical_not(live))
    def _():
        o_ref[...] = jnp.zeros_like(o_ref)


def _ffn(block_e, nvalid, xs, w1, b1, w2, b2, tb):
    rows = xs.shape[0] // ROW_TILE
    nb = rows // tb
    blk = pl.BlockSpec((tb * ROW_TILE, LANES), lambda i, be, nv: (i, 0))
    grid_spec = pltpu.PrefetchScalarGridSpec(
        num_scalar_prefetch=2,
        grid=(nb,),
        in_specs=[blk,
                  pl.BlockSpec((1, D_MODEL, 2 * D_FF), lambda i, be, nv: (be[i], 0, 0)),
                  pl.BlockSpec((1, 1, 2 * D_FF), lambda i, be, nv: (be[i], 0, 0)),
                  pl.BlockSpec((1, D_FF, D_MODEL), lambda i, be, nv: (be[i], 0, 0)),
                  pl.BlockSpec((1, 1, D_MODEL), lambda i, be, nv: (be[i], 0, 0))],
        out_specs=blk,
    )
    return pl.pallas_call(
        functools.partial(_ffn_kernel, tb=tb),
        grid_spec=grid_spec,
        out_shape=jax.ShapeDtypeStruct((rows * ROW_TILE, LANES), F32),
        compiler_params=_cparams(("arbitrary",)),
    )(block_e, nvalid, xs, w1, b1, w2, b2)


def _combine_kernel(dcur_ref, dnext_ref, x_ref, gate_ref, fg_ref, os_hbm, y_ref, yn_ref, buf, sem, *, tm):
    i = pl.program_id(0)
    n = pl.num_programs(0)

    def issue_tile(dref, slot):
        def issue(t, c):
            dst = pl.multiple_of(t * ROW_TILE, ROW_TILE)
            for k in range(TOP_K):
                src = pl.multiple_of(dref[0, 0, t * TOP_K + k] * ROW_TILE, ROW_TILE)
                pltpu.make_async_copy(os_hbm.at[pl.ds(src, ROW_TILE)],
                                      buf.at[slot, k, pl.ds(dst, ROW_TILE)], sem.at[slot]).start()
            return c

        lax.fori_loop(0, tm, issue, 0, unroll=2)

    @pl.when(i == 0)
    def _():
        issue_tile(dcur_ref, 0)

    @pl.when(i + 1 < n)
    def _():
        issue_tile(dnext_ref, (i + 1) % 2)

    slot = i % 2
    for k in range(TOP_K):
        pltpu.make_async_copy(os_hbm.at[pl.ds(0, tm * ROW_TILE)], buf.at[slot, k], sem.at[slot]).wait()
    gate = gate_ref[...]
    x = x_ref[...]
    pieces = []
    for s in range(ROW_TILE):
        acc = x[:, s * LANES:(s + 1) * LANES]
        for k in range(TOP_K):
            acc = acc + gate[:, k:k + 1] * buf[slot, k, pl.ds(s, tm, stride=ROW_TILE), :]
        pieces.append(acc)
    y = jnp.concatenate(pieces, axis=-1)
    y_ref[...] = y
    yn_ref[...] = _rms(y, fg_ref[...])


def _combine(dest3, x2, gate, fg, os2, tm):
    t = x2.shape[0]
    nt = t // tm
    return pl.pallas_call(
        functools.partial(_combine_kernel, tm=tm),
        grid=(nt,),
        in_specs=[pl.BlockSpec((1, 1, tm * TOP_K), lambda i: (i, 0, 0), memory_space=pltpu.SMEM),
                  pl.BlockSpec((1, 1, tm * TOP_K), lambda i: (jnp.minimum(i + 1, nt - 1), 0, 0),
                               memory_space=pltpu.SMEM),
                  pl.BlockSpec((tm, D_MODEL), lambda i: (i, 0)),
                  pl.BlockSpec((tm, LANES), lambda i: (i, 0)),
                  pl.BlockSpec(fg.shape, lambda i: (0, 0)),
                  pl.BlockSpec(memory_space=pl.ANY)],
        out_specs=[pl.BlockSpec((tm, D_MODEL), lambda i: (i, 0)),
                   pl.BlockSpec((tm, D_MODEL), lambda i: (i, 0))],
        out_shape=[jax.ShapeDtypeStruct((t, D_MODEL), F32), jax.ShapeDtypeStruct((t, D_MODEL), F32)],
        scratch_shapes=[pltpu.VMEM((2, TOP_K, tm * ROW_TILE, LANES), F32), pltpu.SemaphoreType.DMA((2,))],
        compiler_params=_cparams(("arbitrary",)),
    )(dest3, dest3, x2, gate, fg, os2)


def _row_tile(t, want):
    tm = min(want, t)
    assert t % tm == 0
    return tm


def _mixer(x3, lw, conv_state8, s0, fox_fn, kv_t):
    b, seq, _ = x3.shape
    t = b * seq
    x2 = x3.reshape(t, D_MODEL)
    qkv, z, fq, fk, fv, sm, smt = _in_proj(x2, lw["norm1_g"], lw["w_main"], lw["w_kv_t"], lw["w_small"],
                                           lw["w_small_t"], _row_tile(t, 256), seq if kv_t else None)
    sm3 = sm.reshape(b, seq, LANES)
    smt3 = jnp.transpose(smt.reshape(16, b, seq), (1, 0, 2))
    logf_t, ct = _logf(smt3, lw["fb_col"])
    logf = jnp.transpose(logf_t, (0, 2, 1))
    c = min(GDN_CHUNK, seq)
    tl = min(256, seq)
    gdn_out, s_new, cbuf = _gdn(qkv.reshape(b, seq, CONV_CH), z.reshape(b, seq, GDN_WIDTH), sm3, smt3,
                                conv_state8, s0, lw["conv_w"], lw["gp_row"], lw["gp_col"], tl, c,
                                exact_small=(c <= SUBLANES))
    fq3 = fq.reshape(b, seq, FOX_WIDTH)
    if kv_t:
        fk3, fv3 = fk, fv
        as_out = lambda a: jnp.transpose(a.reshape(b, FOX_HEADS, FOX_HEAD_DIM, seq), (0, 3, 1, 2))
    else:
        fk3, fv3 = fk.reshape(b, seq, FOX_WIDTH), fv.reshape(b, seq, FOX_WIDTH)
        as_out = lambda a: a.reshape(b, seq, FOX_HEADS, FOX_HEAD_DIM)
    fox_out = fox_fn(fq3, fk3, fv3, ct)
    y2 = _out_proj(x2, gdn_out.reshape(t, GDN_WIDTH), fox_out.reshape(t, FOX_WIDTH), lw["w_out"],
                   _row_tile(t, 512))
    return y2.reshape(b, seq, D_MODEL), s_new, cbuf, as_out(fk3), as_out(fv3), logf


def _moe(x3, lw, final_g):
    b, seq, _ = x3.shape
    t = b * seq
    x2 = x3.reshape(t, D_MODEL)
    h2, route, gate, counts = _router(x2, lw["norm3_g"], lw["w_router"], lw["b_router"], _row_tile(t, 256))
    tb = EXPERT_BLOCK_LARGE if t * TOP_K >= N_EXPERTS * 4 * EXPERT_BLOCK_LARGE else EXPERT_BLOCK
    cnt = counts[0, :N_EXPERTS].astype(I32)
    padded = (cnt + tb - 1) // tb * tb
    pends = jnp.cumsum(padded)
    pstart = pends - padded
    nb = t * TOP_K // tb + N_EXPERTS
    rows = nb * tb
    block_row0 = jnp.arange(nb, dtype=I32) * tb
    block_e = jnp.minimum(jnp.sum((pends[None, :] <= block_row0[:, None]).astype(I32), axis=1), N_EXPERTS - 1)
    nvalid = pends[-1:].astype(I32)
    hot = route[:, :TOP_K, None] == jnp.arange(N_EXPERTS, dtype=I32)[None, None, :]
    dest = jnp.sum(jnp.where(hot, pstart[None, None, :], 0), axis=-1) + route[:, TOP_K:2 * TOP_K]
    tmd = _row_tile(t, 256)
    xs = _dispatch(dest.reshape(t // tmd, 1, tmd * TOP_K), h2, rows, tmd)
    os2 = _ffn(block_e, nvalid, xs, lw["w1"], lw["b1"], lw["w2"], lw["b2"], tb)
    tmc = _row_tile(t, 256)
    y2, yn2 = _combine(dest.reshape(t // tmc, 1, tmc * TOP_K), x2, gate, final_g, os2, tmc)
    return y2.reshape(b, seq, D_MODEL), yn2.reshape(b, seq, D_MODEL)


def _prep_layer(l, norm1_g, w_in, conv_w, gdn_a_log, gdn_dt_bias, gdn_onorm, fox_fbias, fox_onorm, w_out,
                norm2_g, mem_norm_g, w_xq, w_mkv, w_xo, norm3_g, w_router, b_router, w1, b1, w2, b2):
    wi = w_in[l]
    gq, gk, gv, gz, gb, ga, fq, fk, fv, ff = _split_in(wi)
    w_main = jnp.concatenate([gq, gk, gv, gz, fq, fk, fv], axis=1).astype(BF16)
    w_kv_t = jnp.transpose(jnp.concatenate([fk, fv], axis=1)).astype(BF16)
    w_small = jnp.concatenate([gb, ga, ff, jnp.zeros((D_MODEL, LANES - 16), F32)], axis=1).astype(BF16)
    w_small_t = jnp.transpose(w_small[:, :16])
    lanes = lambda v, off: jnp.zeros((LANES,), F32).at[off:off + v.shape[0]].set(v)
    gp_row = jnp.zeros((SUBLANES, LANES), F32)
    gp_row = gp_row.at[0].set(lanes(gdn_a_log[l], GDN_HEADS)).at[1].set(lanes(gdn_dt_bias[l], GDN_HEADS))
    gp_row = gp_row.at[2].set(gdn_onorm[l])
    gp_col = jnp.zeros((16, LANES), F32)
    gp_col = gp_col.at[GDN_HEADS:2 * GDN_HEADS, 0].set(gdn_a_log[l]).at[GDN_HEADS:2 * GDN_HEADS, 1].set(gdn_dt_bias[l])
    fb_col = jnp.zeros((16, LANES), F32).at[8:16, 0].set(fox_fbias[l])
    wr = jnp.concatenate([w_router[l], jnp.zeros((D_MODEL, LANES - N_EXPERTS), F32)], axis=1)
    br = jnp.full((1, LANES), NEG_BIG, F32).at[0, :N_EXPERTS].set(b_router[l])
    return {
        "norm1_g": norm1_g[l].reshape(1, D_MODEL), "w_main": w_main, "w_kv_t": w_kv_t,
        "w_small": w_small, "w_small_t": w_small_t,
        "conv_w": conv_w[l], "gp_row": gp_row, "gp_col": gp_col, "fb_col": fb_col,
        "fox_onorm": fox_onorm[l].reshape(1, FOX_HEAD_DIM), "w_out": w_out[l].astype(BF16),
        "norm2_g": norm2_g[l].reshape(1, D_MODEL), "mem_norm_g": mem_norm_g[l].reshape(1, D_MODEL),
        "w_xq": w_xq[l].astype(BF16), "w_mkv": w_mkv[l].astype(BF16), "w_xo": w_xo[l].astype(BF16),
        "norm3_g": norm3_g[l].reshape(1, D_MODEL), "w_router": wr, "b_router": br,
        "w1": w1[l].astype(BF16), "b1": b1[l].reshape(N_EXPERTS, 1, 2 * D_FF),
        "w2": w2[l].astype(BF16), "b2": b2[l].reshape(N_EXPERTS, 1, D_MODEL),
    }


def _split_in(wi):
    widths = (GDN_WIDTH, GDN_WIDTH, GDN_WIDTH, GDN_WIDTH, GDN_HEADS, GDN_HEADS,
              FOX_WIDTH, FOX_WIDTH, FOX_WIDTH, FOX_HEADS)
    outs, c0 = [], 0
    for w in widths:
        outs.append(wi[:, c0:c0 + w])
        c0 += w
    return outs


def kernel(x_prompt, x_sample, mem_prompt, cache_fox_k, cache_fox_v, cache_fox_logf, cache_mem_k, cache_mem_v, state_gdn, state_conv, page_table, norm1_g, w_in, conv_w, gdn_a_log, gdn_dt_bias, gdn_onorm, fox_fbias, fox_onorm, w_out, norm2_g, mem_norm_g, w_xq, w_mkv, w_xo, norm3_g, w_router, b_router, w1, b1, w2, b2, final_norm_g):
    depth = w_in.shape[0]
    bp, lp, _ = x_prompt.shape
    bs, ls, _ = x_sample.shape
    n_mem = mem_prompt.shape[1]
    n_pool, page = cache_fox_k.shape[1], cache_fox_k.shape[2]
    ckt = jnp.transpose(cache_fox_k, (0, 1, 3, 4, 2)).reshape(depth, n_pool, FOX_WIDTH, page)
    cvt = jnp.transpose(cache_fox_v, (0, 1, 3, 4, 2)).reshape(depth, n_pool, FOX_WIDTH, page)
    lf_rows = jnp.transpose(cache_fox_logf, (0, 1, 3, 2)).reshape(depth, n_pool * FOX_HEADS, page)
    tok = jnp.arange(page, dtype=I32)
    later = tok[:, None] > tok[None, :]
    w_suffix = jnp.concatenate([later, jnp.ones_like(later)], axis=1).astype(BF16)
    head_mask = (jnp.arange(FOX_HEADS)[:, None] == jnp.arange(FOX_HEADS)[None, :]).astype(F32)
    final_g = final_norm_g.reshape(1, D_MODEL)
    xp, xs = x_prompt, x_sample
    yp = ys = None
    outs = {k: [] for k in ("fkp", "fvp", "flp", "mkp", "mvp", "sgp", "scp", "fks", "fvs", "fls", "sgs", "scs")}
    for l in range(depth):
        lw = _prep_layer(l, norm1_g, w_in, conv_w, gdn_a_log, gdn_dt_bias, gdn_onorm, fox_fbias, fox_onorm,
                         w_out, norm2_g, mem_norm_g, w_xq, w_mkv, w_xo, norm3_g, w_router, b_router,
                         w1, b1, w2, b2)
        tq = min(256, lp)

        def fox_p(fq3, fk3, fv3, ct, lw=lw, tq=tq):
            return _fox_prompt(fq3, fk3, fv3, ct.reshape(bp, FOX_HEADS, lp // tq, tq), lw["fox_onorm"], tq)

        xp, s_new, cbuf, k_new, v_new, lf_new = _mixer(
            xp, lw, jnp.zeros((bp, SUBLANES, CONV_CH), F32),
            jnp.zeros((bp, GDN_HEADS, GDN_DK, GDN_DV), F32), fox_p, kv_t=True)
        outs["fkp"].append(k_new); outs["fvp"].append(v_new); outs["flp"].append(lf_new)
        outs["sgp"].append(s_new); outs["scp"].append(cbuf)

        r3 = _logf_pages(lf_rows[l], w_suffix).reshape(n_pool, FOX_HEADS, 2 * page)
        on_tiled = jnp.tile(lw["fox_onorm"], (1, FOX_HEADS))

        def fox_s(fq3, fk3, fv3, ct, lw=lw, l=l, r3=r3, on_tiled=on_tiled):
            q4 = fq3.reshape(bs, ls, FOX_HEADS, FOX_HEAD_DIM) * (FOX_HEAD_DIM ** -0.5)
            qbd = jnp.einsum("bqhd,hg->bqhgd", q4, head_mask).reshape(bs, ls * FOX_HEADS, FOX_WIDTH)
            return _fox_sample(page_table, l, qbd.astype(BF16), fk3, fv3, ct, on_tiled, ckt, cvt, r3,
                               pps=math.gcd(page_table.shape[1], 16))

        cst8 = jnp.pad(state_conv[l], ((0, 0), (SUBLANES - (CONV_W - 1), 0), (0, 0)))
        xs, s_new, cbuf, k_new, v_new, lf_new = _mixer(xs, lw, cst8, state_gdn[l], fox_s, kv_t=False)
        outs["fks"].append(k_new); outs["fvs"].append(v_new); outs["fls"].append(lf_new)
        outs["sgs"].append(s_new); outs["scs"].append(cbuf)

        mk2, mv2 = _norm_proj(mem_prompt.reshape(bp * n_mem, D_MODEL), lw["mem_norm_g"], lw["w_mkv"],
                              (XA_WIDTH, XA_WIDTH), _row_tile(bp * n_mem, 512))
        outs["mkp"].append(mk2.reshape(bp, n_mem, XA_HEADS, XA_HEAD_DIM))
        outs["mvp"].append(mv2.reshape(bp, n_mem, XA_HEADS, XA_HEAD_DIM))
        xp = _xattn(xp, lw["norm2_g"], lw["w_xq"], lw["w_xo"], mk2.reshape(bp, n_mem, XA_WIDTH),
                    mv2.reshape(bp, n_mem, XA_WIDTH), min(512, lp))
        xs = _xattn(xs, lw["norm2_g"], lw["w_xq"], lw["w_xo"], cache_mem_k[l].reshape(bs, n_mem, XA_WIDTH),
                    cache_mem_v[l].reshape(bs, n_mem, XA_WIDTH), ls)

        xp, yp = _moe(xp, lw, final_g)
        xs, ys = _moe(xs, lw, final_g)
    st = jnp.stack
    return (yp, ys, st(outs["fkp"]), st(outs["fvp"]), st(outs["flp"]), st(outs["mkp"]), st(outs["mvp"]),
            st(outs["sgp"]), st(outs["scp"]), st(outs["fks"]), st(outs["fvs"]), st(outs["fls"]),
            st(outs["sgs"]), st(outs["scs"]))
```

```python
import functools
import math

import jax
import jax.numpy as jnp
from jax import lax
from jax.experimental import pallas as pl
from jax.experimental.pallas import tpu as pltpu

F32 = jnp.float32
BF16 = jnp.bfloat16
I32 = jnp.int32
HIGHEST = lax.Precision.HIGHEST

D_MODEL = 1024
GDN_HEADS = 4
GDN_DK = 128
GDN_DV = 128
GDN_WIDTH = GDN_HEADS * GDN_DV
CONV_W = 4
CONV_CH = 3 * GDN_WIDTH
GDN_CHUNK = 64
FOX_HEADS = 8
FOX_HEAD_DIM = 64
FOX_WIDTH = FOX_HEADS * FOX_HEAD_DIM
XA_HEADS = 4
XA_HEAD_DIM = 128
XA_WIDTH = XA_HEADS * XA_HEAD_DIM
N_EXPERTS = 32
TOP_K = 4
D_FF = D_MODEL
SWIGLU_LIMIT = 7.0
SWIGLU_ALPHA = 1.702
NORM_EPS = 1e-6
NEG_BIG = -1e30

LANES = 128
SUBLANES = 8
VMEM_LIMIT = 52 * 1024 * 1024
MAIN_COLS = 2 * GDN_WIDTH + 2 * GDN_WIDTH + 3 * FOX_WIDTH
EXPERT_BLOCK = 256
EXPERT_BLOCK_LARGE = 512


def _cparams(sem):
    return pltpu.CompilerParams(dimension_semantics=sem, vmem_limit_bytes=VMEM_LIMIT)


def _dot(a, b):
    return jnp.dot(a.astype(BF16), b.astype(BF16), preferred_element_type=F32)


def _dot_nt(a, b):
    return lax.dot_general(a.astype(BF16), b.astype(BF16), (((1,), (1,)), ((), ())),
                           preferred_element_type=F32)


def _dot_tn(a, b):
    return lax.dot_general(a.astype(BF16), b.astype(BF16), (((0,), (0,)), ((), ())),
                           preferred_element_type=F32)


def _dot_hi(a, b):
    return jnp.dot(a, b, precision=HIGHEST, preferred_element_type=F32)


def _dot_nt_hi(a, b):
    return lax.dot_general(a, b, (((1,), (1,)), ((), ())), precision=HIGHEST,
                           preferred_element_type=F32)


def _rms(x, g):
    return x * lax.rsqrt(jnp.mean(x * x, axis=-1, keepdims=True) + NORM_EPS) * g


def _sigmoid(x):
    return 1.0 / (1.0 + jnp.exp(-x))


def _softplus(x):
    return jnp.maximum(x, 0.0) + jnp.log1p(jnp.exp(-jnp.abs(x)))


def _log_sigmoid(x):
    return jnp.minimum(x, 0.0) - jnp.log1p(jnp.exp(-jnp.abs(x)))


def _iota2(shape, dim):
    return lax.broadcasted_iota(I32, shape, dim)


ROW_TILE = D_MODEL // LANES


def _store_row_tiles(ref, val, n):
    for s in range(ROW_TILE):
        ref[pl.ds(s, n, stride=ROW_TILE), :] = val[:, s * LANES:(s + 1) * LANES]


def _load_row_tiles(ref, n):
    return jnp.concatenate([ref[pl.ds(s, n, stride=ROW_TILE), :] for s in range(ROW_TILE)], axis=-1)


def _in_proj_kernel(x_ref, g_ref, wm_ref, wkvt_ref, ws_ref, wst_ref,
                    qkv_ref, z_ref, fq_ref, fk_ref, fv_ref, sm_ref, smt_ref, *, kv_t):
    hb = _rms(x_ref[...], g_ref[...]).astype(BF16)
    nt = lambda w: lax.dot_general(w, hb, (((1,), (1,)), ((), ())), preferred_element_type=F32)
    c0 = 0
    for ref, width in ((qkv_ref, CONV_CH), (z_ref, GDN_WIDTH), (fq_ref, FOX_WIDTH)):
        ref[...] = jnp.dot(hb, wm_ref[:, c0:c0 + width], preferred_element_type=F32)
        c0 += width
    if kv_t:
        fk_ref[0] = nt(wkvt_ref[0:FOX_WIDTH, :])
        fv_ref[0] = nt(wkvt_ref[FOX_WIDTH:, :])
    else:
        fk_ref[...] = jnp.dot(hb, wm_ref[:, c0:c0 + FOX_WIDTH], preferred_element_type=F32)
        fv_ref[...] = jnp.dot(hb, wm_ref[:, c0 + FOX_WIDTH:], preferred_element_type=F32)
    sm_ref[...] = jnp.dot(hb, ws_ref[...], preferred_element_type=F32)
    smt_ref[...] = nt(wst_ref[...])


def _in_proj(x2, g, wm, wkvt, ws, wst, tm, kv_t_seq):
    t = x2.shape[0]
    row = lambda w: pl.BlockSpec((tm, w), lambda i: (i, 0))
    full = lambda a: pl.BlockSpec(a.shape, lambda i: (0,) * a.ndim)
    if kv_t_seq:
        per_b = kv_t_seq // tm
        kv_spec = pl.BlockSpec((1, FOX_WIDTH, tm), lambda i: (i // per_b, 0, i % per_b))
        kv_shape = jax.ShapeDtypeStruct((t // kv_t_seq, FOX_WIDTH, kv_t_seq), F32)
    else:
        kv_spec = row(FOX_WIDTH)
        kv_shape = jax.ShapeDtypeStruct((t, FOX_WIDTH), F32)
    return pl.pallas_call(
        functools.partial(_in_proj_kernel, kv_t=bool(kv_t_seq)),
        grid=(t // tm,),
        in_specs=[row(D_MODEL), full(g), full(wm), full(wkvt), full(ws), full(wst)],
        out_specs=[row(CONV_CH), row(GDN_WIDTH), row(FOX_WIDTH), kv_spec, kv_spec,
                   row(LANES), pl.BlockSpec((16, tm), lambda i: (0, i))],
        out_shape=[jax.ShapeDtypeStruct((t, CONV_CH), F32), jax.ShapeDtypeStruct((t, GDN_WIDTH), F32),
                   jax.ShapeDtypeStruct((t, FOX_WIDTH), F32), kv_shape, kv_shape,
                   jax.ShapeDtypeStruct((t, LANES), F32), jax.ShapeDtypeStruct((16, t), F32)],
        compiler_params=_cparams(("parallel",)),
    )(x2, g, wm, wkvt, ws, wst)


def _norm_proj_kernel(x_ref, g_ref, w_ref, *out_refs):
    hb = _rms(x_ref[...], g_ref[...]).astype(BF16)
    c0 = 0
    for ref in out_refs:
        width = ref.shape[-1]
        ref[...] = jnp.dot(hb, w_ref[:, c0:c0 + width], preferred_element_type=F32)
        c0 += width


def _norm_proj(x2, g, w, widths, tm):
    t = x2.shape[0]
    return pl.pallas_call(
        _norm_proj_kernel,
        grid=(t // tm,),
        in_specs=[pl.BlockSpec((tm, D_MODEL), lambda i: (i, 0)),
                  pl.BlockSpec(g.shape, lambda i: (0, 0)),
                  pl.BlockSpec(w.shape, lambda i: (0, 0))],
        out_specs=[pl.BlockSpec((tm, wd), lambda i: (i, 0)) for wd in widths],
        out_shape=[jax.ShapeDtypeStruct((t, wd), F32) for wd in widths],
        compiler_params=_cparams(("parallel",)),
    )(x2, g, w)


def _small_mm(a, b):
    acc = a[:, 0:1] * b[0:1, :]
    for i in range(1, a.shape[1]):
        acc = acc + a[:, i:i + 1] * b[i:i + 1, :]
    return acc


def _logf_kernel(smt_ref, fbc_ref, logf_ref, ct_ref, *, seq, chunk):
    lft = _log_sigmoid(smt_ref[0][8:16, :] + fbc_ref[8:16, 0:1])
    logf_ref[0] = lft
    tri = (_iota2((chunk, chunk), 0) <= _iota2((chunk, chunk), 1)).astype(F32)
    carry = jnp.zeros((FOX_HEADS, 1), F32)
    for c in range(seq // chunk):
        blk = lft[:, c * chunk:(c + 1) * chunk]
        cs = (_small_mm(blk, tri) if chunk <= SUBLANES else _dot_hi(blk, tri)) + carry
        ct_ref[0, :, c * chunk:(c + 1) * chunk] = cs
        carry = cs[:, chunk - 1:chunk]


def _logf(smt3, fbc):
    b, _, seq = smt3.shape
    chunk = min(256, seq)
    return pl.pallas_call(
        functools.partial(_logf_kernel, seq=seq, chunk=chunk),
        grid=(b,),
        in_specs=[pl.BlockSpec((1, 16, seq), lambda i: (i, 0, 0)),
                  pl.BlockSpec(fbc.shape, lambda i: (0, 0))],
        out_specs=[pl.BlockSpec((1, FOX_HEADS, seq), lambda i: (i, 0, 0)),
                   pl.BlockSpec((1, FOX_HEADS, seq), lambda i: (i, 0, 0))],
        out_shape=[jax.ShapeDtypeStruct((b, FOX_HEADS, seq), F32),
                   jax.ShapeDtypeStruct((b, FOX_HEADS, seq), F32)],
        compiler_params=_cparams(("parallel",)),
    )(smt3, fbc)


def _unit_lower_inverse(a, c, mm):
    eye = (_iota2((c, c), 0) == _iota2((c, c), 1)).astype(F32)
    n = -a
    t = eye + n
    p = n
    levels = int(math.log2(c))
    for _ in range(levels - 1):
        p = mm(p, p)
        t = t + mm(t, p)
    return t


def _gdn_kernel(qkv_ref, z_ref, sm_ref, smt_ref, cst_ref, s0_ref, cw_ref, gpr_ref, gpc_ref,
                o_ref, sn_ref, cb_ref, s_scr, tail_scr, *, tl, c, exact_small):
    t = pl.program_id(1)
    nt = pl.num_programs(1)

    @pl.when(t == 0)
    def _():
        s_scr[...] = s0_ref[0]
        tail_scr[...] = cst_ref[0]

    rb = (lambda v: v.astype(BF16).astype(F32)) if exact_small else (lambda v: v.astype(BF16))
    bdims = ((0,), (0,))
    mm = lambda a, b: lax.dot_general(rb(a), rb(b), (((2,), (1,)), bdims), preferred_element_type=F32)
    mm_nt = lambda a, b: lax.dot_general(rb(a), rb(b), (((2,), (2,)), bdims), preferred_element_type=F32)
    mm_tn = lambda a, b: lax.dot_general(rb(a), rb(b), (((1,), (1,)), bdims), preferred_element_type=F32)

    x = qkv_ref[0]
    tail = tail_scr[...]
    cw = cw_ref[...]
    row8 = _iota2((SUBLANES, CONV_CH), 0)
    acc = x * cw[CONV_W - 1:CONV_W, :]
    for s in range(1, CONV_W):
        xs = pltpu.roll(x, s, 0)
        head = jnp.where(row8 < s, pltpu.roll(tail, s, 0), xs[0:SUBLANES])
        xs = head if tl == SUBLANES else jnp.concatenate([head, xs[SUBLANES:]], axis=0)
        acc = acc + xs * cw[CONV_W - 1 - s:CONV_W - s, :]
    conv = acc * _sigmoid(acc)
    tail_scr[...] = x[tl - SUBLANES:tl, :]

    @pl.when(t == nt - 1)
    def _():
        cb_ref[0] = x[tl - (CONV_W - 1):tl, :]

    sm = sm_ref[0]
    beta_c = _sigmoid(sm)
    g_c = -jnp.exp(gpr_ref[0:1, :]) * _softplus(sm + gpr_ref[1:2, :])
    smt = smt_ref[0]
    g_r = -jnp.exp(gpc_ref[:, 0:1]) * _softplus(smt + gpc_ref[:, 1:2])
    onorm = gpr_ref[2:3, :]

    ii = _iota2((c, c), 0)
    jj = _iota2((c, c), 1)
    tri_c = (jj <= ii).astype(F32)
    tri_r = (ii <= jj).astype(F32)
    small = c <= SUBLANES
    nc = tl // c
    cum_cols, cum_rows = [], []
    for ci in range(nc):
        gc_blk = g_c[ci * c:(ci + 1) * c, :]
        gr_blk = g_r[:, ci * c:(ci + 1) * c]
        cum_cols.append(_small_mm(tri_c, gc_blk) if small else _dot_hi(tri_c, gc_blk))
        cum_rows.append(_small_mm(gr_blk, tri_r) if small else _dot_hi(gr_blk, tri_r))

    def per_pair(fn):
        return jnp.stack([fn(ci, h) for ci in range(nc) for h in range(GDN_HEADS)], axis=0)

    rows = lambda ci: slice(ci * c, (ci + 1) * c)
    qs = per_pair(lambda ci, h: conv[rows(ci), h * GDN_DK:(h + 1) * GDN_DK])
    ks = per_pair(lambda ci, h: conv[rows(ci), GDN_WIDTH + h * GDN_DK:GDN_WIDTH + (h + 1) * GDN_DK])
    vs = per_pair(lambda ci, h: conv[rows(ci), 2 * GDN_WIDTH + h * GDN_DV:2 * GDN_WIDTH + (h + 1) * GDN_DV])
    beta = per_pair(lambda ci, h: beta_c[rows(ci), h:h + 1])
    cum_c = per_pair(lambda ci, h: cum_cols[ci][:, GDN_HEADS + h:GDN_HEADS + h + 1])
    cum_r = per_pair(lambda ci, h: cum_rows[ci][GDN_HEADS + h:GDN_HEADS + h + 1, :])
    qs = qs * lax.rsqrt(jnp.sum(qs * qs, axis=-1, keepdims=True) + NORM_EPS) * (GDN_DK ** -0.5)
    ks = ks * lax.rsqrt(jnp.sum(ks * ks, axis=-1, keepdims=True) + NORM_EPS)
    cum_last = cum_c[:, c - 1:c, :]
    dec = jnp.exp(jnp.where(ii >= jj, cum_c - cum_r, NEG_BIG))
    dec_strict = jnp.where(ii > jj, dec, 0.0)
    a_mat = beta * mm_nt(ks, ks) * dec_strict
    tinv = _unit_lower_inverse(a_mat, c, mm)
    e_cum = jnp.exp(cum_c)
    rhs = jnp.concatenate([beta * vs, (beta * e_cum) * ks], axis=-1)
    sol = mm(tinv, rhs)
    w_v, w_k = sol[:, :, :GDN_DV], sol[:, :, GDN_DV:]
    p_qk = mm_nt(qs, ks) * dec
    q_g = qs * e_cum
    k_d = ks * jnp.exp(cum_last - cum_c)
    g_end = jnp.exp(cum_last)

    state = s_scr[...]
    for ci in range(nc):
        pr = slice(ci * GDN_HEADS, (ci + 1) * GDN_HEADS)
        u = w_v[pr] - mm(w_k[pr], state)
        o = mm(q_g[pr], state) + mm(p_qk[pr], u)
        state = g_end[pr] * state + mm_tn(k_d[pr], u)
        for h in range(GDN_HEADS):
            lo = h * GDN_DV
            zh = z_ref[0, rows(ci), lo:lo + GDN_DV]
            o_ref[0, rows(ci), lo:lo + GDN_DV] = _rms(o[h], onorm) * (zh * _sigmoid(zh))
    s_scr[...] = state

    @pl.when(t == nt - 1)
    def _():
        sn_ref[0] = state


def _gdn(qkv3, z3, sm3, smt3, cst8, s0, cw, gpr, gpc, tl, c, exact_small):
    b, seq, _ = qkv3.shape
    nt = seq // tl
    full2 = lambda a: pl.BlockSpec(a.shape, lambda i, j: (0, 0))
    return pl.pallas_call(
        functools.partial(_gdn_kernel, tl=tl, c=c, exact_small=exact_small),
        grid=(b, nt),
        in_specs=[pl.BlockSpec((1, tl, CONV_CH), lambda i, j: (i, j, 0)),
                  pl.BlockSpec((1, tl, GDN_WIDTH), lambda i, j: (i, j, 0)),
                  pl.BlockSpec((1, tl, LANES), lambda i, j: (i, j, 0)),
                  pl.BlockSpec((1, 16, tl), lambda i, j: (i, 0, j)),
                  pl.BlockSpec((1, SUBLANES, CONV_CH), lambda i, j: (i, 0, 0)),
                  pl.BlockSpec((1, GDN_HEADS, GDN_DK, GDN_DV), lambda i, j: (i, 0, 0, 0)),
                  full2(cw), full2(gpr), full2(gpc)],
        out_specs=[pl.BlockSpec((1, tl, GDN_WIDTH), lambda i, j: (i, j, 0)),
                   pl.BlockSpec((1, GDN_HEADS, GDN_DK, GDN_DV), lambda i, j: (i, 0, 0, 0)),
                   pl.BlockSpec((1, CONV_W - 1, CONV_CH), lambda i, j: (i, 0, 0))],
        out_shape=[jax.ShapeDtypeStruct((b, seq, GDN_WIDTH), F32),
                   jax.ShapeDtypeStruct((b, GDN_HEADS, GDN_DK, GDN_DV), F32),
                   jax.ShapeDtypeStruct((b, CONV_W - 1, CONV_CH), F32)],
        scratch_shapes=[pltpu.VMEM((GDN_HEADS, GDN_DK, GDN_DV), F32),
                        pltpu.VMEM((SUBLANES, CONV_CH), F32)],
        compiler_params=_cparams(("parallel", "arbitrary")),
    )(qkv3, z3, sm3, smt3, cst8, s0, cw, gpr, gpc)


FOX_HEAD_GROUP = 4


def _fox_prompt_kernel(fq_ref, fkt_ref, fvt_ref, ct_ref, on_ref, o_ref, kb_scr, vb_scr, *, tq):
    qi = pl.program_id(1)
    nk = kb_scr.shape[0]

    @pl.when(qi == 0)
    def _():
        for j in range(nk):
            kb_scr[j] = fkt_ref[0, :, j * tq:(j + 1) * tq].astype(BF16)
            vb_scr[j] = fvt_ref[0, :, j * tq:(j + 1) * tq].astype(BF16)

    q_all = fq_ref[0] * (FOX_HEAD_DIM ** -0.5)
    onorm = on_ref[...]
    causal = _iota2((tq, tq), 1) <= _iota2((tq, tq), 0)
    outs = []
    for g0 in range(0, FOX_HEADS, FOX_HEAD_GROUP):
        heads = range(g0, g0 + FOX_HEAD_GROUP)
        qs = [q_all[:, h * FOX_HEAD_DIM:(h + 1) * FOX_HEAD_DIM].astype(BF16) for h in heads]

        def step(j, carry, masked, heads=heads, qs=qs):
            new = []
            for (m, l, acc), h, qh in zip(carry, heads, qs):
                kt = kb_scr[j, h * FOX_HEAD_DIM:(h + 1) * FOX_HEAD_DIM, :]
                vt = vb_scr[j, h * FOX_HEAD_DIM:(h + 1) * FOX_HEAD_DIM, :]
                s = jnp.dot(qh, kt, preferred_element_type=F32) - ct_ref[0, h, pl.ds(j, 1), :]
                if masked:
                    s = jnp.where(causal, s, NEG_BIG)
                m_new = jnp.maximum(m, jnp.max(s, axis=-1, keepdims=True))
                p = jnp.exp(s - m_new)
                alpha = jnp.exp(m - m_new)
                l = alpha * l + jnp.sum(p, axis=-1, keepdims=True)
                acc = alpha * acc + lax.dot_general(p.astype(BF16), vt, (((1,), (1,)), ((), ())),
                                                    preferred_element_type=F32)
                new.append((m_new, l, acc))
            return tuple(new)

        init = tuple((jnp.full((tq, 1), NEG_BIG, F32), jnp.zeros((tq, 1), F32),
                      jnp.zeros((tq, FOX_HEAD_DIM), F32)) for _ in heads)
        carry = lax.fori_loop(0, qi, functools.partial(step, masked=False), init)
        for m, l, acc in step(qi, carry, True):
            outs.append(_rms(acc / l, onorm))
    o_ref[0] = jnp.concatenate(outs, axis=-1)


def _fox_prompt(fq3, fkt, fvt, ct4, onorm, tq):
    b, seq, _ = fq3.shape
    nq = seq // tq
    return pl.pallas_call(
        functools.partial(_fox_prompt_kernel, tq=tq),
        grid=(b, nq),
        in_specs=[pl.BlockSpec((1, tq, FOX_WIDTH), lambda i, j: (i, j, 0)),
                  pl.BlockSpec((1, FOX_WIDTH, seq), lambda i, j: (i, 0, 0)),
                  pl.BlockSpec((1, FOX_WIDTH, seq), lambda i, j: (i, 0, 0)),
                  pl.BlockSpec((1, FOX_HEADS, nq, tq), lambda i, j: (i, 0, 0, 0)),
                  pl.BlockSpec(onorm.shape, lambda i, j: (0, 0))],
        out_specs=pl.BlockSpec((1, tq, FOX_WIDTH), lambda i, j: (i, j, 0)),
        out_shape=jax.ShapeDtypeStruct((b, seq, FOX_WIDTH), F32),
        scratch_shapes=[pltpu.VMEM((nq, FOX_WIDTH, tq), BF16), pltpu.VMEM((nq, FOX_WIDTH, tq), BF16)],
        compiler_params=_cparams(("parallel", "arbitrary")),
    )(fq3, fkt, fvt, ct4, onorm)


def _logf_pages_kernel(lf_ref, w_ref, o_ref):
    lf = lf_ref[...]
    a = lf.astype(BF16)
    r1 = lf - a.astype(F32)
    b = r1.astype(BF16)
    c = (r1 - b.astype(F32)).astype(BF16)
    w = w_ref[...]
    o_ref[...] = (jnp.dot(a, w, preferred_element_type=F32) + jnp.dot(b, w, preferred_element_type=F32)
                  + jnp.dot(c, w, preferred_element_type=F32))


def _logf_pages(lf_flat, w2):
    n_pool, width = lf_flat.shape
    tp = next(c for c in (2048, 1024, 512, 256, 128, 64, 32, 16, 8) if n_pool % c == 0)
    return pl.pallas_call(
        _logf_pages_kernel,
        grid=(n_pool // tp,),
        in_specs=[pl.BlockSpec((tp, width), lambda i: (i, 0)),
                  pl.BlockSpec(w2.shape, lambda i: (0, 0))],
        out_specs=pl.BlockSpec((tp, 2 * width), lambda i: (i, 0)),
        out_shape=jax.ShapeDtypeStruct((n_pool, 2 * width), F32),
        compiler_params=_cparams(("parallel",)),
    )(lf_flat, w2)


def _fox_sample_kernel(pt_ref, qbd_ref, kn_ref, vn_ref, cn_ref, on_ref, *rest, pps, lq):
    k_refs = rest[0:pps]
    v_refs = rest[pps:2 * pps]
    r_refs = rest[2 * pps:3 * pps]
    o_ref = rest[3 * pps]
    m_scr, l_scr, acc_scr, suf_scr = rest[3 * pps + 1:]
    step = pl.program_id(1)
    nstep = pl.num_programs(1)
    rows = lq * FOX_HEADS
    qbd = qbd_ref[0]
    tile_rows = lambda v: jnp.concatenate([v] * lq, axis=0)

    @pl.when(step == 0)
    def _():
        s = lax.dot_general(qbd, kn_ref[0].astype(BF16), (((1,), (1,)), ((), ())),
                            preferred_element_type=F32)
        s = s - tile_rows(cn_ref[0])
        s = jnp.where(_iota2((rows, lq), 1) <= _iota2((rows, lq), 0) // FOX_HEADS, s, NEG_BIG)
        m = jnp.max(s, axis=-1, keepdims=True)
        p = jnp.exp(s - m)
        m_scr[...] = m
        l_scr[...] = jnp.sum(p, axis=-1, keepdims=True)
        acc_scr[...] = jnp.dot(p.astype(BF16), vn_ref[0].astype(BF16), preferred_element_type=F32)
        suf_scr[...] = jnp.zeros_like(suf_scr)

    page = k_refs[0].shape[3]
    suf = suf_scr[...]
    scores = []
    for i in range(pps):
        r2 = r_refs[i][0]
        s = jnp.dot(qbd, k_refs[i][0, 0].astype(BF16), preferred_element_type=F32)
        scores.append(s + tile_rows(r2[:, :page] + suf))
        suf = suf + r2[:, page:]
    suf_scr[...] = suf
    m = m_scr[...]
    m_new = m
    for s in scores:
        m_new = jnp.maximum(m_new, jnp.max(s, axis=-1, keepdims=True))
    alpha = jnp.exp(m - m_new)
    l = alpha * l_scr[...]
    acc = alpha * acc_scr[...]
    for i in range(pps):
        p = jnp.exp(scores[i] - m_new)
        l = l + jnp.sum(p, axis=-1, keepdims=True)
        acc = acc + lax.dot_general(p.astype(BF16), v_refs[i][0, 0].astype(BF16), (((1,), (1,)), ((), ())),
                                    preferred_element_type=F32)
    m_scr[...] = m_new
    l_scr[...] = l
    acc_scr[...] = acc

    @pl.when(step == nstep - 1)
    def _():
        own = _iota2((rows, FOX_WIDTH), 1) // FOX_HEAD_DIM == _iota2((rows, FOX_WIDTH), 0) % FOX_HEADS
        o = jnp.where(own, acc / l, 0.0)
        ms = jnp.sum(o * o, axis=-1, keepdims=True) * (1.0 / FOX_HEAD_DIM)
        o = o * lax.rsqrt(ms + NORM_EPS) * on_ref[...]
        o_ref[0] = jnp.sum(o.reshape(lq, FOX_HEADS, FOX_WIDTH), axis=1)


def _fox_sample(page_table, layer, qbd, kn3, vn3, cn3, onorm, ckt, cvt, r3, pps):
    b, lq, _ = kn3.shape
    n_pages = page_table.shape[1]
    page = ckt.shape[3]
    nstep = n_pages // pps
    rows = FOX_HEADS * lq

    def page_map(i):
        return lambda bi, s, pt: (layer, pt[bi, n_pages - 1 - (s * pps + i)], 0, 0)

    def r_map(i):
        return lambda bi, s, pt: (pt[bi, n_pages - 1 - (s * pps + i)], 0, 0)

    in_specs = [pl.BlockSpec((1, rows, FOX_WIDTH), lambda bi, s, pt: (bi, 0, 0)),
                pl.BlockSpec((1, lq, FOX_WIDTH), lambda bi, s, pt: (bi, 0, 0)),
                pl.BlockSpec((1, lq, FOX_WIDTH), lambda bi, s, pt: (bi, 0, 0)),
                pl.BlockSpec((1, FOX_HEADS, lq), lambda bi, s, pt: (bi, 0, 0)),
                pl.BlockSpec(onorm.shape, lambda bi, s, pt: (0, 0))]
    in_specs += [pl.BlockSpec((1, 1, FOX_WIDTH, page), page_map(i)) for i in range(pps)]
    in_specs += [pl.BlockSpec((1, 1, FOX_WIDTH, page), page_map(i)) for i in range(pps)]
    in_specs += [pl.BlockSpec((1, FOX_HEADS, 2 * page), r_map(i)) for i in range(pps)]
    grid_spec = pltpu.PrefetchScalarGridSpec(
        num_scalar_prefetch=1,
        grid=(b, nstep),
        in_specs=in_specs,
        out_specs=pl.BlockSpec((1, lq, FOX_WIDTH), lambda bi, s, pt: (bi, 0, 0)),
        scratch_shapes=[pltpu.VMEM((rows, 1), F32), pltpu.VMEM((rows, 1), F32),
                        pltpu.VMEM((rows, FOX_WIDTH), F32), pltpu.VMEM((FOX_HEADS, page), F32)],
    )
    return pl.pallas_call(
        functools.partial(_fox_sample_kernel, pps=pps, lq=lq),
        grid_spec=grid_spec,
        out_shape=jax.ShapeDtypeStruct((b, lq, FOX_WIDTH), F32),
        compiler_params=_cparams(("parallel", "arbitrary")),
    )(page_table, qbd, kn3, vn3, cn3, onorm, *([ckt] * pps), *([cvt] * pps), *([r3] * pps))


def _out_proj_kernel(x_ref, a_ref, b_ref, w_ref, o_ref):
    o_ref[...] = (x_ref[...]
                  + jnp.dot(a_ref[...].astype(BF16), w_ref[0:GDN_WIDTH, :], preferred_element_type=F32)
                  + jnp.dot(b_ref[...].astype(BF16), w_ref[GDN_WIDTH:, :], preferred_element_type=F32))


def _out_proj(x2, a2, b2, w, tm):
    t = x2.shape[0]
    return pl.pallas_call(
        _out_proj_kernel,
        grid=(t // tm,),
        in_specs=[pl.BlockSpec((tm, D_MODEL), lambda i: (i, 0)),
                  pl.BlockSpec((tm, GDN_WIDTH), lambda i: (i, 0)),
                  pl.BlockSpec((tm, FOX_WIDTH), lambda i: (i, 0)),
                  pl.BlockSpec(w.shape, lambda i: (0, 0))],
        out_specs=pl.BlockSpec((tm, D_MODEL), lambda i: (i, 0)),
        out_shape=jax.ShapeDtypeStruct((t, D_MODEL), F32),
        compiler_params=_cparams(("parallel",)),
    )(x2, a2, b2, w)


def _xattn_kernel(x_ref, g_ref, wq_ref, wo_ref, mk_ref, mv_ref, o_ref):
    x = x_ref[0]
    hb = _rms(x, g_ref[...]).astype(BF16)
    q = jnp.dot(hb, wq_ref[...], preferred_element_type=F32) * (XA_HEAD_DIM ** -0.5)
    mk = mk_ref[0].astype(BF16)
    mv = mv_ref[0].astype(BF16)
    outs = []
    for h in range(XA_HEADS):
        lo = h * XA_HEAD_DIM
        s = lax.dot_general(q[:, lo:lo + XA_HEAD_DIM].astype(BF16), mk[:, lo:lo + XA_HEAD_DIM],
                            (((1,), (1,)), ((), ())), preferred_element_type=F32)
        p = jnp.exp(s - jnp.max(s, axis=-1, keepdims=True))
        p = p / jnp.sum(p, axis=-1, keepdims=True)
        outs.append(jnp.dot(p.astype(BF16), mv[:, lo:lo + XA_HEAD_DIM], preferred_element_type=F32))
    o = jnp.concatenate(outs, axis=-1).astype(BF16)
    o_ref[0] = x + jnp.dot(o, wo_ref[...], preferred_element_type=F32)


def _xattn(x3, g, wq, wo, mk3, mv3, tq):
    b, seq, _ = x3.shape
    n_mem = mk3.shape[1]
    return pl.pallas_call(
        _xattn_kernel,
        grid=(b, seq // tq),
        in_specs=[pl.BlockSpec((1, tq, D_MODEL), lambda i, j: (i, j, 0)),
                  pl.BlockSpec(g.shape, lambda i, j: (0, 0)),
                  pl.BlockSpec(wq.shape, lambda i, j: (0, 0)),
                  pl.BlockSpec(wo.shape, lambda i, j: (0, 0)),
                  pl.BlockSpec((1, n_mem, XA_WIDTH), lambda i, j: (i, 0, 0)),
                  pl.BlockSpec((1, n_mem, XA_WIDTH), lambda i, j: (i, 0, 0))],
        out_specs=pl.BlockSpec((1, tq, D_MODEL), lambda i, j: (i, j, 0)),
        out_shape=jax.ShapeDtypeStruct((b, seq, D_MODEL), F32),
        compiler_params=_cparams(("parallel", "parallel")),
    )(x3, g, wq, wo, mk3, mv3)


def _router_kernel(x_ref, g_ref, wr_ref, br_ref, h_ref, route_ref, gate_ref, cnt_ref, base_scr, *, tm):
    i = pl.program_id(0)

    @pl.when(i == 0)
    def _():
        base_scr[...] = jnp.zeros_like(base_scr)

    h = _rms(x_ref[...], g_ref[...])
    _store_row_tiles(h_ref, h, tm)
    logits = _dot_hi(h, wr_ref[...]) + br_ref[...]
    lane = _iota2((tm, LANES), 1)
    lane_f = lane.astype(F32)
    vals, hots, idxs = [], [], []
    cur = logits
    for _ in range(TOP_K):
        mx = jnp.max(cur, axis=-1, keepdims=True)
        idx_f = jnp.min(jnp.where(cur == mx, lane_f, float(LANES)), axis=-1, keepdims=True)
        hot = lane_f == idx_f
        vals.append(mx)
        idxs.append(idx_f.astype(I32))
        hots.append(hot)
        cur = jnp.where(hot, -jnp.inf, cur)
    exps = [jnp.exp(v - vals[0]) for v in vals]
    denom = exps[0] + exps[1] + exps[2] + exps[3]
    member = (hots[0] | hots[1] | hots[2] | hots[3]).astype(F32)
    strict = (_iota2((tm, tm), 1) < _iota2((tm, tm), 0)).astype(BF16)
    before = jnp.dot(strict, member.astype(BF16), preferred_element_type=F32) + base_scr[...]
    route = jnp.zeros((tm, LANES), I32)
    gate = jnp.zeros((tm, LANES), F32)
    for k in range(TOP_K):
        rank = jnp.sum(jnp.where(hots[k], before, 0.0), axis=-1, keepdims=True).astype(I32)
        route = jnp.where(lane == k, idxs[k], route)
        route = jnp.where(lane == TOP_K + k, rank, route)
        gate = jnp.where(lane == k, exps[k] / denom, gate)
    route_ref[...] = route
    gate_ref[...] = gate
    base_scr[...] = base_scr[...] + jnp.sum(member, axis=0, keepdims=True)
    cnt_ref[...] = base_scr[...]


def _router(x2, g, wr, br, tm):
    t = x2.shape[0]
    return pl.pallas_call(
        functools.partial(_router_kernel, tm=tm),
        grid=(t // tm,),
        in_specs=[pl.BlockSpec((tm, D_MODEL), lambda i: (i, 0)),
                  pl.BlockSpec(g.shape, lambda i: (0, 0)),
                  pl.BlockSpec(wr.shape, lambda i: (0, 0)),
                  pl.BlockSpec(br.shape, lambda i: (0, 0))],
        out_specs=[pl.BlockSpec((tm * ROW_TILE, LANES), lambda i: (i, 0)),
                   pl.BlockSpec((tm, LANES), lambda i: (i, 0)),
                   pl.BlockSpec((tm, LANES), lambda i: (i, 0)),
                   pl.BlockSpec((1, LANES), lambda i: (0, 0))],
        out_shape=[jax.ShapeDtypeStruct((t * ROW_TILE, LANES), F32), jax.ShapeDtypeStruct((t, LANES), I32),
                   jax.ShapeDtypeStruct((t, LANES), F32), jax.ShapeDtypeStruct((1, LANES), F32)],
        scratch_shapes=[pltpu.VMEM((1, LANES), F32)],
        compiler_params=_cparams(("arbitrary",)),
    )(x2, g, wr, br)


def _dispatch_kernel(dest_ref, h_ref, init_hbm, xs_hbm, sem, *, tm):
    del init_hbm
    def issue(t, c):
        src = pl.multiple_of(t * ROW_TILE, ROW_TILE)
        for k in range(TOP_K):
            dst = pl.multiple_of(dest_ref[0, 0, t * TOP_K + k] * ROW_TILE, ROW_TILE)
            pltpu.make_async_copy(h_ref.at[pl.ds(src, ROW_TILE)], xs_hbm.at[pl.ds(dst, ROW_TILE)], sem).start()
        return c

    lax.fori_loop(0, tm, issue, 0, unroll=2)
    for _ in range(TOP_K):
        pltpu.make_async_copy(h_ref, xs_hbm.at[pl.ds(0, tm * ROW_TILE)], sem).wait()


def _dispatch(dest3, h2, rows, tm):
    t = h2.shape[0] // ROW_TILE
    init = jnp.zeros((rows * ROW_TILE, LANES), F32)
    return pl.pallas_call(
        functools.partial(_dispatch_kernel, tm=tm),
        grid=(t // tm,),
        in_specs=[pl.BlockSpec((1, 1, tm * TOP_K), lambda i: (i, 0, 0), memory_space=pltpu.SMEM),
                  pl.BlockSpec((tm * ROW_TILE, LANES), lambda i: (i, 0)),
                  pl.BlockSpec(memory_space=pl.ANY)],
        out_specs=pl.BlockSpec(memory_space=pl.ANY),
        out_shape=jax.ShapeDtypeStruct((rows * ROW_TILE, LANES), F32),
        scratch_shapes=[pltpu.SemaphoreType.DMA(())],
        input_output_aliases={2: 0},
        compiler_params=pltpu.CompilerParams(dimension_semantics=("arbitrary",), vmem_limit_bytes=VMEM_LIMIT,
                                             has_side_effects=True),
    )(dest3, h2, init)


def _ffn_kernel(be_ref, nv_ref, x_ref, w1_ref, b1_ref, w2_ref, b2_ref, o_ref, *, tb):
    i = pl.program_id(0)
    live = i * tb < nv_ref[0]

    @pl.when(live)
    def _():
        x = _load_row_tiles(x_ref, tb).astype(BF16)
        hb = jnp.dot(x, w1_ref[0], preferred_element_type=F32) + b1_ref[0]
        glu = jnp.minimum(hb[:, :D_FF], SWIGLU_LIMIT)
        lin = jnp.clip(hb[:, D_FF:], -SWIGLU_LIMIT, SWIGLU_LIMIT)
        act = glu * _sigmoid(SWIGLU_ALPHA * glu) * (lin + 1.0)
        y = jnp.dot(act.astype(BF16), w2_ref[0], preferred_element_type=F32) + b2_ref[0]
        _store_row_tiles(o_ref, y, tb)

    @pl.when(jnp.logical_not(live))
    def _():
        o_ref[...] = jnp.zeros_like(o_ref)


def _ffn(block_e, nvalid, xs, w1, b1, w2, b2, tb):
    rows = xs.shape[0] // ROW_TILE
    nb = rows // tb
    blk = pl.BlockSpec((tb * ROW_TILE, LANES), lambda i, be, nv: (i, 0))
    grid_spec = pltpu.PrefetchScalarGridSpec(
        num_scalar_prefetch=2,
        grid=(nb,),
        in_specs=[blk,
                  pl.BlockSpec((1, D_MODEL, 2 * D_FF), lambda i, be, nv: (be[i], 0, 0)),
                  pl.BlockSpec((1, 1, 2 * D_FF), lambda i, be, nv: (be[i], 0, 0)),
                  pl.BlockSpec((1, D_FF, D_MODEL), lambda i, be, nv: (be[i], 0, 0)),
                  pl.BlockSpec((1, 1, D_MODEL), lambda i, be, nv: (be[i], 0, 0))],
        out_specs=blk,
    )
    return pl.pallas_call(
        functools.partial(_ffn_kernel, tb=tb),
        grid_spec=grid_spec,
        out_shape=jax.ShapeDtypeStruct((rows * ROW_TILE, LANES), F32),
        compiler_params=_cparams(("arbitrary",)),
    )(block_e, nvalid, xs, w1, b1, w2, b2)


def _combine_kernel(dcur_ref, dnext_ref, x_ref, gate_ref, fg_ref, os_hbm, y_ref, yn_ref, buf, sem, *, tm):
    i = pl.program_id(0)
    n = pl.num_programs(0)

    def issue_tile(dref, slot):
        def issue(t, c):
            dst = pl.multiple_of(t * ROW_TILE, ROW_TILE)
            for k in range(TOP_K):
                src = pl.multiple_of(dref[0, 0, t * TOP_K + k] * ROW_TILE, ROW_TILE)
                pltpu.make_async_copy(os_hbm.at[pl.ds(src, ROW_TILE)],
                                      buf.at[slot, k, pl.ds(dst, ROW_TILE)], sem.at[slot]).start()
            return c

        lax.fori_loop(0, tm, issue, 0, unroll=2)

    @pl.when(i == 0)
    def _():
        issue_tile(dcur_ref, 0)

    @pl.when(i + 1 < n)
    def _():
        issue_tile(dnext_ref, (i + 1) % 2)

    slot = i % 2
    for k in range(TOP_K):
        pltpu.make_async_copy(os_hbm.at[pl.ds(0, tm * ROW_TILE)], buf.at[slot, k], sem.at[slot]).wait()
    gate = gate_ref[...]
    x = x_ref[...]
    pieces = []
    for s in range(ROW_TILE):
        acc = x[:, s * LANES:(s + 1) * LANES]
        for k in range(TOP_K):
            acc = acc + gate[:, k:k + 1] * buf[slot, k, pl.ds(s, tm, stride=ROW_TILE), :]
        pieces.append(acc)
    y = jnp.concatenate(pieces, axis=-1)
    y_ref[...] = y
    yn_ref[...] = _rms(y, fg_ref[...])


def _combine(dest3, x2, gate, fg, os2, tm):
    t = x2.shape[0]
    nt = t // tm
    return pl.pallas_call(
        functools.partial(_combine_kernel, tm=tm),
        grid=(nt,),
        in_specs=[pl.BlockSpec((1, 1, tm * TOP_K), lambda i: (i, 0, 0), memory_space=pltpu.SMEM),
                  pl.BlockSpec((1, 1, tm * TOP_K), lambda i: (jnp.minimum(i + 1, nt - 1), 0, 0),
                               memory_space=pltpu.SMEM),
                  pl.BlockSpec((tm, D_MODEL), lambda i: (i, 0)),
                  pl.BlockSpec((tm, LANES), lambda i: (i, 0)),
                  pl.BlockSpec(fg.shape, lambda i: (0, 0)),
                  pl.BlockSpec(memory_space=pl.ANY)],
        out_specs=[pl.BlockSpec((tm, D_MODEL), lambda i: (i, 0)),
                   pl.BlockSpec((tm, D_MODEL), lambda i: (i, 0))],
        out_shape=[jax.ShapeDtypeStruct((t, D_MODEL), F32), jax.ShapeDtypeStruct((t, D_MODEL), F32)],
        scratch_shapes=[pltpu.VMEM((2, TOP_K, tm * ROW_TILE, LANES), F32), pltpu.SemaphoreType.DMA((2,))],
        compiler_params=_cparams(("arbitrary",)),
    )(dest3, dest3, x2, gate, fg, os2)


def _row_tile(t, want):
    tm = min(want, t)
    assert t % tm == 0
    return tm


def _mixer(x3, lw, conv_state8, s0, fox_fn, kv_t):
    b, seq, _ = x3.shape
    t = b * seq
    x2 = x3.reshape(t, D_MODEL)
    qkv, z, fq, fk, fv, sm, smt = _in_proj(x2, lw["norm1_g"], lw["w_main"], lw["w_kv_t"], lw["w_small"],
                                           lw["w_small_t"], _row_tile(t, 256), seq if kv_t else None)
    sm3 = sm.reshape(b, seq, LANES)
    smt3 = jnp.transpose(smt.reshape(16, b, seq), (1, 0, 2))
    logf_t, ct = _logf(smt3, lw["fb_col"])
    logf = jnp.transpose(logf_t, (0, 2, 1))
    c = min(GDN_CHUNK, seq)
    tl = min(256, seq)
    gdn_out, s_new, cbuf = _gdn(qkv.reshape(b, seq, CONV_CH), z.reshape(b, seq, GDN_WIDTH), sm3, smt3,
                                conv_state8, s0, lw["conv_w"], lw["gp_row"], lw["gp_col"], tl, c,
                                exact_small=(c <= SUBLANES))
    fq3 = fq.reshape(b, seq, FOX_WIDTH)
    if kv_t:
        fk3, fv3 = fk, fv
        as_out = lambda a: jnp.transpose(a.reshape(b, FOX_HEADS, FOX_HEAD_DIM, seq), (0, 3, 1, 2))
    else:
        fk3, fv3 = fk.reshape(b, seq, FOX_WIDTH), fv.reshape(b, seq, FOX_WIDTH)
        as_out = lambda a: a.reshape(b, seq, FOX_HEADS, FOX_HEAD_DIM)
    fox_out = fox_fn(fq3, fk3, fv3, ct)
    y2 = _out_proj(x2, gdn_out.reshape(t, GDN_WIDTH), fox_out.reshape(t, FOX_WIDTH), lw["w_out"],
                   _row_tile(t, 512))
    return y2.reshape(b, seq, D_MODEL), s_new, cbuf, as_out(fk3), as_out(fv3), logf


def _moe(x3, lw, final_g):
    b, seq, _ = x3.shape
    t = b * seq
    x2 = x3.reshape(t, D_MODEL)
    h2, route, gate, counts = _router(x2, lw["norm3_g"], lw["w_router"], lw["b_router"], _row_tile(t, 256))
    tb = EXPERT_BLOCK_LARGE if t * TOP_K >= N_EXPERTS * 4 * EXPERT_BLOCK_LARGE else EXPERT_BLOCK
    cnt = counts[0, :N_EXPERTS].astype(I32)
    padded = (cnt + tb - 1) // tb * tb
    pends = jnp.cumsum(padded)
    pstart = pends - padded
    nb = t * TOP_K // tb + N_EXPERTS
    rows = nb * tb
    block_row0 = jnp.arange(nb, dtype=I32) * tb
    block_e = jnp.minimum(jnp.sum((pends[None, :] <= block_row0[:, None]).astype(I32), axis=1), N_EXPERTS - 1)
    nvalid = pends[-1:].astype(I32)
    hot = route[:, :TOP_K, None] == jnp.arange(N_EXPERTS, dtype=I32)[None, None, :]
    dest = jnp.sum(jnp.where(hot, pstart[None, None, :], 0), axis=-1) + route[:, TOP_K:2 * TOP_K]
    tmd = _row_tile(t, 512)
    xs = _dispatch(dest.reshape(t // tmd, 1, tmd * TOP_K), h2, rows, tmd)
    os2 = _ffn(block_e, nvalid, xs, lw["w1"], lw["b1"], lw["w2"], lw["b2"], tb)
    tmc = _row_tile(t, 256)
    y2, yn2 = _combine(dest.reshape(t // tmc, 1, tmc * TOP_K), x2, gate, final_g, os2, tmc)
    return y2.reshape(b, seq, D_MODEL), yn2.reshape(b, seq, D_MODEL)


def _prep_layer(l, norm1_g, w_in, conv_w, gdn_a_log, gdn_dt_bias, gdn_onorm, fox_fbias, fox_onorm, w_out,
                norm2_g, mem_norm_g, w_xq, w_mkv, w_xo, norm3_g, w_router, b_router, w1, b1, w2, b2):
    wi = w_in[l]
    gq, gk, gv, gz, gb, ga, fq, fk, fv, ff = _split_in(wi)
    w_main = jnp.concatenate([gq, gk, gv, gz, fq, fk, fv], axis=1).astype(BF16)
    w_kv_t = jnp.transpose(jnp.concatenate([fk, fv], axis=1)).astype(BF16)
    w_small = jnp.concatenate([gb, ga, ff, jnp.zeros((D_MODEL, LANES - 16), F32)], axis=1).astype(BF16)
    w_small_t = jnp.transpose(w_small[:, :16])
    lanes = lambda v, off: jnp.zeros((LANES,), F32).at[off:off + v.shape[0]].set(v)
    gp_row = jnp.zeros((SUBLANES, LANES), F32)
    gp_row = gp_row.at[0].set(lanes(gdn_a_log[l], GDN_HEADS)).at[1].set(lanes(gdn_dt_bias[l], GDN_HEADS))
    gp_row = gp_row.at[2].set(gdn_onorm[l])
    gp_col = jnp.zeros((16, LANES), F32)
    gp_col = gp_col.at[GDN_HEADS:2 * GDN_HEADS, 0].set(gdn_a_log[l]).at[GDN_HEADS:2 * GDN_HEADS, 1].set(gdn_dt_bias[l])
    fb_col = jnp.zeros((16, LANES), F32).at[8:16, 0].set(fox_fbias[l])
    wr = jnp.concatenate([w_router[l], jnp.zeros((D_MODEL, LANES - N_EXPERTS), F32)], axis=1)
    br = jnp.full((1, LANES), NEG_BIG, F32).at[0, :N_EXPERTS].set(b_router[l])
    return {
        "norm1_g": norm1_g[l].reshape(1, D_MODEL), "w_main": w_main, "w_kv_t": w_kv_t,
        "w_small": w_small, "w_small_t": w_small_t,
        "conv_w": conv_w[l], "gp_row": gp_row, "gp_col": gp_col, "fb_col": fb_col,
        "fox_onorm": fox_onorm[l].reshape(1, FOX_HEAD_DIM), "w_out": w_out[l].astype(BF16),
        "norm2_g": norm2_g[l].reshape(1, D_MODEL), "mem_norm_g": mem_norm_g[l].reshape(1, D_MODEL),
        "w_xq": w_xq[l].astype(BF16), "w_mkv": w_mkv[l].astype(BF16), "w_xo": w_xo[l].astype(BF16),
        "norm3_g": norm3_g[l].reshape(1, D_MODEL), "w_router": wr, "b_router": br,
        "w1": w1[l].astype(BF16), "b1": b1[l].reshape(N_EXPERTS, 1, 2 * D_FF),
        "w2": w2[l].astype(BF16), "b2": b2[l].reshape(N_EXPERTS, 1, D_MODEL),
    }


def _split_in(wi):
    widths = (GDN_WIDTH, GDN_WIDTH, GDN_WIDTH, GDN_WIDTH, GDN_HEADS, GDN_HEADS,
              FOX_WIDTH, FOX_WIDTH, FOX_WIDTH, FOX_HEADS)
    outs, c0 = [], 0
    for w in widths:
        outs.append(wi[:, c0:c0 + w])
        c0 += w
    return outs


def kernel(x_prompt, x_sample, mem_prompt, cache_fox_k, cache_fox_v, cache_fox_logf, cache_mem_k, cache_mem_v, state_gdn, state_conv, page_table, norm1_g, w_in, conv_w, gdn_a_log, gdn_dt_bias, gdn_onorm, fox_fbias, fox_onorm, w_out, norm2_g, mem_norm_g, w_xq, w_mkv, w_xo, norm3_g, w_router, b_router, w1, b1, w2, b2, final_norm_g):
    depth = w_in.shape[0]
    bp, lp, _ = x_prompt.shape
    bs, ls, _ = x_sample.shape
    n_mem = mem_prompt.shape[1]
    n_pool, page = cache_fox_k.shape[1], cache_fox_k.shape[2]
    ckt = jnp.transpose(cache_fox_k, (0, 1, 3, 4, 2)).reshape(depth, n_pool, FOX_WIDTH, page)
    cvt = jnp.transpose(cache_fox_v, (0, 1, 3, 4, 2)).reshape(depth, n_pool, FOX_WIDTH, page)
    lf_rows = jnp.transpose(cache_fox_logf, (0, 1, 3, 2)).reshape(depth, n_pool * FOX_HEADS, page)
    tok = jnp.arange(page, dtype=I32)
    later = tok[:, None] > tok[None, :]
    w_suffix = jnp.concatenate([later, jnp.ones_like(later)], axis=1).astype(BF16)
    head_mask = (jnp.arange(FOX_HEADS)[:, None] == jnp.arange(FOX_HEADS)[None, :]).astype(F32)
    final_g = final_norm_g.reshape(1, D_MODEL)
    xp, xs = x_prompt, x_sample
    yp = ys = None
    outs = {k: [] for k in ("fkp", "fvp", "flp", "mkp", "mvp", "sgp", "scp", "fks", "fvs", "fls", "sgs", "scs")}
    for l in range(depth):
        lw = _prep_layer(l, norm1_g, w_in, conv_w, gdn_a_log, gdn_dt_bias, gdn_onorm, fox_fbias, fox_onorm,
                         w_out, norm2_g, mem_norm_g, w_xq, w_mkv, w_xo, norm3_g, w_router, b_router,
                         w1, b1, w2, b2)
        tq = min(512, lp)

        def fox_p(fq3, fk3, fv3, ct, lw=lw, tq=tq):
            return _fox_prompt(fq3, fk3, fv3, ct.reshape(bp, FOX_HEADS, lp // tq, tq), lw["fox_onorm"], tq)

        xp, s_new, cbuf, k_new, v_new, lf_new = _mixer(
            xp, lw, jnp.zeros((bp, SUBLANES, CONV_CH), F32),
            jnp.zeros((bp, GDN_HEADS, GDN_DK, GDN_DV), F32), fox_p, kv_t=True)
        outs["fkp"].append(k_new); outs["fvp"].append(v_new); outs["flp"].append(lf_new)
        outs["sgp"].append(s_new); outs["scp"].append(cbuf)

        r3 = _logf_pages(lf_rows[l], w_suffix).reshape(n_pool, FOX_HEADS, 2 * page)
        on_tiled = jnp.tile(lw["fox_onorm"], (1, FOX_HEADS))

        def fox_s(fq3, fk3, fv3, ct, lw=lw, l=l, r3=r3, on_tiled=on_tiled):
            q4 = fq3.reshape(bs, ls, FOX_HEADS, FOX_HEAD_DIM) * (FOX_HEAD_DIM ** -0.5)
            qbd = jnp.einsum("bqhd,hg->bqhgd", q4, head_mask).reshape(bs, ls * FOX_HEADS, FOX_WIDTH)
            return _fox_sample(page_table, l, qbd.astype(BF16), fk3, fv3, ct, on_tiled, ckt, cvt, r3,
                               pps=math.gcd(page_table.shape[1], 32))

        cst8 = jnp.pad(state_conv[l], ((0, 0), (SUBLANES - (CONV_W - 1), 0), (0, 0)))
        xs, s_new, cbuf, k_new, v_new, lf_new = _mixer(xs, lw, cst8, state_gdn[l], fox_s, kv_t=False)
        outs["fks"].append(k_new); outs["fvs"].append(v_new); outs["fls"].append(lf_new)
        outs["sgs"].append(s_new); outs["scs"].append(cbuf)

        mk2, mv2 = _norm_proj(mem_prompt.reshape(bp * n_mem, D_MODEL), lw["mem_norm_g"], lw["w_mkv"],
                              (XA_WIDTH, XA_WIDTH), _row_tile(bp * n_mem, 512))
        outs["mkp"].append(mk2.reshape(bp, n_mem, XA_HEADS, XA_HEAD_DIM))
        outs["mvp"].append(mv2.reshape(bp, n_mem, XA_HEADS, XA_HEAD_DIM))
        xp = _xattn(xp, lw["norm2_g"], lw["w_xq"], lw["w_xo"], mk2.reshape(bp, n_mem, XA_WIDTH),
                    mv2.reshape(bp, n_mem, XA_WIDTH), min(512, lp))
        xs = _xattn(xs, lw["norm2_g"], lw["w_xq"], lw["w_xo"], cache_mem_k[l].reshape(bs, n_mem, XA_WIDTH),
                    cache_mem_v[l].reshape(bs, n_mem, XA_WIDTH), ls)

        xp, yp = _moe(xp, lw, final_g)
        xs, ys = _moe(xs, lw, final_g)
    st = jnp.stack
    return (yp, ys, st(outs["fkp"]), st(outs["fvp"]), st(outs["flp"]), st(outs["mkp"]), st(outs["mvp"]),
            st(outs["sgp"]), st(outs["scp"]), st(outs["fks"]), st(outs["fvs"]), st(outs["fls"]),
            st(outs["sgs"]), st(outs["scs"]))
```

```python
import functools
import math

import jax
import jax.numpy as jnp
from jax import lax
from jax.experimental import pallas as pl
from jax.experimental.pallas import tpu as pltpu

F32 = jnp.float32
BF16 = jnp.bfloat16
I32 = jnp.int32
HIGHEST = lax.Precision.HIGHEST

D_MODEL = 1024
GDN_HEADS = 4
GDN_DK = 128
GDN_DV = 128
GDN_WIDTH = GDN_HEADS * GDN_DV
CONV_W = 4
CONV_CH = 3 * GDN_WIDTH
GDN_CHUNK = 64
FOX_HEADS = 8
FOX_HEAD_DIM = 64
FOX_WIDTH = FOX_HEADS * FOX_HEAD_DIM
XA_HEADS = 4
XA_HEAD_DIM = 128
XA_WIDTH = XA_HEADS * XA_HEAD_DIM
N_EXPERTS = 32
TOP_K = 4
D_FF = D_MODEL
SWIGLU_LIMIT = 7.0
SWIGLU_ALPHA = 1.702
NORM_EPS = 1e-6
NEG_BIG = -1e30

LANES = 128
SUBLANES = 8
VMEM_LIMIT = 52 * 1024 * 1024
FFN_VMEM_LIMIT = 58 * 1024 * 1024
MAIN_COLS = 2 * GDN_WIDTH + 2 * GDN_WIDTH + 3 * FOX_WIDTH
EXPERT_BLOCK = 256
EXPERT_BLOCK_LARGE = 512


def _cparams(sem):
    return pltpu.CompilerParams(dimension_semantics=sem, vmem_limit_bytes=VMEM_LIMIT)


def _dot(a, b):
    return jnp.dot(a.astype(BF16), b.astype(BF16), preferred_element_type=F32)


def _dot_nt(a, b):
    return lax.dot_general(a.astype(BF16), b.astype(BF16), (((1,), (1,)), ((), ())),
                           preferred_element_type=F32)


def _dot_tn(a, b):
    return lax.dot_general(a.astype(BF16), b.astype(BF16), (((0,), (0,)), ((), ())),
                           preferred_element_type=F32)


def _dot_hi(a, b):
    return jnp.dot(a, b, precision=HIGHEST, preferred_element_type=F32)


def _dot_nt_hi(a, b):
    return lax.dot_general(a, b, (((1,), (1,)), ((), ())), precision=HIGHEST,
                           preferred_element_type=F32)


def _rms(x, g):
    return x * lax.rsqrt(jnp.mean(x * x, axis=-1, keepdims=True) + NORM_EPS) * g


def _sigmoid(x):
    return 1.0 / (1.0 + jnp.exp(-x))


def _softplus(x):
    return jnp.maximum(x, 0.0) + jnp.log1p(jnp.exp(-jnp.abs(x)))


def _log_sigmoid(x):
    return jnp.minimum(x, 0.0) - jnp.log1p(jnp.exp(-jnp.abs(x)))


def _iota2(shape, dim):
    return lax.broadcasted_iota(I32, shape, dim)


ROW_TILE = D_MODEL // LANES


def _store_row_tiles(ref, val, n):
    for s in range(ROW_TILE):
        ref[pl.ds(s, n, stride=ROW_TILE), :] = val[:, s * LANES:(s + 1) * LANES]


def _load_row_tiles(ref, n):
    return jnp.concatenate([ref[pl.ds(s, n, stride=ROW_TILE), :] for s in range(ROW_TILE)], axis=-1)


def _in_proj_kernel(x_ref, g_ref, wm_ref, wkvt_ref, ws_ref, wst_ref,
                    qkv_ref, z_ref, fq_ref, fk_ref, fv_ref, sm_ref, smt_ref, *, kv_t):
    hb = _rms(x_ref[...], g_ref[...]).astype(BF16)
    nt = lambda w: lax.dot_general(w, hb, (((1,), (1,)), ((), ())), preferred_element_type=F32)
    c0 = 0
    for ref, width in ((qkv_ref, CONV_CH), (z_ref, GDN_WIDTH), (fq_ref, FOX_WIDTH)):
        ref[...] = jnp.dot(hb, wm_ref[:, c0:c0 + width], preferred_element_type=F32)
        c0 += width
    if kv_t:
        fk_ref[0] = nt(wkvt_ref[0:FOX_WIDTH, :])
        fv_ref[0] = nt(wkvt_ref[FOX_WIDTH:, :])
    else:
        fk_ref[...] = jnp.dot(hb, wm_ref[:, c0:c0 + FOX_WIDTH], preferred_element_type=F32)
        fv_ref[...] = jnp.dot(hb, wm_ref[:, c0 + FOX_WIDTH:], preferred_element_type=F32)
    sm_ref[...] = jnp.dot(hb, ws_ref[...], preferred_element_type=F32)
    smt_ref[...] = nt(wst_ref[...])


def _in_proj(x2, g, wm, wkvt, ws, wst, tm, kv_t_seq):
    t = x2.shape[0]
    row = lambda w: pl.BlockSpec((tm, w), lambda i: (i, 0))
    full = lambda a: pl.BlockSpec(a.shape, lambda i: (0,) * a.ndim)
    if kv_t_seq:
        per_b = kv_t_seq // tm
        kv_spec = pl.BlockSpec((1, FOX_WIDTH, tm), lambda i: (i // per_b, 0, i % per_b))
        kv_shape = jax.ShapeDtypeStruct((t // kv_t_seq, FOX_WIDTH, kv_t_seq), F32)
    else:
        kv_spec = row(FOX_WIDTH)
        kv_shape = jax.ShapeDtypeStruct((t, FOX_WIDTH), F32)
    return pl.pallas_call(
        functools.partial(_in_proj_kernel, kv_t=bool(kv_t_seq)),
        grid=(t // tm,),
        in_specs=[row(D_MODEL), full(g), full(wm), full(wkvt), full(ws), full(wst)],
        out_specs=[row(CONV_CH), row(GDN_WIDTH), row(FOX_WIDTH), kv_spec, kv_spec,
                   row(LANES), pl.BlockSpec((16, tm), lambda i: (0, i))],
        out_shape=[jax.ShapeDtypeStruct((t, CONV_CH), F32), jax.ShapeDtypeStruct((t, GDN_WIDTH), F32),
                   jax.ShapeDtypeStruct((t, FOX_WIDTH), F32), kv_shape, kv_shape,
                   jax.ShapeDtypeStruct((t, LANES), F32), jax.ShapeDtypeStruct((16, t), F32)],
        compiler_params=_cparams(("parallel",)),
    )(x2, g, wm, wkvt, ws, wst)


def _norm_proj_kernel(x_ref, g_ref, w_ref, *out_refs):
    hb = _rms(x_ref[...], g_ref[...]).astype(BF16)
    c0 = 0
    for ref in out_refs:
        width = ref.shape[-1]
        ref[...] = jnp.dot(hb, w_ref[:, c0:c0 + width], preferred_element_type=F32)
        c0 += width


def _norm_proj(x2, g, w, widths, tm):
    t = x2.shape[0]
    return pl.pallas_call(
        _norm_proj_kernel,
        grid=(t // tm,),
        in_specs=[pl.BlockSpec((tm, D_MODEL), lambda i: (i, 0)),
                  pl.BlockSpec(g.shape, lambda i: (0, 0)),
                  pl.BlockSpec(w.shape, lambda i: (0, 0))],
        out_specs=[pl.BlockSpec((tm, wd), lambda i: (i, 0)) for wd in widths],
        out_shape=[jax.ShapeDtypeStruct((t, wd), F32) for wd in widths],
        compiler_params=_cparams(("parallel",)),
    )(x2, g, w)


def _small_mm(a, b):
    acc = a[:, 0:1] * b[0:1, :]
    for i in range(1, a.shape[1]):
        acc = acc + a[:, i:i + 1] * b[i:i + 1, :]
    return acc


def _logf_kernel(smt_ref, fbc_ref, logf_ref, ct_ref, *, seq, chunk):
    lft = _log_sigmoid(smt_ref[0][8:16, :] + fbc_ref[8:16, 0:1])
    logf_ref[0] = lft
    tri = (_iota2((chunk, chunk), 0) <= _iota2((chunk, chunk), 1)).astype(F32)
    carry = jnp.zeros((FOX_HEADS, 1), F32)
    for c in range(seq // chunk):
        blk = lft[:, c * chunk:(c + 1) * chunk]
        cs = (_small_mm(blk, tri) if chunk <= SUBLANES else _dot_hi(blk, tri)) + carry
        ct_ref[0, :, c * chunk:(c + 1) * chunk] = cs
        carry = cs[:, chunk - 1:chunk]


def _logf(smt3, fbc):
    b, _, seq = smt3.shape
    chunk = min(256, seq)
    return pl.pallas_call(
        functools.partial(_logf_kernel, seq=seq, chunk=chunk),
        grid=(b,),
        in_specs=[pl.BlockSpec((1, 16, seq), lambda i: (i, 0, 0)),
                  pl.BlockSpec(fbc.shape, lambda i: (0, 0))],
        out_specs=[pl.BlockSpec((1, FOX_HEADS, seq), lambda i: (i, 0, 0)),
                   pl.BlockSpec((1, FOX_HEADS, seq), lambda i: (i, 0, 0))],
        out_shape=[jax.ShapeDtypeStruct((b, FOX_HEADS, seq), F32),
                   jax.ShapeDtypeStruct((b, FOX_HEADS, seq), F32)],
        compiler_params=_cparams(("parallel",)),
    )(smt3, fbc)


def _unit_lower_inverse(a, c, mm):
    eye = (_iota2((c, c), 0) == _iota2((c, c), 1)).astype(F32)
    n = -a
    t = eye + n
    p = n
    levels = int(math.log2(c))
    for _ in range(levels - 1):
        p = mm(p, p)
        t = t + mm(t, p)
    return t


def _gdn_kernel(qkv_ref, z_ref, sm_ref, smt_ref, cst_ref, s0_ref, cw_ref, gpr_ref, gpc_ref,
                o_ref, sn_ref, cb_ref, s_scr, tail_scr, *, tl, c, exact_small):
    t = pl.program_id(1)
    nt = pl.num_programs(1)

    @pl.when(t == 0)
    def _():
        s_scr[...] = s0_ref[0]
        tail_scr[...] = cst_ref[0]

    rb = (lambda v: v.astype(BF16).astype(F32)) if exact_small else (lambda v: v.astype(BF16))
    bdims = ((0,), (0,))
    mm = lambda a, b: lax.dot_general(rb(a), rb(b), (((2,), (1,)), bdims), preferred_element_type=F32)
    mm_nt = lambda a, b: lax.dot_general(rb(a), rb(b), (((2,), (2,)), bdims), preferred_element_type=F32)
    mm_tn = lambda a, b: lax.dot_general(rb(a), rb(b), (((1,), (1,)), bdims), preferred_element_type=F32)

    x = qkv_ref[0]
    tail = tail_scr[...]
    cw = cw_ref[...]
    row8 = _iota2((SUBLANES, CONV_CH), 0)
    acc = x * cw[CONV_W - 1:CONV_W, :]
    for s in range(1, CONV_W):
        xs = pltpu.roll(x, s, 0)
        head = jnp.where(row8 < s, pltpu.roll(tail, s, 0), xs[0:SUBLANES])
        xs = head if tl == SUBLANES else jnp.concatenate([head, xs[SUBLANES:]], axis=0)
        acc = acc + xs * cw[CONV_W - 1 - s:CONV_W - s, :]
    conv = acc * _sigmoid(acc)
    tail_scr[...] = x[tl - SUBLANES:tl, :]

    @pl.when(t == nt - 1)
    def _():
        cb_ref[0] = x[tl - (CONV_W - 1):tl, :]

    sm = sm_ref[0]
    beta_c = _sigmoid(sm)
    g_c = -jnp.exp(gpr_ref[0:1, :]) * _softplus(sm + gpr_ref[1:2, :])
    smt = smt_ref[0]
    g_r = -jnp.exp(gpc_ref[:, 0:1]) * _softplus(smt + gpc_ref[:, 1:2])
    onorm = gpr_ref[2:3, :]

    ii = _iota2((c, c), 0)
    jj = _iota2((c, c), 1)
    tri_c = (jj <= ii).astype(F32)
    tri_r = (ii <= jj).astype(F32)
    small = c <= SUBLANES
    nc = tl // c
    cum_cols, cum_rows = [], []
    for ci in range(nc):
        gc_blk = g_c[ci * c:(ci + 1) * c, :]
        gr_blk = g_r[:, ci * c:(ci + 1) * c]
        cum_cols.append(_small_mm(tri_c, gc_blk) if small else _dot_hi(tri_c, gc_blk))
        cum_rows.append(_small_mm(gr_blk, tri_r) if small else _dot_hi(gr_blk, tri_r))

    def per_pair(fn):
        return jnp.stack([fn(ci, h) for ci in range(nc) for h in range(GDN_HEADS)], axis=0)

    rows = lambda ci: slice(ci * c, (ci + 1) * c)
    qs = per_pair(lambda ci, h: conv[rows(ci), h * GDN_DK:(h + 1) * GDN_DK])
    ks = per_pair(lambda ci, h: conv[rows(ci), GDN_WIDTH + h * GDN_DK:GDN_WIDTH + (h + 1) * GDN_DK])
    vs = per_pair(lambda ci, h: conv[rows(ci), 2 * GDN_WIDTH + h * GDN_DV:2 * GDN_WIDTH + (h + 1) * GDN_DV])
    beta = per_pair(lambda ci, h: beta_c[rows(ci), h:h + 1])
    cum_c = per_pair(lambda ci, h: cum_cols[ci][:, GDN_HEADS + h:GDN_HEADS + h + 1])
    cum_r = per_pair(lambda ci, h: cum_rows[ci][GDN_HEADS + h:GDN_HEADS + h + 1, :])
    qs = qs * lax.rsqrt(jnp.sum(qs * qs, axis=-1, keepdims=True) + NORM_EPS) * (GDN_DK ** -0.5)
    ks = ks * lax.rsqrt(jnp.sum(ks * ks, axis=-1, keepdims=True) + NORM_EPS)
    cum_last = cum_c[:, c - 1:c, :]
    dec = jnp.exp(jnp.where(ii >= jj, cum_c - cum_r, NEG_BIG))
    dec_strict = jnp.where(ii > jj, dec, 0.0)
    a_mat = beta * mm_nt(ks, ks) * dec_strict
    tinv = _unit_lower_inverse(a_mat, c, mm)
    e_cum = jnp.exp(cum_c)
    rhs = jnp.concatenate([beta * vs, (beta * e_cum) * ks], axis=-1)
    sol = mm(tinv, rhs)
    w_v, w_k = sol[:, :, :GDN_DV], sol[:, :, GDN_DV:]
    p_qk = mm_nt(qs, ks) * dec
    q_g = qs * e_cum
    k_d = ks * jnp.exp(cum_last - cum_c)
    g_end = jnp.exp(cum_last)

    state = s_scr[...]
    for ci in range(nc):
        pr = slice(ci * GDN_HEADS, (ci + 1) * GDN_HEADS)
        u = w_v[pr] - mm(w_k[pr], state)
        o = mm(q_g[pr], state) + mm(p_qk[pr], u)
        state = g_end[pr] * state + mm_tn(k_d[pr], u)
        for h in range(GDN_HEADS):
            lo = h * GDN_DV
            zh = z_ref[0, rows(ci), lo:lo + GDN_DV]
            o_ref[0, rows(ci), lo:lo + GDN_DV] = _rms(o[h], onorm) * (zh * _sigmoid(zh))
    s_scr[...] = state

    @pl.when(t == nt - 1)
    def _():
        sn_ref[0] = state


def _gdn(qkv3, z3, sm3, smt3, cst8, s0, cw, gpr, gpc, tl, c, exact_small):
    b, seq, _ = qkv3.shape
    nt = seq // tl
    full2 = lambda a: pl.BlockSpec(a.shape, lambda i, j: (0, 0))
    return pl.pallas_call(
        functools.partial(_gdn_kernel, tl=tl, c=c, exact_small=exact_small),
        grid=(b, nt),
        in_specs=[pl.BlockSpec((1, tl, CONV_CH), lambda i, j: (i, j, 0)),
                  pl.BlockSpec((1, tl, GDN_WIDTH), lambda i, j: (i, j, 0)),
                  pl.BlockSpec((1, tl, LANES), lambda i, j: (i, j, 0)),
                  pl.BlockSpec((1, 16, tl), lambda i, j: (i, 0, j)),
                  pl.BlockSpec((1, SUBLANES, CONV_CH), lambda i, j: (i, 0, 0)),
                  pl.BlockSpec((1, GDN_HEADS, GDN_DK, GDN_DV), lambda i, j: (i, 0, 0, 0)),
                  full2(cw), full2(gpr), full2(gpc)],
        out_specs=[pl.BlockSpec((1, tl, GDN_WIDTH), lambda i, j: (i, j, 0)),
                   pl.BlockSpec((1, GDN_HEADS, GDN_DK, GDN_DV), lambda i, j: (i, 0, 0, 0)),
                   pl.BlockSpec((1, CONV_W - 1, CONV_CH), lambda i, j: (i, 0, 0))],
        out_shape=[jax.ShapeDtypeStruct((b, seq, GDN_WIDTH), F32),
                   jax.ShapeDtypeStruct((b, GDN_HEADS, GDN_DK, GDN_DV), F32),
                   jax.ShapeDtypeStruct((b, CONV_W - 1, CONV_CH), F32)],
        scratch_shapes=[pltpu.VMEM((GDN_HEADS, GDN_DK, GDN_DV), F32),
                        pltpu.VMEM((SUBLANES, CONV_CH), F32)],
        compiler_params=_cparams(("parallel", "arbitrary")),
    )(qkv3, z3, sm3, smt3, cst8, s0, cw, gpr, gpc)


FOX_HEAD_GROUP = 2


def _fox_prompt_kernel(fq_ref, fkt_ref, fvt_ref, ct_ref, on_ref, o_ref, kb_scr, vb_scr, *, tq):
    qi = pl.program_id(1)
    nk = kb_scr.shape[0]

    @pl.when(qi == 0)
    def _():
        for j in range(nk):
            kb_scr[j] = fkt_ref[0, :, j * tq:(j + 1) * tq].astype(BF16)
            vb_scr[j] = fvt_ref[0, :, j * tq:(j + 1) * tq].astype(BF16)

    q_all = fq_ref[0] * (FOX_HEAD_DIM ** -0.5)
    onorm = on_ref[...]
    causal = _iota2((tq, tq), 1) <= _iota2((tq, tq), 0)
    outs = []
    for g0 in range(0, FOX_HEADS, FOX_HEAD_GROUP):
        heads = range(g0, g0 + FOX_HEAD_GROUP)
        qs = [q_all[:, h * FOX_HEAD_DIM:(h + 1) * FOX_HEAD_DIM].astype(BF16) for h in heads]

        def step(j, carry, masked, heads=heads, qs=qs):
            new = []
            for (m, l, acc), h, qh in zip(carry, heads, qs):
                kt = kb_scr[j, h * FOX_HEAD_DIM:(h + 1) * FOX_HEAD_DIM, :]
                vt = vb_scr[j, h * FOX_HEAD_DIM:(h + 1) * FOX_HEAD_DIM, :]
                s = jnp.dot(qh, kt, preferred_element_type=F32) - ct_ref[0, h, pl.ds(j, 1), :]
                if masked:
                    s = jnp.where(causal, s, NEG_BIG)
                m_new = jnp.maximum(m, jnp.max(s, axis=-1, keepdims=True))
                p = jnp.exp(s - m_new)
                alpha = jnp.exp(m - m_new)
                l = alpha * l + jnp.sum(p, axis=-1, keepdims=True)
                acc = alpha * acc + lax.dot_general(p.astype(BF16), vt, (((1,), (1,)), ((), ())),
                                                    preferred_element_type=F32)
                new.append((m_new, l, acc))
            return tuple(new)

        init = tuple((jnp.full((tq, 1), NEG_BIG, F32), jnp.zeros((tq, 1), F32),
                      jnp.zeros((tq, FOX_HEAD_DIM), F32)) for _ in heads)
        carry = lax.fori_loop(0, qi, functools.partial(step, masked=False), init)
        for m, l, acc in step(qi, carry, True):
            outs.append(_rms(acc / l, onorm))
    o_ref[0] = jnp.concatenate(outs, axis=-1)


def _fox_prompt(fq3, fkt, fvt, ct4, onorm, tq):
    b, seq, _ = fq3.shape
    nq = seq // tq
    return pl.pallas_call(
        functools.partial(_fox_prompt_kernel, tq=tq),
        grid=(b, nq),
        in_specs=[pl.BlockSpec((1, tq, FOX_WIDTH), lambda i, j: (i, j, 0)),
                  pl.BlockSpec((1, FOX_WIDTH, seq), lambda i, j: (i, 0, 0)),
                  pl.BlockSpec((1, FOX_WIDTH, seq), lambda i, j: (i, 0, 0)),
                  pl.BlockSpec((1, FOX_HEADS, nq, tq), lambda i, j: (i, 0, 0, 0)),
                  pl.BlockSpec(onorm.shape, lambda i, j: (0, 0))],
        out_specs=pl.BlockSpec((1, tq, FOX_WIDTH), lambda i, j: (i, j, 0)),
        out_shape=jax.ShapeDtypeStruct((b, seq, FOX_WIDTH), F32),
        scratch_shapes=[pltpu.VMEM((nq, FOX_WIDTH, tq), BF16), pltpu.VMEM((nq, FOX_WIDTH, tq), BF16)],
        compiler_params=_cparams(("parallel", "arbitrary")),
    )(fq3, fkt, fvt, ct4, onorm)


def _logf_pages_kernel(lf_ref, w_ref, o_ref):
    lf = lf_ref[...]
    a = lf.astype(BF16)
    r1 = lf - a.astype(F32)
    b = r1.astype(BF16)
    c = (r1 - b.astype(F32)).astype(BF16)
    w = w_ref[...]
    o_ref[...] = (jnp.dot(a, w, preferred_element_type=F32) + jnp.dot(b, w, preferred_element_type=F32)
                  + jnp.dot(c, w, preferred_element_type=F32))


def _logf_pages(lf_flat, w2):
    n_pool, width = lf_flat.shape
    tp = next(c for c in (2048, 1024, 512, 256, 128, 64, 32, 16, 8) if n_pool % c == 0)
    return pl.pallas_call(
        _logf_pages_kernel,
        grid=(n_pool // tp,),
        in_specs=[pl.BlockSpec((tp, width), lambda i: (i, 0)),
                  pl.BlockSpec(w2.shape, lambda i: (0, 0))],
        out_specs=pl.BlockSpec((tp, 2 * width), lambda i: (i, 0)),
        out_shape=jax.ShapeDtypeStruct((n_pool, 2 * width), F32),
        compiler_params=_cparams(("parallel",)),
    )(lf_flat, w2)


def _fox_sample_kernel(pt_ref, qbd_ref, kn_ref, vn_ref, cn_ref, on_ref, *rest, pps, lq):
    k_refs = rest[0:pps]
    v_refs = rest[pps:2 * pps]
    r_refs = rest[2 * pps:3 * pps]
    o_ref = rest[3 * pps]
    m_scr, l_scr, acc_scr, suf_scr = rest[3 * pps + 1:]
    step = pl.program_id(1)
    nstep = pl.num_programs(1)
    rows = lq * FOX_HEADS
    qbd = qbd_ref[0]
    tile_rows = lambda v: jnp.concatenate([v] * lq, axis=0)

    @pl.when(step == 0)
    def _():
        s = lax.dot_general(qbd, kn_ref[0].astype(BF16), (((1,), (1,)), ((), ())),
                            preferred_element_type=F32)
        s = s - tile_rows(cn_ref[0])
        s = jnp.where(_iota2((rows, lq), 1) <= _iota2((rows, lq), 0) // FOX_HEADS, s, NEG_BIG)
        m = jnp.max(s, axis=-1, keepdims=True)
        p = jnp.exp(s - m)
        m_scr[...] = m
        l_scr[...] = jnp.sum(p, axis=-1, keepdims=True)
        acc_scr[...] = jnp.dot(p.astype(BF16), vn_ref[0].astype(BF16), preferred_element_type=F32)
        suf_scr[...] = jnp.zeros_like(suf_scr)

    page = k_refs[0].shape[3]
    suf = suf_scr[...]
    scores = []
    for i in range(pps):
        r2 = r_refs[i][0]
        s = jnp.dot(qbd, k_refs[i][0, 0].astype(BF16), preferred_element_type=F32)
        scores.append(s + tile_rows(r2[:, :page] + suf))
        suf = suf + r2[:, page:]
    suf_scr[...] = suf
    m = m_scr[...]
    m_new = m
    for s in scores:
        m_new = jnp.maximum(m_new, jnp.max(s, axis=-1, keepdims=True))
    alpha = jnp.exp(m - m_new)
    l = alpha * l_scr[...]
    acc = alpha * acc_scr[...]
    for i in range(pps):
        p = jnp.exp(scores[i] - m_new)
        l = l + jnp.sum(p, axis=-1, keepdims=True)
        acc = acc + lax.dot_general(p.astype(BF16), v_refs[i][0, 0].astype(BF16), (((1,), (1,)), ((), ())),
                                    preferred_element_type=F32)
    m_scr[...] = m_new
    l_scr[...] = l
    acc_scr[...] = acc

    @pl.when(step == nstep - 1)
    def _():
        own = _iota2((rows, FOX_WIDTH), 1) // FOX_HEAD_DIM == _iota2((rows, FOX_WIDTH), 0) % FOX_HEADS
        o = jnp.where(own, acc / l, 0.0)
        ms = jnp.sum(o * o, axis=-1, keepdims=True) * (1.0 / FOX_HEAD_DIM)
        o = o * lax.rsqrt(ms + NORM_EPS) * on_ref[...]
        o_ref[0] = jnp.sum(o.reshape(lq, FOX_HEADS, FOX_WIDTH), axis=1)


def _fox_sample(page_table, layer, qbd, kn3, vn3, cn3, onorm, ckt, cvt, r3, pps):
    b, lq, _ = kn3.shape
    n_pages = page_table.shape[1]
    page = ckt.shape[3]
    nstep = n_pages // pps
    rows = FOX_HEADS * lq

    def page_map(i):
        return lambda bi, s, pt: (layer, pt[bi, n_pages - 1 - (s * pps + i)], 0, 0)

    def r_map(i):
        return lambda bi, s, pt: (pt[bi, n_pages - 1 - (s * pps + i)], 0, 0)

    in_specs = [pl.BlockSpec((1, rows, FOX_WIDTH), lambda bi, s, pt: (bi, 0, 0)),
                pl.BlockSpec((1, lq, FOX_WIDTH), lambda bi, s, pt: (bi, 0, 0)),
                pl.BlockSpec((1, lq, FOX_WIDTH), lambda bi, s, pt: (bi, 0, 0)),
                pl.BlockSpec((1, FOX_HEADS, lq), lambda bi, s, pt: (bi, 0, 0)),
                pl.BlockSpec(onorm.shape, lambda bi, s, pt: (0, 0))]
    in_specs += [pl.BlockSpec((1, 1, FOX_WIDTH, page), page_map(i)) for i in range(pps)]
    in_specs += [pl.BlockSpec((1, 1, FOX_WIDTH, page), page_map(i)) for i in range(pps)]
    in_specs += [pl.BlockSpec((1, FOX_HEADS, 2 * page), r_map(i)) for i in range(pps)]
    grid_spec = pltpu.PrefetchScalarGridSpec(
        num_scalar_prefetch=1,
        grid=(b, nstep),
        in_specs=in_specs,
        out_specs=pl.BlockSpec((1, lq, FOX_WIDTH), lambda bi, s, pt: (bi, 0, 0)),
        scratch_shapes=[pltpu.VMEM((rows, 1), F32), pltpu.VMEM((rows, 1), F32),
                        pltpu.VMEM((rows, FOX_WIDTH), F32), pltpu.VMEM((FOX_HEADS, page), F32)],
    )
    return pl.pallas_call(
        functools.partial(_fox_sample_kernel, pps=pps, lq=lq),
        grid_spec=grid_spec,
        out_shape=jax.ShapeDtypeStruct((b, lq, FOX_WIDTH), F32),
        compiler_params=_cparams(("parallel", "arbitrary")),
    )(page_table, qbd, kn3, vn3, cn3, onorm, *([ckt] * pps), *([cvt] * pps), *([r3] * pps))


def _out_proj_kernel(x_ref, a_ref, b_ref, w_ref, o_ref):
    o_ref[...] = (x_ref[...]
                  + jnp.dot(a_ref[...].astype(BF16), w_ref[0:GDN_WIDTH, :], preferred_element_type=F32)
                  + jnp.dot(b_ref[...].astype(BF16), w_ref[GDN_WIDTH:, :], preferred_element_type=F32))


def _out_proj(x2, a2, b2, w, tm):
    t = x2.shape[0]
    return pl.pallas_call(
        _out_proj_kernel,
        grid=(t // tm,),
        in_specs=[pl.BlockSpec((tm, D_MODEL), lambda i: (i, 0)),
                  pl.BlockSpec((tm, GDN_WIDTH), lambda i: (i, 0)),
                  pl.BlockSpec((tm, FOX_WIDTH), lambda i: (i, 0)),
                  pl.BlockSpec(w.shape, lambda i: (0, 0))],
        out_specs=pl.BlockSpec((tm, D_MODEL), lambda i: (i, 0)),
        out_shape=jax.ShapeDtypeStruct((t, D_MODEL), F32),
        compiler_params=_cparams(("parallel",)),
    )(x2, a2, b2, w)


def _xattn_kernel(x_ref, g_ref, wq_ref, wo_ref, mk_ref, mv_ref, o_ref):
    x = x_ref[0]
    hb = _rms(x, g_ref[...]).astype(BF16)
    q = jnp.dot(hb, wq_ref[...], preferred_element_type=F32) * (XA_HEAD_DIM ** -0.5)
    mk = mk_ref[0].astype(BF16)
    mv = mv_ref[0].astype(BF16)
    outs = []
    for h in range(XA_HEADS):
        lo = h * XA_HEAD_DIM
        s = lax.dot_general(q[:, lo:lo + XA_HEAD_DIM].astype(BF16), mk[:, lo:lo + XA_HEAD_DIM],
                            (((1,), (1,)), ((), ())), preferred_element_type=F32)
        p = jnp.exp(s - jnp.max(s, axis=-1, keepdims=True))
        p = p / jnp.sum(p, axis=-1, keepdims=True)
        outs.append(jnp.dot(p.astype(BF16), mv[:, lo:lo + XA_HEAD_DIM], preferred_element_type=F32))
    o = jnp.concatenate(outs, axis=-1).astype(BF16)
    o_ref[0] = x + jnp.dot(o, wo_ref[...], preferred_element_type=F32)


def _xattn(x3, g, wq, wo, mk3, mv3, tq):
    b, seq, _ = x3.shape
    n_mem = mk3.shape[1]
    return pl.pallas_call(
        _xattn_kernel,
        grid=(b, seq // tq),
        in_specs=[pl.BlockSpec((1, tq, D_MODEL), lambda i, j: (i, j, 0)),
                  pl.BlockSpec(g.shape, lambda i, j: (0, 0)),
                  pl.BlockSpec(wq.shape, lambda i, j: (0, 0)),
                  pl.BlockSpec(wo.shape, lambda i, j: (0, 0)),
                  pl.BlockSpec((1, n_mem, XA_WIDTH), lambda i, j: (i, 0, 0)),
                  pl.BlockSpec((1, n_mem, XA_WIDTH), lambda i, j: (i, 0, 0))],
        out_specs=pl.BlockSpec((1, tq, D_MODEL), lambda i, j: (i, j, 0)),
        out_shape=jax.ShapeDtypeStruct((b, seq, D_MODEL), F32),
        compiler_params=_cparams(("parallel", "parallel")),
    )(x3, g, wq, wo, mk3, mv3)


def _router_kernel(x_ref, g_ref, wr_ref, br_ref, h_ref, route_ref, gate_ref, cnt_ref, base_scr, *, tm):
    i = pl.program_id(0)

    @pl.when(i == 0)
    def _():
        base_scr[...] = jnp.zeros_like(base_scr)

    h = _rms(x_ref[...], g_ref[...])
    _store_row_tiles(h_ref, h, tm)
    logits = _dot_hi(h, wr_ref[...]) + br_ref[...]
    lane = _iota2((tm, LANES), 1)
    lane_f = lane.astype(F32)
    vals, hots, idxs = [], [], []
    cur = logits
    for _ in range(TOP_K):
        mx = jnp.max(cur, axis=-1, keepdims=True)
        idx_f = jnp.min(jnp.where(cur == mx, lane_f, float(LANES)), axis=-1, keepdims=True)
        hot = lane_f == idx_f
        vals.append(mx)
        idxs.append(idx_f.astype(I32))
        hots.append(hot)
        cur = jnp.where(hot, -jnp.inf, cur)
    exps = [jnp.exp(v - vals[0]) for v in vals]
    denom = exps[0] + exps[1] + exps[2] + exps[3]
    member = (hots[0] | hots[1] | hots[2] | hots[3]).astype(F32)
    strict = (_iota2((tm, tm), 1) < _iota2((tm, tm), 0)).astype(BF16)
    before = jnp.dot(strict, member.astype(BF16), preferred_element_type=F32) + base_scr[...]
    route = jnp.zeros((tm, LANES), I32)
    gate = jnp.zeros((tm, LANES), F32)
    for k in range(TOP_K):
        rank = jnp.sum(jnp.where(hots[k], before, 0.0), axis=-1, keepdims=True).astype(I32)
        route = jnp.where(lane == k, idxs[k], route)
        route = jnp.where(lane == TOP_K + k, rank, route)
        gate = jnp.where(lane == k, exps[k] / denom, gate)
    route_ref[...] = route
    gate_ref[...] = gate
    base_scr[...] = base_scr[...] + jnp.sum(member, axis=0, keepdims=True)
    cnt_ref[...] = base_scr[...]


def _router(x2, g, wr, br, tm):
    t = x2.shape[0]
    return pl.pallas_call(
        functools.partial(_router_kernel, tm=tm),
        grid=(t // tm,),
        in_specs=[pl.BlockSpec((tm, D_MODEL), lambda i: (i, 0)),
                  pl.BlockSpec(g.shape, lambda i: (0, 0)),
                  pl.BlockSpec(wr.shape, lambda i: (0, 0)),
                  pl.BlockSpec(br.shape, lambda i: (0, 0))],
        out_specs=[pl.BlockSpec((tm * ROW_TILE, LANES), lambda i: (i, 0)),
                   pl.BlockSpec((tm, LANES), lambda i: (i, 0)),
                   pl.BlockSpec((tm, LANES), lambda i: (i, 0)),
                   pl.BlockSpec((1, LANES), lambda i: (0, 0))],
        out_shape=[jax.ShapeDtypeStruct((t * ROW_TILE, LANES), F32), jax.ShapeDtypeStruct((t, LANES), I32),
                   jax.ShapeDtypeStruct((t, LANES), F32), jax.ShapeDtypeStruct((1, LANES), F32)],
        scratch_shapes=[pltpu.VMEM((1, LANES), F32)],
        compiler_params=_cparams(("arbitrary",)),
    )(x2, g, wr, br)


def _dispatch_kernel(dest_ref, h_ref, init_hbm, xs_hbm, sem, *, tm):
    del init_hbm
    def issue(t, c):
        src = pl.multiple_of(t * ROW_TILE, ROW_TILE)
        for k in range(TOP_K):
            dst = pl.multiple_of(dest_ref[0, 0, t * TOP_K + k] * ROW_TILE, ROW_TILE)
            pltpu.make_async_copy(h_ref.at[pl.ds(src, ROW_TILE)], xs_hbm.at[pl.ds(dst, ROW_TILE)], sem).start()
        return c

    lax.fori_loop(0, tm, issue, 0, unroll=2)
    for _ in range(TOP_K):
        pltpu.make_async_copy(h_ref, xs_hbm.at[pl.ds(0, tm * ROW_TILE)], sem).wait()


def _dispatch(dest3, h2, rows, tm):
    t = h2.shape[0] // ROW_TILE
    init = jnp.zeros((rows * ROW_TILE, LANES), F32)
    return pl.pallas_call(
        functools.partial(_dispatch_kernel, tm=tm),
        grid=(t // tm,),
        in_specs=[pl.BlockSpec((1, 1, tm * TOP_K), lambda i: (i, 0, 0), memory_space=pltpu.SMEM),
                  pl.BlockSpec((tm * ROW_TILE, LANES), lambda i: (i, 0)),
                  pl.BlockSpec(memory_space=pl.ANY)],
        out_specs=pl.BlockSpec(memory_space=pl.ANY),
        out_shape=jax.ShapeDtypeStruct((rows * ROW_TILE, LANES), F32),
        scratch_shapes=[pltpu.SemaphoreType.DMA(())],
        input_output_aliases={2: 0},
        compiler_params=pltpu.CompilerParams(dimension_semantics=("arbitrary",), vmem_limit_bytes=VMEM_LIMIT,
                                             has_side_effects=True),
    )(dest3, h2, init)


def _ffn_kernel(be_ref, nv_ref, x_ref, w1_ref, b1_ref, w2_ref, b2_ref, o_ref, w1b_scr, w2b_scr, *, tb):
    i = pl.program_id(0)
    live = i * tb < nv_ref[0]

    @pl.when((i == 0) | (be_ref[i] != be_ref[jnp.maximum(i - 1, 0)]))
    def _():
        w1b_scr[...] = w1_ref[0].astype(BF16)
        w2b_scr[...] = w2_ref[0].astype(BF16)

    @pl.when(live)
    def _():
        x = _load_row_tiles(x_ref, tb).astype(BF16)
        hb = jnp.dot(x, w1b_scr[...], preferred_element_type=F32) + b1_ref[0]
        glu = jnp.minimum(hb[:, :D_FF], SWIGLU_LIMIT)
        lin = jnp.clip(hb[:, D_FF:], -SWIGLU_LIMIT, SWIGLU_LIMIT)
        act = glu * _sigmoid(SWIGLU_ALPHA * glu) * (lin + 1.0)
        y = jnp.dot(act.astype(BF16), w2b_scr[...], preferred_element_type=F32) + b2_ref[0]
        _store_row_tiles(o_ref, y, tb)

    @pl.when(jnp.logical_not(live))
    def _():
        o_ref[...] = jnp.zeros_like(o_ref)


def _ffn(block_e, nvalid, xs, w1, b1, w2, b2, tb):
    rows = xs.shape[0] // ROW_TILE
    nb = rows // tb
    blk = pl.BlockSpec((tb * ROW_TILE, LANES), lambda i, be, nv: (i, 0))
    grid_spec = pltpu.PrefetchScalarGridSpec(
        num_scalar_prefetch=2,
        grid=(nb,),
        in_specs=[blk,
                  pl.BlockSpec((1, D_MODEL, 2 * D_FF), lambda i, be, nv: (be[i], 0, 0)),
                  pl.BlockSpec((1, 1, 2 * D_FF), lambda i, be, nv: (be[i], 0, 0)),
                  pl.BlockSpec((1, D_FF, D_MODEL), lambda i, be, nv: (be[i], 0, 0)),
                  pl.BlockSpec((1, 1, D_MODEL), lambda i, be, nv: (be[i], 0, 0))],
        out_specs=blk,
        scratch_shapes=[pltpu.VMEM((D_MODEL, 2 * D_FF), BF16), pltpu.VMEM((D_FF, D_MODEL), BF16)],
    )
    return pl.pallas_call(
        functools.partial(_ffn_kernel, tb=tb),
        grid_spec=grid_spec,
        out_shape=jax.ShapeDtypeStruct((rows * ROW_TILE, LANES), F32),
        compiler_params=pltpu.CompilerParams(dimension_semantics=("arbitrary",), vmem_limit_bytes=FFN_VMEM_LIMIT),
    )(block_e, nvalid, xs, w1, b1, w2, b2)


def _combine_kernel(dcur_ref, dnext_ref, x_ref, gate_ref, fg_ref, os_hbm, y_ref, yn_ref, buf, sem, *, tm):
    i = pl.program_id(0)
    n = pl.num_programs(0)

    def issue_tile(dref, slot):
        def issue(t, c):
            dst = pl.multiple_of(t * ROW_TILE, ROW_TILE)
            for k in range(TOP_K):
                src = pl.multiple_of(dref[0, 0, t * TOP_K + k] * ROW_TILE, ROW_TILE)
                pltpu.make_async_copy(os_hbm.at[pl.ds(src, ROW_TILE)],
                                      buf.at[slot, k, pl.ds(dst, ROW_TILE)], sem.at[slot]).start()
            return c

        lax.fori_loop(0, tm, issue, 0, unroll=2)

    @pl.when(i == 0)
    def _():
        issue_tile(dcur_ref, 0)

    @pl.when(i + 1 < n)
    def _():
        issue_tile(dnext_ref, (i + 1) % 2)

    slot = i % 2
    for k in range(TOP_K):
        pltpu.make_async_copy(os_hbm.at[pl.ds(0, tm * ROW_TILE)], buf.at[slot, k], sem.at[slot]).wait()
    gate = gate_ref[...]
    x = x_ref[...]
    pieces = []
    for s in range(ROW_TILE):
        acc = x[:, s * LANES:(s + 1) * LANES]
        for k in range(TOP_K):
            acc = acc + gate[:, k:k + 1] * buf[slot, k, pl.ds(s, tm, stride=ROW_TILE), :]
        pieces.append(acc)
    y = jnp.concatenate(pieces, axis=-1)
    y_ref[...] = y
    yn_ref[...] = _rms(y, fg_ref[...])


def _combine(dest3, x2, gate, fg, os2, tm):
    t = x2.shape[0]
    nt = t // tm
    return pl.pallas_call(
        functools.partial(_combine_kernel, tm=tm),
        grid=(nt,),
        in_specs=[pl.BlockSpec((1, 1, tm * TOP_K), lambda i: (i, 0, 0), memory_space=pltpu.SMEM),
                  pl.BlockSpec((1, 1, tm * TOP_K), lambda i: (jnp.minimum(i + 1, nt - 1), 0, 0),
                               memory_space=pltpu.SMEM),
                  pl.BlockSpec((tm, D_MODEL), lambda i: (i, 0)),
                  pl.BlockSpec((tm, LANES), lambda i: (i, 0)),
                  pl.BlockSpec(fg.shape, lambda i: (0, 0)),
                  pl.BlockSpec(memory_space=pl.ANY)],
        out_specs=[pl.BlockSpec((tm, D_MODEL), lambda i: (i, 0)),
                   pl.BlockSpec((tm, D_MODEL), lambda i: (i, 0))],
        out_shape=[jax.ShapeDtypeStruct((t, D_MODEL), F32), jax.ShapeDtypeStruct((t, D_MODEL), F32)],
        scratch_shapes=[pltpu.VMEM((2, TOP_K, tm * ROW_TILE, LANES), F32), pltpu.SemaphoreType.DMA((2,))],
        compiler_params=_cparams(("arbitrary",)),
    )(dest3, dest3, x2, gate, fg, os2)


def _row_tile(t, want):
    tm = min(want, t)
    assert t % tm == 0
    return tm


def _mixer(x3, lw, conv_state8, s0, fox_fn, kv_t):
    b, seq, _ = x3.shape
    t = b * seq
    x2 = x3.reshape(t, D_MODEL)
    qkv, z, fq, fk, fv, sm, smt = _in_proj(x2, lw["norm1_g"], lw["w_main"], lw["w_kv_t"], lw["w_small"],
                                           lw["w_small_t"], _row_tile(t, 256), seq if kv_t else None)
    sm3 = sm.reshape(b, seq, LANES)
    smt3 = jnp.transpose(smt.reshape(16, b, seq), (1, 0, 2))
    logf_t, ct = _logf(smt3, lw["fb_col"])
    logf = jnp.transpose(logf_t, (0, 2, 1))
    c = min(GDN_CHUNK, seq)
    tl = min(256, seq)
    gdn_out, s_new, cbuf = _gdn(qkv.reshape(b, seq, CONV_CH), z.reshape(b, seq, GDN_WIDTH), sm3, smt3,
                                conv_state8, s0, lw["conv_w"], lw["gp_row"], lw["gp_col"], tl, c,
                                exact_small=(c <= SUBLANES))
    fq3 = fq.reshape(b, seq, FOX_WIDTH)
    if kv_t:
        fk3, fv3 = fk, fv
        as_out = lambda a: jnp.transpose(a.reshape(b, FOX_HEADS, FOX_HEAD_DIM, seq), (0, 3, 1, 2))
    else:
        fk3, fv3 = fk.reshape(b, seq, FOX_WIDTH), fv.reshape(b, seq, FOX_WIDTH)
        as_out = lambda a: a.reshape(b, seq, FOX_HEADS, FOX_HEAD_DIM)
    fox_out = fox_fn(fq3, fk3, fv3, ct)
    y2 = _out_proj(x2, gdn_out.reshape(t, GDN_WIDTH), fox_out.reshape(t, FOX_WIDTH), lw["w_out"],
                   _row_tile(t, 512))
    return y2.reshape(b, seq, D_MODEL), s_new, cbuf, as_out(fk3), as_out(fv3), logf


def _moe(x3, lw, final_g):
    b, seq, _ = x3.shape
    t = b * seq
    x2 = x3.reshape(t, D_MODEL)
    h2, route, gate, counts = _router(x2, lw["norm3_g"], lw["w_router"], lw["b_router"], _row_tile(t, 256))
    tb = EXPERT_BLOCK_LARGE if t * TOP_K >= N_EXPERTS * 4 * EXPERT_BLOCK_LARGE else EXPERT_BLOCK
    cnt = counts[0, :N_EXPERTS].astype(I32)
    padded = (cnt + tb - 1) // tb * tb
    pends = jnp.cumsum(padded)
    pstart = pends - padded
    nb = t * TOP_K // tb + N_EXPERTS
    rows = nb * tb
    block_row0 = jnp.arange(nb, dtype=I32) * tb
    block_e = jnp.minimum(jnp.sum((pends[None, :] <= block_row0[:, None]).astype(I32), axis=1), N_EXPERTS - 1)
    nvalid = pends[-1:].astype(I32)
    hot = route[:, :TOP_K, None] == jnp.arange(N_EXPERTS, dtype=I32)[None, None, :]
    dest = jnp.sum(jnp.where(hot, pstart[None, None, :], 0), axis=-1) + route[:, TOP_K:2 * TOP_K]
    tmd = _row_tile(t, 512)
    xs = _dispatch(dest.reshape(t // tmd, 1, tmd * TOP_K), h2, rows, tmd)
    os2 = _ffn(block_e, nvalid, xs, lw["w1"], lw["b1"], lw["w2"], lw["b2"], tb)
    tmc = _row_tile(t, 256)
    y2, yn2 = _combine(dest.reshape(t // tmc, 1, tmc * TOP_K), x2, gate, final_g, os2, tmc)
    return y2.reshape(b, seq, D_MODEL), yn2.reshape(b, seq, D_MODEL)


def _prep_layer(l, norm1_g, w_in, conv_w, gdn_a_log, gdn_dt_bias, gdn_onorm, fox_fbias, fox_onorm, w_out,
                norm2_g, mem_norm_g, w_xq, w_mkv, w_xo, norm3_g, w_router, b_router, w1, b1, w2, b2):
    wi = w_in[l]
    gq, gk, gv, gz, gb, ga, fq, fk, fv, ff = _split_in(wi)
    w_main = jnp.concatenate([gq, gk, gv, gz, fq, fk, fv], axis=1).astype(BF16)
    w_kv_t = jnp.transpose(jnp.concatenate([fk, fv], axis=1)).astype(BF16)
    w_small = jnp.concatenate([gb, ga, ff, jnp.zeros((D_MODEL, LANES - 16), F32)], axis=1).astype(BF16)
    w_small_t = jnp.transpose(w_small[:, :16])
    lanes = lambda v, off: jnp.zeros((LANES,), F32).at[off:off + v.shape[0]].set(v)
    gp_row = jnp.zeros((SUBLANES, LANES), F32)
    gp_row = gp_row.at[0].set(lanes(gdn_a_log[l], GDN_HEADS)).at[1].set(lanes(gdn_dt_bias[l], GDN_HEADS))
    gp_row = gp_row.at[2].set(gdn_onorm[l])
    gp_col = jnp.zeros((16, LANES), F32)
    gp_col = gp_col.at[GDN_HEADS:2 * GDN_HEADS, 0].set(gdn_a_log[l]).at[GDN_HEADS:2 * GDN_HEADS, 1].set(gdn_dt_bias[l])
    fb_col = jnp.zeros((16, LANES), F32).at[8:16, 0].set(fox_fbias[l])
    wr = jnp.concatenate([w_router[l], jnp.zeros((D_MODEL, LANES - N_EXPERTS), F32)], axis=1)
    br = jnp.full((1, LANES), NEG_BIG, F32).at[0, :N_EXPERTS].set(b_router[l])
    return {
        "norm1_g": norm1_g[l].reshape(1, D_MODEL), "w_main": w_main, "w_kv_t": w_kv_t,
        "w_small": w_small, "w_small_t": w_small_t,
        "conv_w": conv_w[l], "gp_row": gp_row, "gp_col": gp_col, "fb_col": fb_col,
        "fox_onorm": fox_onorm[l].reshape(1, FOX_HEAD_DIM), "w_out": w_out[l].astype(BF16),
        "norm2_g": norm2_g[l].reshape(1, D_MODEL), "mem_norm_g": mem_norm_g[l].reshape(1, D_MODEL),
        "w_xq": w_xq[l].astype(BF16), "w_mkv": w_mkv[l].astype(BF16), "w_xo": w_xo[l].astype(BF16),
        "norm3_g": norm3_g[l].reshape(1, D_MODEL), "w_router": wr, "b_router": br,
        "w1": w1[l], "b1": b1[l].reshape(N_EXPERTS, 1, 2 * D_FF),
        "w2": w2[l], "b2": b2[l].reshape(N_EXPERTS, 1, D_MODEL),
    }


def _split_in(wi):
    widths = (GDN_WIDTH, GDN_WIDTH, GDN_WIDTH, GDN_WIDTH, GDN_HEADS, GDN_HEADS,
              FOX_WIDTH, FOX_WIDTH, FOX_WIDTH, FOX_HEADS)
    outs, c0 = [], 0
    for w in widths:
        outs.append(wi[:, c0:c0 + w])
        c0 += w
    return outs


def kernel(x_prompt, x_sample, mem_prompt, cache_fox_k, cache_fox_v, cache_fox_logf, cache_mem_k, cache_mem_v, state_gdn, state_conv, page_table, norm1_g, w_in, conv_w, gdn_a_log, gdn_dt_bias, gdn_onorm, fox_fbias, fox_onorm, w_out, norm2_g, mem_norm_g, w_xq, w_mkv, w_xo, norm3_g, w_router, b_router, w1, b1, w2, b2, final_norm_g):
    depth = w_in.shape[0]
    bp, lp, _ = x_prompt.shape
    bs, ls, _ = x_sample.shape
    n_mem = mem_prompt.shape[1]
    n_pool, page = cache_fox_k.shape[1], cache_fox_k.shape[2]
    ckt = jnp.transpose(cache_fox_k, (0, 1, 3, 4, 2)).reshape(depth, n_pool, FOX_WIDTH, page)
    cvt = jnp.transpose(cache_fox_v, (0, 1, 3, 4, 2)).reshape(depth, n_pool, FOX_WIDTH, page)
    lf_rows = jnp.transpose(cache_fox_logf, (0, 1, 3, 2)).reshape(depth, n_pool * FOX_HEADS, page)
    tok = jnp.arange(page, dtype=I32)
    later = tok[:, None] > tok[None, :]
    w_suffix = jnp.concatenate([later, jnp.ones_like(later)], axis=1).astype(BF16)
    head_mask = (jnp.arange(FOX_HEADS)[:, None] == jnp.arange(FOX_HEADS)[None, :]).astype(F32)
    final_g = final_norm_g.reshape(1, D_MODEL)
    xp, xs = x_prompt, x_sample
    yp = ys = None
    outs = {k: [] for k in ("fkp", "fvp", "flp", "mkp", "mvp", "sgp", "scp", "fks", "fvs", "fls", "sgs", "scs")}
    for l in range(depth):
        lw = _prep_layer(l, norm1_g, w_in, conv_w, gdn_a_log, gdn_dt_bias, gdn_onorm, fox_fbias, fox_onorm,
                         w_out, norm2_g, mem_norm_g, w_xq, w_mkv, w_xo, norm3_g, w_router, b_router,
                         w1, b1, w2, b2)
        tq = min(512, lp)

        def fox_p(fq3, fk3, fv3, ct, lw=lw, tq=tq):
            return _fox_prompt(fq3, fk3, fv3, ct.reshape(bp, FOX_HEADS, lp // tq, tq), lw["fox_onorm"], tq)

        xp, s_new, cbuf, k_new, v_new, lf_new = _mixer(
            xp, lw, jnp.zeros((bp, SUBLANES, CONV_CH), F32),
            jnp.zeros((bp, GDN_HEADS, GDN_DK, GDN_DV), F32), fox_p, kv_t=True)
        outs["fkp"].append(k_new); outs["fvp"].append(v_new); outs["flp"].append(lf_new)
        outs["sgp"].append(s_new); outs["scp"].append(cbuf)

        r3 = _logf_pages(lf_rows[l], w_suffix).reshape(n_pool, FOX_HEADS, 2 * page)
        on_tiled = jnp.tile(lw["fox_onorm"], (1, FOX_HEADS))

        def fox_s(fq3, fk3, fv3, ct, lw=lw, l=l, r3=r3, on_tiled=on_tiled):
            q4 = fq3.reshape(bs, ls, FOX_HEADS, FOX_HEAD_DIM) * (FOX_HEAD_DIM ** -0.5)
            qbd = jnp.einsum("bqhd,hg->bqhgd", q4, head_mask).reshape(bs, ls * FOX_HEADS, FOX_WIDTH)
            return _fox_sample(page_table, l, qbd.astype(BF16), fk3, fv3, ct, on_tiled, ckt, cvt, r3,
                               pps=math.gcd(page_table.shape[1], 32))

        cst8 = jnp.pad(state_conv[l], ((0, 0), (SUBLANES - (CONV_W - 1), 0), (0, 0)))
        xs, s_new, cbuf, k_new, v_new, lf_new = _mixer(xs, lw, cst8, state_gdn[l], fox_s, kv_t=False)
        outs["fks"].append(k_new); outs["fvs"].append(v_new); outs["fls"].append(lf_new)
        outs["sgs"].append(s_new); outs["scs"].append(cbuf)

        mk2, mv2 = _norm_proj(mem_prompt.reshape(bp * n_mem, D_MODEL), lw["mem_norm_g"], lw["w_mkv"],
                              (XA_WIDTH, XA_WIDTH), _row_tile(bp * n_mem, 512))
        outs["mkp"].append(mk2.reshape(bp, n_mem, XA_HEADS, XA_HEAD_DIM))
        outs["mvp"].append(mv2.reshape(bp, n_mem, XA_HEADS, XA_HEAD_DIM))
        xp = _xattn(xp, lw["norm2_g"], lw["w_xq"], lw["w_xo"], mk2.reshape(bp, n_mem, XA_WIDTH),
                    mv2.reshape(bp, n_mem, XA_WIDTH), min(512, lp))
        xs = _xattn(xs, lw["norm2_g"], lw["w_xq"], lw["w_xo"], cache_mem_k[l].reshape(bs, n_mem, XA_WIDTH),
                    cache_mem_v[l].reshape(bs, n_mem, XA_WIDTH), ls)

        xp, yp = _moe(xp, lw, final_g)
        xs, ys = _moe(xs, lw, final_g)
    st = jnp.stack
    return (yp, ys, st(outs["fkp"]), st(outs["fvp"]), st(outs["flp"]), st(outs["mkp"]), st(outs["mvp"]),
            st(outs["sgp"]), st(outs["scp"]), st(outs["fks"]), st(outs["fvs"]), st(outs["fls"]),
            st(outs["sgs"]), st(outs["scs"]))
```

```python
import functools
import math

import jax
import jax.numpy as jnp
from jax import lax
from jax.experimental import pallas as pl
from jax.experimental.pallas import tpu as pltpu

F32 = jnp.float32
BF16 = jnp.bfloat16
I32 = jnp.int32
HIGHEST = lax.Precision.HIGHEST

D_MODEL = 1024
GDN_HEADS = 4
GDN_DK = 128
GDN_DV = 128
GDN_WIDTH = GDN_HEADS * GDN_DV
CONV_W = 4
CONV_CH = 3 * GDN_WIDTH
GDN_CHUNK = 64
FOX_HEADS = 8
FOX_HEAD_DIM = 64
FOX_WIDTH = FOX_HEADS * FOX_HEAD_DIM
XA_HEADS = 4
XA_HEAD_DIM = 128
XA_WIDTH = XA_HEADS * XA_HEAD_DIM
N_EXPERTS = 32
TOP_K = 4
D_FF = D_MODEL
SWIGLU_LIMIT = 7.0
SWIGLU_ALPHA = 1.702
NORM_EPS = 1e-6
NEG_BIG = -1e30

LANES = 128
SUBLANES = 8
VMEM_LIMIT = 52 * 1024 * 1024
MAIN_COLS = 2 * GDN_WIDTH + 2 * GDN_WIDTH + 3 * FOX_WIDTH
EXPERT_BLOCK = 256
EXPERT_BLOCK_LARGE = 512


def _cparams(sem):
    return pltpu.CompilerParams(dimension_semantics=sem, vmem_limit_bytes=VMEM_LIMIT)


def _dot(a, b):
    return jnp.dot(a.astype(BF16), b.astype(BF16), preferred_element_type=F32)


def _dot_nt(a, b):
    return lax.dot_general(a.astype(BF16), b.astype(BF16), (((1,), (1,)), ((), ())),
                           preferred_element_type=F32)


def _dot_tn(a, b):
    return lax.dot_general(a.astype(BF16), b.astype(BF16), (((0,), (0,)), ((), ())),
                           preferred_element_type=F32)


def _dot_hi(a, b):
    return jnp.dot(a, b, precision=HIGHEST, preferred_element_type=F32)


def _dot_nt_hi(a, b):
    return lax.dot_general(a, b, (((1,), (1,)), ((), ())), precision=HIGHEST,
                           preferred_element_type=F32)


def _rms(x, g):
    return x * lax.rsqrt(jnp.mean(x * x, axis=-1, keepdims=True) + NORM_EPS) * g


def _sigmoid(x):
    return 1.0 / (1.0 + jnp.exp(-x))


def _softplus(x):
    return jnp.maximum(x, 0.0) + jnp.log1p(jnp.exp(-jnp.abs(x)))


def _log_sigmoid(x):
    return jnp.minimum(x, 0.0) - jnp.log1p(jnp.exp(-jnp.abs(x)))


def _iota2(shape, dim):
    return lax.broadcasted_iota(I32, shape, dim)


ROW_TILE = D_MODEL // LANES


def _store_row_tiles(ref, val, n):
    for s in range(ROW_TILE):
        ref[pl.ds(s, n, stride=ROW_TILE), :] = val[:, s * LANES:(s + 1) * LANES]


def _load_row_tiles(ref, n):
    return jnp.concatenate([ref[pl.ds(s, n, stride=ROW_TILE), :] for s in range(ROW_TILE)], axis=-1)


def _in_proj_kernel(x_ref, g_ref, wm_ref, wkvt_ref, ws_ref, wst_ref,
                    qkv_ref, z_ref, fq_ref, fk_ref, fv_ref, sm_ref, smt_ref, *, kv_t):
    hb = _rms(x_ref[...], g_ref[...]).astype(BF16)
    nt = lambda w: lax.dot_general(w, hb, (((1,), (1,)), ((), ())), preferred_element_type=F32)
    c0 = 0
    for ref, width in ((qkv_ref, CONV_CH), (z_ref, GDN_WIDTH), (fq_ref, FOX_WIDTH)):
        ref[...] = jnp.dot(hb, wm_ref[:, c0:c0 + width], preferred_element_type=F32)
        c0 += width
    if kv_t:
        fk_ref[0] = nt(wkvt_ref[0:FOX_WIDTH, :])
        fv_ref[0] = nt(wkvt_ref[FOX_WIDTH:, :])
    else:
        fk_ref[...] = jnp.dot(hb, wm_ref[:, c0:c0 + FOX_WIDTH], preferred_element_type=F32)
        fv_ref[...] = jnp.dot(hb, wm_ref[:, c0 + FOX_WIDTH:], preferred_element_type=F32)
    sm_ref[...] = jnp.dot(hb, ws_ref[...], preferred_element_type=F32)
    smt_ref[...] = nt(wst_ref[...])


def _in_proj(x2, g, wm, wkvt, ws, wst, tm, kv_t_seq):
    t = x2.shape[0]
    row = lambda w: pl.BlockSpec((tm, w), lambda i: (i, 0))
    full = lambda a: pl.BlockSpec(a.shape, lambda i: (0,) * a.ndim)
    if kv_t_seq:
        per_b = kv_t_seq // tm
        kv_spec = pl.BlockSpec((1, FOX_WIDTH, tm), lambda i: (i // per_b, 0, i % per_b))
        kv_shape = jax.ShapeDtypeStruct((t // kv_t_seq, FOX_WIDTH, kv_t_seq), F32)
    else:
        kv_spec = row(FOX_WIDTH)
        kv_shape = jax.ShapeDtypeStruct((t, FOX_WIDTH), F32)
    return pl.pallas_call(
        functools.partial(_in_proj_kernel, kv_t=bool(kv_t_seq)),
        grid=(t // tm,),
        in_specs=[row(D_MODEL), full(g), full(wm), full(wkvt), full(ws), full(wst)],
        out_specs=[row(CONV_CH), row(GDN_WIDTH), row(FOX_WIDTH), kv_spec, kv_spec,
                   row(LANES), pl.BlockSpec((16, tm), lambda i: (0, i))],
        out_shape=[jax.ShapeDtypeStruct((t, CONV_CH), F32), jax.ShapeDtypeStruct((t, GDN_WIDTH), F32),
                   jax.ShapeDtypeStruct((t, FOX_WIDTH), F32), kv_shape, kv_shape,
                   jax.ShapeDtypeStruct((t, LANES), F32), jax.ShapeDtypeStruct((16, t), F32)],
        compiler_params=_cparams(("parallel",)),
    )(x2, g, wm, wkvt, ws, wst)


def _norm_proj_kernel(x_ref, g_ref, w_ref, *out_refs):
    hb = _rms(x_ref[...], g_ref[...]).astype(BF16)
    c0 = 0
    for ref in out_refs:
        width = ref.shape[-1]
        ref[...] = jnp.dot(hb, w_ref[:, c0:c0 + width], preferred_element_type=F32)
        c0 += width


def _norm_proj(x2, g, w, widths, tm):
    t = x2.shape[0]
    return pl.pallas_call(
        _norm_proj_kernel,
        grid=(t // tm,),
        in_specs=[pl.BlockSpec((tm, D_MODEL), lambda i: (i, 0)),
                  pl.BlockSpec(g.shape, lambda i: (0, 0)),
                  pl.BlockSpec(w.shape, lambda i: (0, 0))],
        out_specs=[pl.BlockSpec((tm, wd), lambda i: (i, 0)) for wd in widths],
        out_shape=[jax.ShapeDtypeStruct((t, wd), F32) for wd in widths],
        compiler_params=_cparams(("parallel",)),
    )(x2, g, w)


def _small_mm(a, b):
    acc = a[:, 0:1] * b[0:1, :]
    for i in range(1, a.shape[1]):
        acc = acc + a[:, i:i + 1] * b[i:i + 1, :]
    return acc


def _logf_kernel(smt_ref, fbc_ref, logf_ref, ct_ref, *, seq, chunk):
    lft = _log_sigmoid(smt_ref[0][8:16, :] + fbc_ref[8:16, 0:1])
    logf_ref[0] = lft
    tri = (_iota2((chunk, chunk), 0) <= _iota2((chunk, chunk), 1)).astype(F32)
    carry = jnp.zeros((FOX_HEADS, 1), F32)
    for c in range(seq // chunk):
        blk = lft[:, c * chunk:(c + 1) * chunk]
        cs = (_small_mm(blk, tri) if chunk <= SUBLANES else _dot_hi(blk, tri)) + carry
        ct_ref[0, :, c * chunk:(c + 1) * chunk] = cs
        carry = cs[:, chunk - 1:chunk]


def _logf(smt3, fbc):
    b, _, seq = smt3.shape
    chunk = min(256, seq)
    return pl.pallas_call(
        functools.partial(_logf_kernel, seq=seq, chunk=chunk),
        grid=(b,),
        in_specs=[pl.BlockSpec((1, 16, seq), lambda i: (i, 0, 0)),
                  pl.BlockSpec(fbc.shape, lambda i: (0, 0))],
        out_specs=[pl.BlockSpec((1, FOX_HEADS, seq), lambda i: (i, 0, 0)),
                   pl.BlockSpec((1, FOX_HEADS, seq), lambda i: (i, 0, 0))],
        out_shape=[jax.ShapeDtypeStruct((b, FOX_HEADS, seq), F32),
                   jax.ShapeDtypeStruct((b, FOX_HEADS, seq), F32)],
        compiler_params=_cparams(("parallel",)),
    )(smt3, fbc)


def _unit_lower_inverse(a, c, mm):
    eye = (_iota2((c, c), 0) == _iota2((c, c), 1)).astype(F32)
    n = -a
    t = eye + n
    p = n
    levels = int(math.log2(c))
    for _ in range(levels - 1):
        p = mm(p, p)
        t = t + mm(t, p)
    return t


def _gdn_kernel(qkv_ref, z_ref, sm_ref, smt_ref, cst_ref, s0_ref, cw_ref, gpr_ref, gpc_ref,
                o_ref, sn_ref, cb_ref, s_scr, tail_scr, *, tl, c, exact_small):
    t = pl.program_id(1)
    nt = pl.num_programs(1)

    @pl.when(t == 0)
    def _():
        s_scr[...] = s0_ref[0]
        tail_scr[...] = cst_ref[0]

    rb = (lambda v: v.astype(BF16).astype(F32)) if exact_small else (lambda v: v.astype(BF16))
    bdims = ((0,), (0,))
    mm = lambda a, b: lax.dot_general(rb(a), rb(b), (((2,), (1,)), bdims), preferred_element_type=F32)
    mm_nt = lambda a, b: lax.dot_general(rb(a), rb(b), (((2,), (2,)), bdims), preferred_element_type=F32)
    mm_tn = lambda a, b: lax.dot_general(rb(a), rb(b), (((1,), (1,)), bdims), preferred_element_type=F32)

    x = qkv_ref[0]
    tail = tail_scr[...]
    cw = cw_ref[...]
    row8 = _iota2((SUBLANES, CONV_CH), 0)
    acc = x * cw[CONV_W - 1:CONV_W, :]
    for s in range(1, CONV_W):
        xs = pltpu.roll(x, s, 0)
        head = jnp.where(row8 < s, pltpu.roll(tail, s, 0), xs[0:SUBLANES])
        xs = head if tl == SUBLANES else jnp.concatenate([head, xs[SUBLANES:]], axis=0)
        acc = acc + xs * cw[CONV_W - 1 - s:CONV_W - s, :]
    conv = acc * _sigmoid(acc)
    tail_scr[...] = x[tl - SUBLANES:tl, :]

    @pl.when(t == nt - 1)
    def _():
        cb_ref[0] = x[tl - (CONV_W - 1):tl, :]

    sm = sm_ref[0]
    beta_c = _sigmoid(sm)
    g_c = -jnp.exp(gpr_ref[0:1, :]) * _softplus(sm + gpr_ref[1:2, :])
    smt = smt_ref[0]
    g_r = -jnp.exp(gpc_ref[:, 0:1]) * _softplus(smt + gpc_ref[:, 1:2])
    onorm = gpr_ref[2:3, :]

    ii = _iota2((c, c), 0)
    jj = _iota2((c, c), 1)
    tri_c = (jj <= ii).astype(F32)
    tri_r = (ii <= jj).astype(F32)
    small = c <= SUBLANES
    nc = tl // c
    cum_cols, cum_rows = [], []
    for ci in range(nc):
        gc_blk = g_c[ci * c:(ci + 1) * c, :]
        gr_blk = g_r[:, ci * c:(ci + 1) * c]
        cum_cols.append(_small_mm(tri_c, gc_blk) if small else _dot_hi(tri_c, gc_blk))
        cum_rows.append(_small_mm(gr_blk, tri_r) if small else _dot_hi(gr_blk, tri_r))

    def per_pair(fn):
        return jnp.stack([fn(ci, h) for ci in range(nc) for h in range(GDN_HEADS)], axis=0)

    rows = lambda ci: slice(ci * c, (ci + 1) * c)
    qs = per_pair(lambda ci, h: conv[rows(ci), h * GDN_DK:(h + 1) * GDN_DK])
    ks = per_pair(lambda ci, h: conv[rows(ci), GDN_WIDTH + h * GDN_DK:GDN_WIDTH + (h + 1) * GDN_DK])
    vs = per_pair(lambda ci, h: conv[rows(ci), 2 * GDN_WIDTH + h * GDN_DV:2 * GDN_WIDTH + (h + 1) * GDN_DV])
    beta = per_pair(lambda ci, h: beta_c[rows(ci), h:h + 1])
    cum_c = per_pair(lambda ci, h: cum_cols[ci][:, GDN_HEADS + h:GDN_HEADS + h + 1])
    cum_r = per_pair(lambda ci, h: cum_rows[ci][GDN_HEADS + h:GDN_HEADS + h + 1, :])
    qs = qs * lax.rsqrt(jnp.sum(qs * qs, axis=-1, keepdims=True) + NORM_EPS) * (GDN_DK ** -0.5)
    ks = ks * lax.rsqrt(jnp.sum(ks * ks, axis=-1, keepdims=True) + NORM_EPS)
    cum_last = cum_c[:, c - 1:c, :]
    dec = jnp.exp(jnp.where(ii >= jj, cum_c - cum_r, NEG_BIG))
    dec_strict = jnp.where(ii > jj, dec, 0.0)
    a_mat = beta * mm_nt(ks, ks) * dec_strict
    tinv = _unit_lower_inverse(a_mat, c, mm)
    e_cum = jnp.exp(cum_c)
    rhs = jnp.concatenate([beta * vs, (beta * e_cum) * ks], axis=-1)
    sol = mm(tinv, rhs)
    w_v, w_k = sol[:, :, :GDN_DV], sol[:, :, GDN_DV:]
    p_qk = mm_nt(qs, ks) * dec
    q_g = qs * e_cum
    k_d = ks * jnp.exp(cum_last - cum_c)
    g_end = jnp.exp(cum_last)

    state = s_scr[...]
    for ci in range(nc):
        pr = slice(ci * GDN_HEADS, (ci + 1) * GDN_HEADS)
        u = w_v[pr] - mm(w_k[pr], state)
        o = mm(q_g[pr], state) + mm(p_qk[pr], u)
        state = g_end[pr] * state + mm_tn(k_d[pr], u)
        for h in range(GDN_HEADS):
            lo = h * GDN_DV
            zh = z_ref[0, rows(ci), lo:lo + GDN_DV]
            o_ref[0, rows(ci), lo:lo + GDN_DV] = _rms(o[h], onorm) * (zh * _sigmoid(zh))
    s_scr[...] = state

    @pl.when(t == nt - 1)
    def _():
        sn_ref[0] = state


def _gdn(qkv3, z3, sm3, smt3, cst8, s0, cw, gpr, gpc, tl, c, exact_small):
    b, seq, _ = qkv3.shape
    nt = seq // tl
    full2 = lambda a: pl.BlockSpec(a.shape, lambda i, j: (0, 0))
    return pl.pallas_call(
        functools.partial(_gdn_kernel, tl=tl, c=c, exact_small=exact_small),
        grid=(b, nt),
        in_specs=[pl.BlockSpec((1, tl, CONV_CH), lambda i, j: (i, j, 0)),
                  pl.BlockSpec((1, tl, GDN_WIDTH), lambda i, j: (i, j, 0)),
                  pl.BlockSpec((1, tl, LANES), lambda i, j: (i, j, 0)),
                  pl.BlockSpec((1, 16, tl), lambda i, j: (i, 0, j)),
                  pl.BlockSpec((1, SUBLANES, CONV_CH), lambda i, j: (i, 0, 0)),
                  pl.BlockSpec((1, GDN_HEADS, GDN_DK, GDN_DV), lambda i, j: (i, 0, 0, 0)),
                  full2(cw), full2(gpr), full2(gpc)],
        out_specs=[pl.BlockSpec((1, tl, GDN_WIDTH), lambda i, j: (i, j, 0)),
                   pl.BlockSpec((1, GDN_HEADS, GDN_DK, GDN_DV), lambda i, j: (i, 0, 0, 0)),
                   pl.BlockSpec((1, CONV_W - 1, CONV_CH), lambda i, j: (i, 0, 0))],
        out_shape=[jax.ShapeDtypeStruct((b, seq, GDN_WIDTH), F32),
                   jax.ShapeDtypeStruct((b, GDN_HEADS, GDN_DK, GDN_DV), F32),
                   jax.ShapeDtypeStruct((b, CONV_W - 1, CONV_CH), F32)],
        scratch_shapes=[pltpu.VMEM((GDN_HEADS, GDN_DK, GDN_DV), F32),
                        pltpu.VMEM((SUBLANES, CONV_CH), F32)],
        compiler_params=_cparams(("parallel", "arbitrary")),
    )(qkv3, z3, sm3, smt3, cst8, s0, cw, gpr, gpc)


FOX_HEAD_GROUP = 4


def _fox_prompt_kernel(fq_ref, fkt_ref, fvt_ref, ct_ref, on_ref, o_ref, kb_scr, vb_scr, *, tq):
    qi = pl.program_id(1)
    nk = kb_scr.shape[0]

    @pl.when(qi == 0)
    def _():
        for j in range(nk):
            kb_scr[j] = fkt_ref[0, :, j * tq:(j + 1) * tq].astype(BF16)
            vb_scr[j] = fvt_ref[0, :, j * tq:(j + 1) * tq].astype(BF16)

    q_all = fq_ref[0] * (FOX_HEAD_DIM ** -0.5)
    onorm = on_ref[...]
    causal = _iota2((tq, tq), 1) <= _iota2((tq, tq), 0)
    outs = []
    for g0 in range(0, FOX_HEADS, FOX_HEAD_GROUP):
        heads = range(g0, g0 + FOX_HEAD_GROUP)
        qs = [q_all[:, h * FOX_HEAD_DIM:(h + 1) * FOX_HEAD_DIM].astype(BF16) for h in heads]

        def step(j, carry, masked, heads=heads, qs=qs):
            new = []
            for (m, l, acc), h, qh in zip(carry, heads, qs):
                kt = kb_scr[j, h * FOX_HEAD_DIM:(h + 1) * FOX_HEAD_DIM, :]
                vt = vb_scr[j, h * FOX_HEAD_DIM:(h + 1) * FOX_HEAD_DIM, :]
                s = jnp.dot(qh, kt, preferred_element_type=F32) - ct_ref[0, h, pl.ds(j, 1), :]
                if masked:
                    s = jnp.where(causal, s, NEG_BIG)
                m_new = jnp.maximum(m, jnp.max(s, axis=-1, keepdims=True))
                p = jnp.exp(s - m_new)
                alpha = jnp.exp(m - m_new)
                l = alpha * l + jnp.sum(p, axis=-1, keepdims=True)
                acc = alpha * acc + lax.dot_general(p.astype(BF16), vt, (((1,), (1,)), ((), ())),
                                                    preferred_element_type=F32)
                new.append((m_new, l, acc))
            return tuple(new)

        init = tuple((jnp.full((tq, 1), NEG_BIG, F32), jnp.zeros((tq, 1), F32),
                      jnp.zeros((tq, FOX_HEAD_DIM), F32)) for _ in heads)
        carry = lax.fori_loop(0, qi, functools.partial(step, masked=False), init)
        for m, l, acc in step(qi, carry, True):
            outs.append(_rms(acc / l, onorm))
    o_ref[0] = jnp.concatenate(outs, axis=-1)


def _fox_prompt(fq3, fkt, fvt, ct4, onorm, tq):
    b, seq, _ = fq3.shape
    nq = seq // tq
    return pl.pallas_call(
        functools.partial(_fox_prompt_kernel, tq=tq),
        grid=(b, nq),
        in_specs=[pl.BlockSpec((1, tq, FOX_WIDTH), lambda i, j: (i, j, 0)),
                  pl.BlockSpec((1, FOX_WIDTH, seq), lambda i, j: (i, 0, 0)),
                  pl.BlockSpec((1, FOX_WIDTH, seq), lambda i, j: (i, 0, 0)),
                  pl.BlockSpec((1, FOX_HEADS, nq, tq), lambda i, j: (i, 0, 0, 0)),
                  pl.BlockSpec(onorm.shape, lambda i, j: (0, 0))],
        out_specs=pl.BlockSpec((1, tq, FOX_WIDTH), lambda i, j: (i, j, 0)),
        out_shape=jax.ShapeDtypeStruct((b, seq, FOX_WIDTH), F32),
        scratch_shapes=[pltpu.VMEM((nq, FOX_WIDTH, tq), BF16), pltpu.VMEM((nq, FOX_WIDTH, tq), BF16)],
        compiler_params=_cparams(("parallel", "arbitrary")),
    )(fq3, fkt, fvt, ct4, onorm)


def _logf_pages_kernel(lf_ref, w_ref, o_ref):
    lf = lf_ref[...]
    a = lf.astype(BF16)
    r1 = lf - a.astype(F32)
    b = r1.astype(BF16)
    c = (r1 - b.astype(F32)).astype(BF16)
    w = w_ref[...]
    o_ref[...] = (jnp.dot(a, w, preferred_element_type=F32) + jnp.dot(b, w, preferred_element_type=F32)
                  + jnp.dot(c, w, preferred_element_type=F32))


def _logf_pages(lf_flat, w2):
    n_pool, width = lf_flat.shape
    tp = next(c for c in (2048, 1024, 512, 256, 128, 64, 32, 16, 8) if n_pool % c == 0)
    return pl.pallas_call(
        _logf_pages_kernel,
        grid=(n_pool // tp,),
        in_specs=[pl.BlockSpec((tp, width), lambda i: (i, 0)),
                  pl.BlockSpec(w2.shape, lambda i: (0, 0))],
        out_specs=pl.BlockSpec((tp, 2 * width), lambda i: (i, 0)),
        out_shape=jax.ShapeDtypeStruct((n_pool, 2 * width), F32),
        compiler_params=_cparams(("parallel",)),
    )(lf_flat, w2)


def _fox_sample_kernel(pt_ref, qbd_ref, kn_ref, vn_ref, cn_ref, on_ref, *rest, pps, lq):
    k_refs = rest[0:pps]
    v_refs = rest[pps:2 * pps]
    r_refs = rest[2 * pps:3 * pps]
    o_ref = rest[3 * pps]
    m_scr, l_scr, acc_scr, suf_scr = rest[3 * pps + 1:]
    step = pl.program_id(1)
    nstep = pl.num_programs(1)
    rows = lq * FOX_HEADS
    qbd = qbd_ref[0]
    tile_rows = lambda v: jnp.concatenate([v] * lq, axis=0)

    @pl.when(step == 0)
    def _():
        s = lax.dot_general(qbd, kn_ref[0].astype(BF16), (((1,), (1,)), ((), ())),
                            preferred_element_type=F32)
        s = s - tile_rows(cn_ref[0])
        s = jnp.where(_iota2((rows, lq), 1) <= _iota2((rows, lq), 0) // FOX_HEADS, s, NEG_BIG)
        m = jnp.max(s, axis=-1, keepdims=True)
        p = jnp.exp(s - m)
        m_scr[...] = m
        l_scr[...] = jnp.sum(p, axis=-1, keepdims=True)
        acc_scr[...] = jnp.dot(p.astype(BF16), vn_ref[0].astype(BF16), preferred_element_type=F32)
        suf_scr[...] = jnp.zeros_like(suf_scr)

    page = k_refs[0].shape[3]
    suf = suf_scr[...]
    scores = []
    for i in range(pps):
        r2 = r_refs[i][0]
        s = jnp.dot(qbd, k_refs[i][0, 0].astype(BF16), preferred_element_type=F32)
        scores.append(s + tile_rows(r2[:, :page] + suf))
        suf = suf + r2[:, page:]
    suf_scr[...] = suf
    m = m_scr[...]
    m_new = m
    for s in scores:
        m_new = jnp.maximum(m_new, jnp.max(s, axis=-1, keepdims=True))
    alpha = jnp.exp(m - m_new)
    l = alpha * l_scr[...]
    acc = alpha * acc_scr[...]
    for i in range(pps):
        p = jnp.exp(scores[i] - m_new)
        l = l + jnp.sum(p, axis=-1, keepdims=True)
        acc = acc + lax.dot_general(p.astype(BF16), v_refs[i][0, 0].astype(BF16), (((1,), (1,)), ((), ())),
                                    preferred_element_type=F32)
    m_scr[...] = m_new
    l_scr[...] = l
    acc_scr[...] = acc

    @pl.when(step == nstep - 1)
    def _():
        own = _iota2((rows, FOX_WIDTH), 1) // FOX_HEAD_DIM == _iota2((rows, FOX_WIDTH), 0) % FOX_HEADS
        o = jnp.where(own, acc / l, 0.0)
        ms = jnp.sum(o * o, axis=-1, keepdims=True) * (1.0 / FOX_HEAD_DIM)
        o = o * lax.rsqrt(ms + NORM_EPS) * on_ref[...]
        o_ref[0] = jnp.sum(o.reshape(lq, FOX_HEADS, FOX_WIDTH), axis=1)


def _fox_sample(page_table, layer, qbd, kn3, vn3, cn3, onorm, ckt, cvt, r3, pps):
    b, lq, _ = kn3.shape
    n_pages = page_table.shape[1]
    page = ckt.shape[3]
    nstep = n_pages // pps
    rows = FOX_HEADS * lq

    def page_map(i):
        return lambda bi, s, pt: (layer, pt[bi, n_pages - 1 - (s * pps + i)], 0, 0)

    def r_map(i):
        return lambda bi, s, pt: (pt[bi, n_pages - 1 - (s * pps + i)], 0, 0)

    in_specs = [pl.BlockSpec((1, rows, FOX_WIDTH), lambda bi, s, pt: (bi, 0, 0)),
                pl.BlockSpec((1, lq, FOX_WIDTH), lambda bi, s, pt: (bi, 0, 0)),
                pl.BlockSpec((1, lq, FOX_WIDTH), lambda bi, s, pt: (bi, 0, 0)),
                pl.BlockSpec((1, FOX_HEADS, lq), lambda bi, s, pt: (bi, 0, 0)),
                pl.BlockSpec(onorm.shape, lambda bi, s, pt: (0, 0))]
    in_specs += [pl.BlockSpec((1, 1, FOX_WIDTH, page), page_map(i)) for i in range(pps)]
    in_specs += [pl.BlockSpec((1, 1, FOX_WIDTH, page), page_map(i)) for i in range(pps)]
    in_specs += [pl.BlockSpec((1, FOX_HEADS, 2 * page), r_map(i)) for i in range(pps)]
    grid_spec = pltpu.PrefetchScalarGridSpec(
        num_scalar_prefetch=1,
        grid=(b, nstep),
        in_specs=in_specs,
        out_specs=pl.BlockSpec((1, lq, FOX_WIDTH), lambda bi, s, pt: (bi, 0, 0)),
        scratch_shapes=[pltpu.VMEM((rows, 1), F32), pltpu.VMEM((rows, 1), F32),
                        pltpu.VMEM((rows, FOX_WIDTH), F32), pltpu.VMEM((FOX_HEADS, page), F32)],
    )
    return pl.pallas_call(
        functools.partial(_fox_sample_kernel, pps=pps, lq=lq),
        grid_spec=grid_spec,
        out_shape=jax.ShapeDtypeStruct((b, lq, FOX_WIDTH), F32),
        compiler_params=_cparams(("parallel", "arbitrary")),
    )(page_table, qbd, kn3, vn3, cn3, onorm, *([ckt] * pps), *([cvt] * pps), *([r3] * pps))


def _out_proj_kernel(x_ref, a_ref, b_ref, w_ref, o_ref):
    o_ref[...] = (x_ref[...]
                  + jnp.dot(a_ref[...].astype(BF16), w_ref[0:GDN_WIDTH, :], preferred_element_type=F32)
                  + jnp.dot(b_ref[...].astype(BF16), w_ref[GDN_WIDTH:, :], preferred_element_type=F32))


def _out_proj(x2, a2, b2, w, tm):
    t = x2.shape[0]
    return pl.pallas_call(
        _out_proj_kernel,
        grid=(t // tm,),
        in_specs=[pl.BlockSpec((tm, D_MODEL), lambda i: (i, 0)),
                  pl.BlockSpec((tm, GDN_WIDTH), lambda i: (i, 0)),
                  pl.BlockSpec((tm, FOX_WIDTH), lambda i: (i, 0)),
                  pl.BlockSpec(w.shape, lambda i: (0, 0))],
        out_specs=pl.BlockSpec((tm, D_MODEL), lambda i: (i, 0)),
        out_shape=jax.ShapeDtypeStruct((t, D_MODEL), F32),
        compiler_params=_cparams(("parallel",)),
    )(x2, a2, b2, w)


def _xattn_kernel(x_ref, g_ref, wq_ref, wo_ref, mk_ref, mv_ref, o_ref):
    x = x_ref[0]
    hb = _rms(x, g_ref[...]).astype(BF16)
    q = jnp.dot(hb, wq_ref[...], preferred_element_type=F32) * (XA_HEAD_DIM ** -0.5)
    mk = mk_ref[0].astype(BF16)
    mv = mv_ref[0].astype(BF16)
    outs = []
    for h in range(XA_HEADS):
        lo = h * XA_HEAD_DIM
        s = lax.dot_general(q[:, lo:lo + XA_HEAD_DIM].astype(BF16), mk[:, lo:lo + XA_HEAD_DIM],
                            (((1,), (1,)), ((), ())), preferred_element_type=F32)
        p = jnp.exp(s - jnp.max(s, axis=-1, keepdims=True))
        p = p / jnp.sum(p, axis=-1, keepdims=True)
        outs.append(jnp.dot(p.astype(BF16), mv[:, lo:lo + XA_HEAD_DIM], preferred_element_type=F32))
    o = jnp.concatenate(outs, axis=-1).astype(BF16)
    o_ref[0] = x + jnp.dot(o, wo_ref[...], preferred_element_type=F32)


def _xattn(x3, g, wq, wo, mk3, mv3, tq):
    b, seq, _ = x3.shape
    n_mem = mk3.shape[1]
    return pl.pallas_call(
        _xattn_kernel,
        grid=(b, seq // tq),
        in_specs=[pl.BlockSpec((1, tq, D_MODEL), lambda i, j: (i, j, 0)),
                  pl.BlockSpec(g.shape, lambda i, j: (0, 0)),
                  pl.BlockSpec(wq.shape, lambda i, j: (0, 0)),
                  pl.BlockSpec(wo.shape, lambda i, j: (0, 0)),
                  pl.BlockSpec((1, n_mem, XA_WIDTH), lambda i, j: (i, 0, 0)),
                  pl.BlockSpec((1, n_mem, XA_WIDTH), lambda i, j: (i, 0, 0))],
        out_specs=pl.BlockSpec((1, tq, D_MODEL), lambda i, j: (i, j, 0)),
        out_shape=jax.ShapeDtypeStruct((b, seq, D_MODEL), F32),
        compiler_params=_cparams(("parallel", "parallel")),
    )(x3, g, wq, wo, mk3, mv3)


def _router_kernel(x_ref, g_ref, wr_ref, br_ref, h_ref, route_ref, gate_ref, cnt_ref, base_scr, *, tm):
    i = pl.program_id(0)

    @pl.when(i == 0)
    def _():
        base_scr[...] = jnp.zeros_like(base_scr)

    h = _rms(x_ref[...], g_ref[...])
    _store_row_tiles(h_ref, h, tm)
    logits = _dot_hi(h, wr_ref[...]) + br_ref[...]
    lane = _iota2((tm, LANES), 1)
    lane_f = lane.astype(F32)
    vals, hots, idxs = [], [], []
    cur = logits
    for _ in range(TOP_K):
        mx = jnp.max(cur, axis=-1, keepdims=True)
        idx_f = jnp.min(jnp.where(cur == mx, lane_f, float(LANES)), axis=-1, keepdims=True)
        hot = lane_f == idx_f
        vals.append(mx)
        idxs.append(idx_f.astype(I32))
        hots.append(hot)
        cur = jnp.where(hot, -jnp.inf, cur)
    exps = [jnp.exp(v - vals[0]) for v in vals]
    denom = exps[0] + exps[1] + exps[2] + exps[3]
    member = (hots[0] | hots[1] | hots[2] | hots[3]).astype(F32)
    strict = (_iota2((tm, tm), 1) < _iota2((tm, tm), 0)).astype(BF16)
    before = jnp.dot(strict, member.astype(BF16), preferred_element_type=F32) + base_scr[...]
    route = jnp.zeros((tm, LANES), I32)
    gate = jnp.zeros((tm, LANES), F32)
    for k in range(TOP_K):
        rank = jnp.sum(jnp.where(hots[k], before, 0.0), axis=-1, keepdims=True).astype(I32)
        route = jnp.where(lane == k, idxs[k], route)
        route = jnp.where(lane == TOP_K + k, rank, route)
        gate = jnp.where(lane == k, exps[k] / denom, gate)
    route_ref[...] = route
    gate_ref[...] = gate
    base_scr[...] = base_scr[...] + jnp.sum(member, axis=0, keepdims=True)
    cnt_ref[...] = base_scr[...]


def _router(x2, g, wr, br, tm):
    t = x2.shape[0]
    return pl.pallas_call(
        functools.partial(_router_kernel, tm=tm),
        grid=(t // tm,),
        in_specs=[pl.BlockSpec((tm, D_MODEL), lambda i: (i, 0)),
                  pl.BlockSpec(g.shape, lambda i: (0, 0)),
                  pl.BlockSpec(wr.shape, lambda i: (0, 0)),
                  pl.BlockSpec(br.shape, lambda i: (0, 0))],
        out_specs=[pl.BlockSpec((tm * ROW_TILE, LANES), lambda i: (i, 0)),
                   pl.BlockSpec((tm, LANES), lambda i: (i, 0)),
                   pl.BlockSpec((tm, LANES), lambda i: (i, 0)),
                   pl.BlockSpec((1, LANES), lambda i: (0, 0))],
        out_shape=[jax.ShapeDtypeStruct((t * ROW_TILE, LANES), F32), jax.ShapeDtypeStruct((t, LANES), I32),
                   jax.ShapeDtypeStruct((t, LANES), F32), jax.ShapeDtypeStruct((1, LANES), F32)],
        scratch_shapes=[pltpu.VMEM((1, LANES), F32)],
        compiler_params=_cparams(("arbitrary",)),
    )(x2, g, wr, br)


def _dispatch_kernel(dest_ref, h_ref, init_hbm, xs_hbm, sem, *, tm):
    del init_hbm
    def issue(t, c):
        src = pl.multiple_of(t * ROW_TILE, ROW_TILE)
        for k in range(TOP_K):
            dst = pl.multiple_of(dest_ref[0, 0, t * TOP_K + k] * ROW_TILE, ROW_TILE)
            pltpu.make_async_copy(h_ref.at[pl.ds(src, ROW_TILE)], xs_hbm.at[pl.ds(dst, ROW_TILE)], sem).start()
        return c

    lax.fori_loop(0, tm, issue, 0, unroll=2)
    for _ in range(TOP_K):
        pltpu.make_async_copy(h_ref, xs_hbm.at[pl.ds(0, tm * ROW_TILE)], sem).wait()


def _dispatch(dest3, h2, rows, tm):
    t = h2.shape[0] // ROW_TILE
    init = jnp.zeros((rows * ROW_TILE, LANES), F32)
    return pl.pallas_call(
        functools.partial(_dispatch_kernel, tm=tm),
        grid=(t // tm,),
        in_specs=[pl.BlockSpec((1, 1, tm * TOP_K), lambda i: (i, 0, 0), memory_space=pltpu.SMEM),
                  pl.BlockSpec((tm * ROW_TILE, LANES), lambda i: (i, 0)),
                  pl.BlockSpec(memory_space=pl.ANY)],
        out_specs=pl.BlockSpec(memory_space=pl.ANY),
        out_shape=jax.ShapeDtypeStruct((rows * ROW_TILE, LANES), F32),
        scratch_shapes=[pltpu.SemaphoreType.DMA(())],
        input_output_aliases={2: 0},
        compiler_params=pltpu.CompilerParams(dimension_semantics=("arbitrary",), vmem_limit_bytes=VMEM_LIMIT,
                                             has_side_effects=True),
    )(dest3, h2, init)


def _ffn_kernel(be_ref, nv_ref, x_ref, w1_ref, b1_ref, w2_ref, b2_ref, o_ref, *, tb):
    i = pl.program_id(0)
    live = i * tb < nv_ref[0]

    @pl.when(live)
    def _():
        x = _load_row_tiles(x_ref, tb).astype(BF16)
        hb = jnp.dot(x, w1_ref[0, 0], preferred_element_type=F32) + b1_ref[0]
        glu = jnp.minimum(hb[:, :D_FF], SWIGLU_LIMIT)
        lin = jnp.clip(hb[:, D_FF:], -SWIGLU_LIMIT, SWIGLU_LIMIT)
        act = glu * _sigmoid(SWIGLU_ALPHA * glu) * (lin + 1.0)
        y = jnp.dot(act.astype(BF16), w2_ref[0, 0], preferred_element_type=F32) + b2_ref[0]
        _store_row_tiles(o_ref, y, tb)

    @pl.when(jnp.logical_not(live))
    def _():
        o_ref[...] = jnp.zeros_like(o_ref)


def _ffn(block_e, nvalid, xs, w1, b1, w2, b2, tb, layer):
    rows = xs.shape[0] // ROW_TILE
    nb = rows // tb
    blk = pl.BlockSpec((tb * ROW_TILE, LANES), lambda i, be, nv: (i, 0))
    grid_spec = pltpu.PrefetchScalarGridSpec(
        num_scalar_prefetch=2,
        grid=(nb,),
        in_specs=[blk,
                  pl.BlockSpec((1, 1, D_MODEL, 2 * D_FF), lambda i, be, nv: (layer, be[i], 0, 0)),
                  pl.BlockSpec((1, 1, 2 * D_FF), lambda i, be, nv: (be[i], 0, 0)),
                  pl.BlockSpec((1, 1, D_FF, D_MODEL), lambda i, be, nv: (layer, be[i], 0, 0)),
                  pl.BlockSpec((1, 1, D_MODEL), lambda i, be, nv: (be[i], 0, 0))],
        out_specs=blk,
    )
    return pl.pallas_call(
        functools.partial(_ffn_kernel, tb=tb),
        grid_spec=grid_spec,
        out_shape=jax.ShapeDtypeStruct((rows * ROW_TILE, LANES), F32),
        compiler_params=_cparams(("arbitrary",)),
    )(block_e, nvalid, xs, w1, b1, w2, b2)


def _combine_kernel(dcur_ref, dnext_ref, x_ref, gate_ref, fg_ref, os_hbm, y_ref, yn_ref, buf, sem, *, tm):
    i = pl.program_id(0)
    n = pl.num_programs(0)

    def issue_tile(dref, slot):
        def issue(t, c):
            dst = pl.multiple_of(t * ROW_TILE, ROW_TILE)
            for k in range(TOP_K):
                src = pl.multiple_of(dref[0, 0, t * TOP_K + k] * ROW_TILE, ROW_TILE)
                pltpu.make_async_copy(os_hbm.at[pl.ds(src, ROW_TILE)],
                                      buf.at[slot, k, pl.ds(dst, ROW_TILE)], sem.at[slot]).start()
            return c

        lax.fori_loop(0, tm, issue, 0, unroll=2)

    @pl.when(i == 0)
    def _():
        issue_tile(dcur_ref, 0)

    @pl.when(i + 1 < n)
    def _():
        issue_tile(dnext_ref, (i + 1) % 2)

    slot = i % 2
    for k in range(TOP_K):
        pltpu.make_async_copy(os_hbm.at[pl.ds(0, tm * ROW_TILE)], buf.at[slot, k], sem.at[slot]).wait()
    gate = gate_ref[...]
    x = x_ref[...]
    pieces = []
    for s in range(ROW_TILE):
        acc = x[:, s * LANES:(s + 1) * LANES]
        for k in range(TOP_K):
            acc = acc + gate[:, k:k + 1] * buf[slot, k, pl.ds(s, tm, stride=ROW_TILE), :]
        pieces.append(acc)
    y = jnp.concatenate(pieces, axis=-1)
    y_ref[...] = y
    yn_ref[...] = _rms(y, fg_ref[...])


def _combine(dest3, x2, gate, fg, os2, tm):
    t = x2.shape[0]
    nt = t // tm
    return pl.pallas_call(
        functools.partial(_combine_kernel, tm=tm),
        grid=(nt,),
        in_specs=[pl.BlockSpec((1, 1, tm * TOP_K), lambda i: (i, 0, 0), memory_space=pltpu.SMEM),
                  pl.BlockSpec((1, 1, tm * TOP_K), lambda i: (jnp.minimum(i + 1, nt - 1), 0, 0),
                               memory_space=pltpu.SMEM),
                  pl.BlockSpec((tm, D_MODEL), lambda i: (i, 0)),
                  pl.BlockSpec((tm, LANES), lambda i: (i, 0)),
                  pl.BlockSpec(fg.shape, lambda i: (0, 0)),
                  pl.BlockSpec(memory_space=pl.ANY)],
        out_specs=[pl.BlockSpec((tm, D_MODEL), lambda i: (i, 0)),
                   pl.BlockSpec((tm, D_MODEL), lambda i: (i, 0))],
        out_shape=[jax.ShapeDtypeStruct((t, D_MODEL), F32), jax.ShapeDtypeStruct((t, D_MODEL), F32)],
        scratch_shapes=[pltpu.VMEM((2, TOP_K, tm * ROW_TILE, LANES), F32), pltpu.SemaphoreType.DMA((2,))],
        compiler_params=_cparams(("arbitrary",)),
    )(dest3, dest3, x2, gate, fg, os2)


def _row_tile(t, want):
    tm = min(want, t)
    assert t % tm == 0
    return tm


def _mixer(x3, lw, conv_state8, s0, fox_fn, kv_t):
    b, seq, _ = x3.shape
    t = b * seq
    x2 = x3.reshape(t, D_MODEL)
    qkv, z, fq, fk, fv, sm, smt = _in_proj(x2, lw["norm1_g"], lw["w_main"], lw["w_kv_t"], lw["w_small"],
                                           lw["w_small_t"], _row_tile(t, 256), seq if kv_t else None)
    sm3 = sm.reshape(b, seq, LANES)
    smt3 = jnp.transpose(smt.reshape(16, b, seq), (1, 0, 2))
    logf_t, ct = _logf(smt3, lw["fb_col"])
    logf = jnp.transpose(logf_t, (0, 2, 1))
    c = min(GDN_CHUNK, seq)
    tl = min(256, seq)
    gdn_out, s_new, cbuf = _gdn(qkv.reshape(b, seq, CONV_CH), z.reshape(b, seq, GDN_WIDTH), sm3, smt3,
                                conv_state8, s0, lw["conv_w"], lw["gp_row"], lw["gp_col"], tl, c,
                                exact_small=(c <= SUBLANES))
    fq3 = fq.reshape(b, seq, FOX_WIDTH)
    if kv_t:
        fk3, fv3 = fk, fv
        as_out = lambda a: jnp.transpose(a.reshape(b, FOX_HEADS, FOX_HEAD_DIM, seq), (0, 3, 1, 2))
    else:
        fk3, fv3 = fk.reshape(b, seq, FOX_WIDTH), fv.reshape(b, seq, FOX_WIDTH)
        as_out = lambda a: a.reshape(b, seq, FOX_HEADS, FOX_HEAD_DIM)
    fox_out = fox_fn(fq3, fk3, fv3, ct)
    y2 = _out_proj(x2, gdn_out.reshape(t, GDN_WIDTH), fox_out.reshape(t, FOX_WIDTH), lw["w_out"],
                   _row_tile(t, 512))
    return y2.reshape(b, seq, D_MODEL), s_new, cbuf, as_out(fk3), as_out(fv3), logf


def _moe(x3, lw, final_g):
    b, seq, _ = x3.shape
    t = b * seq
    x2 = x3.reshape(t, D_MODEL)
    h2, route, gate, counts = _router(x2, lw["norm3_g"], lw["w_router"], lw["b_router"], _row_tile(t, 256))
    tb = EXPERT_BLOCK_LARGE if t * TOP_K >= N_EXPERTS * 4 * EXPERT_BLOCK_LARGE else EXPERT_BLOCK
    cnt = counts[0, :N_EXPERTS].astype(I32)
    padded = (cnt + tb - 1) // tb * tb
    pends = jnp.cumsum(padded)
    pstart = pends - padded
    nb = t * TOP_K // tb + N_EXPERTS
    rows = nb * tb
    block_row0 = jnp.arange(nb, dtype=I32) * tb
    block_e = jnp.minimum(jnp.sum((pends[None, :] <= block_row0[:, None]).astype(I32), axis=1), N_EXPERTS - 1)
    nvalid = pends[-1:].astype(I32)
    hot = route[:, :TOP_K, None] == jnp.arange(N_EXPERTS, dtype=I32)[None, None, :]
    dest = jnp.sum(jnp.where(hot, pstart[None, None, :], 0), axis=-1) + route[:, TOP_K:2 * TOP_K]
    tmd = _row_tile(t, 512)
    xs = _dispatch(dest.reshape(t // tmd, 1, tmd * TOP_K), h2, rows, tmd)
    os2 = _ffn(block_e, nvalid, xs, lw["w1"], lw["b1"], lw["w2"], lw["b2"], tb, lw["layer"])
    tmc = _row_tile(t, 256)
    y2, yn2 = _combine(dest.reshape(t // tmc, 1, tmc * TOP_K), x2, gate, final_g, os2, tmc)
    return y2.reshape(b, seq, D_MODEL), yn2.reshape(b, seq, D_MODEL)


def _prep_layer(l, norm1_g, w_in, conv_w, gdn_a_log, gdn_dt_bias, gdn_onorm, fox_fbias, fox_onorm, w_out,
                norm2_g, mem_norm_g, w_xq, w_mkv, w_xo, norm3_g, w_router, b_router, w1, b1, w2, b2):
    wi = w_in[l]
    gq, gk, gv, gz, gb, ga, fq, fk, fv, ff = _split_in(wi)
    w_main = jnp.concatenate([gq, gk, gv, gz, fq, fk, fv], axis=1).astype(BF16)
    w_kv_t = jnp.transpose(jnp.concatenate([fk, fv], axis=1)).astype(BF16)
    w_small = jnp.concatenate([gb, ga, ff, jnp.zeros((D_MODEL, LANES - 16), F32)], axis=1).astype(BF16)
    w_small_t = jnp.transpose(w_small[:, :16])
    lanes = lambda v, off: jnp.zeros((LANES,), F32).at[off:off + v.shape[0]].set(v)
    gp_row = jnp.zeros((SUBLANES, LANES), F32)
    gp_row = gp_row.at[0].set(lanes(gdn_a_log[l], GDN_HEADS)).at[1].set(lanes(gdn_dt_bias[l], GDN_HEADS))
    gp_row = gp_row.at[2].set(gdn_onorm[l])
    gp_col = jnp.zeros((16, LANES), F32)
    gp_col = gp_col.at[GDN_HEADS:2 * GDN_HEADS, 0].set(gdn_a_log[l]).at[GDN_HEADS:2 * GDN_HEADS, 1].set(gdn_dt_bias[l])
    fb_col = jnp.zeros((16, LANES), F32).at[8:16, 0].set(fox_fbias[l])
    wr = jnp.concatenate([w_router[l], jnp.zeros((D_MODEL, LANES - N_EXPERTS), F32)], axis=1)
    br = jnp.full((1, LANES), NEG_BIG, F32).at[0, :N_EXPERTS].set(b_router[l])
    return {
        "norm1_g": norm1_g[l].reshape(1, D_MODEL), "w_main": w_main, "w_kv_t": w_kv_t,
        "w_small": w_small, "w_small_t": w_small_t,
        "conv_w": conv_w[l], "gp_row": gp_row, "gp_col": gp_col, "fb_col": fb_col,
        "fox_onorm": fox_onorm[l].reshape(1, FOX_HEAD_DIM), "w_out": w_out[l].astype(BF16),
        "norm2_g": norm2_g[l].reshape(1, D_MODEL), "mem_norm_g": mem_norm_g[l].reshape(1, D_MODEL),
        "w_xq": w_xq[l].astype(BF16), "w_mkv": w_mkv[l].astype(BF16), "w_xo": w_xo[l].astype(BF16),
        "norm3_g": norm3_g[l].reshape(1, D_MODEL), "w_router": wr, "b_router": br,
        "layer": l, "w1": w1, "b1": b1[l].reshape(N_EXPERTS, 1, 2 * D_FF),
        "w2": w2, "b2": b2[l].reshape(N_EXPERTS, 1, D_MODEL),
    }


def _split_in(wi):
    widths = (GDN_WIDTH, GDN_WIDTH, GDN_WIDTH, GDN_WIDTH, GDN_HEADS, GDN_HEADS,
              FOX_WIDTH, FOX_WIDTH, FOX_WIDTH, FOX_HEADS)
    outs, c0 = [], 0
    for w in widths:
        outs.append(wi[:, c0:c0 + w])
        c0 += w
    return outs


def kernel(x_prompt, x_sample, mem_prompt, cache_fox_k, cache_fox_v, cache_fox_logf, cache_mem_k, cache_mem_v, state_gdn, state_conv, page_table, norm1_g, w_in, conv_w, gdn_a_log, gdn_dt_bias, gdn_onorm, fox_fbias, fox_onorm, w_out, norm2_g, mem_norm_g, w_xq, w_mkv, w_xo, norm3_g, w_router, b_router, w1, b1, w2, b2, final_norm_g):
    depth = w_in.shape[0]
    bp, lp, _ = x_prompt.shape
    bs, ls, _ = x_sample.shape
    n_mem = mem_prompt.shape[1]
    n_pool, page = cache_fox_k.shape[1], cache_fox_k.shape[2]
    ckt = jnp.transpose(cache_fox_k, (0, 1, 3, 4, 2)).reshape(depth, n_pool, FOX_WIDTH, page)
    cvt = jnp.transpose(cache_fox_v, (0, 1, 3, 4, 2)).reshape(depth, n_pool, FOX_WIDTH, page)
    lf_rows = jnp.transpose(cache_fox_logf, (0, 1, 3, 2)).reshape(depth, n_pool * FOX_HEADS, page)
    tok = jnp.arange(page, dtype=I32)
    later = tok[:, None] > tok[None, :]
    w_suffix = jnp.concatenate([later, jnp.ones_like(later)], axis=1).astype(BF16)
    head_mask = (jnp.arange(FOX_HEADS)[:, None] == jnp.arange(FOX_HEADS)[None, :]).astype(F32)
    final_g = final_norm_g.reshape(1, D_MODEL)
    xp, xs = x_prompt, x_sample
    yp = ys = None
    outs = {k: [] for k in ("fkp", "fvp", "flp", "mkp", "mvp", "sgp", "scp", "fks", "fvs", "fls", "sgs", "scs")}
    w1_bf16, w2_bf16 = w1.astype(BF16), w2.astype(BF16)
    for l in range(depth):
        lw = _prep_layer(l, norm1_g, w_in, conv_w, gdn_a_log, gdn_dt_bias, gdn_onorm, fox_fbias, fox_onorm,
                         w_out, norm2_g, mem_norm_g, w_xq, w_mkv, w_xo, norm3_g, w_router, b_router,
                         w1_bf16, b1, w2_bf16, b2)
        tq = min(512, lp)

        def fox_p(fq3, fk3, fv3, ct, lw=lw, tq=tq):
            return _fox_prompt(fq3, fk3, fv3, ct.reshape(bp, FOX_HEADS, lp // tq, tq), lw["fox_onorm"], tq)

        xp, s_new, cbuf, k_new, v_new, lf_new = _mixer(
            xp, lw, jnp.zeros((bp, SUBLANES, CONV_CH), F32),
            jnp.zeros((bp, GDN_HEADS, GDN_DK, GDN_DV), F32), fox_p, kv_t=True)
        outs["fkp"].append(k_new); outs["fvp"].append(v_new); outs["flp"].append(lf_new)
        outs["sgp"].append(s_new); outs["scp"].append(cbuf)

        r3 = _logf_pages(lf_rows[l], w_suffix).reshape(n_pool, FOX_HEADS, 2 * page)
        on_tiled = jnp.tile(lw["fox_onorm"], (1, FOX_HEADS))

        def fox_s(fq3, fk3, fv3, ct, lw=lw, l=l, r3=r3, on_tiled=on_tiled):
            q4 = fq3.reshape(bs, ls, FOX_HEADS, FOX_HEAD_DIM) * (FOX_HEAD_DIM ** -0.5)
            qbd = jnp.einsum("bqhd,hg->bqhgd", q4, head_mask).reshape(bs, ls * FOX_HEADS, FOX_WIDTH)
            return _fox_sample(page_table, l, qbd.astype(BF16), fk3, fv3, ct, on_tiled, ckt, cvt, r3,
                               pps=math.gcd(page_table.shape[1], 32))

        cst8 = jnp.pad(state_conv[l], ((0, 0), (SUBLANES - (CONV_W - 1), 0), (0, 0)))
        xs, s_new, cbuf, k_new, v_new, lf_new = _mixer(xs, lw, cst8, state_gdn[l], fox_s, kv_t=False)
        outs["fks"].append(k_new); outs["fvs"].append(v_new); outs["fls"].append(lf_new)
        outs["sgs"].append(s_new); outs["scs"].append(cbuf)

        mk2, mv2 = _norm_proj(mem_prompt.reshape(bp * n_mem, D_MODEL), lw["mem_norm_g"], lw["w_mkv"],
                              (XA_WIDTH, XA_WIDTH), _row_tile(bp * n_mem, 512))
        outs["mkp"].append(mk2.reshape(bp, n_mem, XA_HEADS, XA_HEAD_DIM))
        outs["mvp"].append(mv2.reshape(bp, n_mem, XA_HEADS, XA_HEAD_DIM))
        xp = _xattn(xp, lw["norm2_g"], lw["w_xq"], lw["w_xo"], mk2.reshape(bp, n_mem, XA_WIDTH),
                    mv2.reshape(bp, n_mem, XA_WIDTH), min(512, lp))
        xs = _xattn(xs, lw["norm2_g"], lw["w_xq"], lw["w_xo"], cache_mem_k[l].reshape(bs, n_mem, XA_WIDTH),
                    cache_mem_v[l].reshape(bs, n_mem, XA_WIDTH), ls)

        xp, yp = _moe(xp, lw, final_g)
        xs, ys = _moe(xs, lw, final_g)
    st = jnp.stack
    return (yp, ys, st(outs["fkp"]), st(outs["fvp"]), st(outs["flp"]), st(outs["mkp"]), st(outs["mvp"]),
            st(outs["sgp"]), st(outs["scp"]), st(outs["fks"]), st(outs["fvs"]), st(outs["fls"]),
            st(outs["sgs"]), st(outs["scs"]))
```
